```python
import math
import jax, jax.numpy as jnp
from jax import lax
import numpy as np

D_MODEL = 1024
BATCH = 4
SEQ = 8192
DEPTH = 2

MEM_LEN = 256
MIX_W = D_MODEL
GROUP_W = MIX_W // 4
N_FOX_HEADS = 4
N_SB_HEADS = 4
N_MLA_HEADS = 4
N_MEM_HEADS = 4
HEAD_DIM = GROUP_W // 4
MLA_Q_RANK = 256
MLA_KV_RANK = 128
MLA_NOPE = HEAD_DIM
MLA_ROPE = 32
MLA_V = GROUP_W // N_MLA_HEADS
ROPE_THETA = 10000.0
BLOCK_Q = 128
LN_EPS = 1e-5
RMS_EPS = 1e-6
FOX_FORGET_BIAS_INIT = 3.0
DEEPNORM_ALPHA = (2 * DEPTH) ** 0.25
DEEPNORM_BETA = (8 * DEPTH) ** -0.25
SPLIT_SIZES = (GROUP_W, GROUP_W, GROUP_W, N_FOX_HEADS,
               GROUP_W, GROUP_W, GROUP_W,
               MLA_Q_RANK, MLA_KV_RANK, MLA_ROPE,
               GROUP_W,
               MIX_W)
IN_COLS = sum(SPLIT_SIZES)

kernel_name = 'hybrid_fox_stickbreak_mla_memory_deepnorm'


def _layer_norm(x, g, b):
    xf = x.astype(jnp.float32)
    mu = jnp.mean(xf, axis=-1, keepdims=True)
    var = jnp.mean(jnp.square(xf - mu), axis=-1, keepdims=True)
    y = (xf - mu) * lax.rsqrt(var + LN_EPS) * g.astype(jnp.float32) + b.astype(jnp.float32)
    return y.astype(x.dtype)


def _rms_norm(x, g):
    xf = x.astype(jnp.float32)
    y = xf * lax.rsqrt(jnp.mean(jnp.square(xf), axis=-1, keepdims=True) + RMS_EPS)
    return (y * g.astype(jnp.float32)).astype(x.dtype)


def _heads(t, n):
    b, s, _ = t.shape
    return t.reshape(b, s, n, -1).transpose(0, 2, 1, 3)


def _merge(t):
    b, h, s, d = t.shape
    return t.transpose(0, 2, 1, 3).reshape(b, s, h * d)


def _rope(x, positions):
    half = x.shape[-1] // 2
    inv_freq = ROPE_THETA ** (-jnp.arange(half, dtype=jnp.float32) / half)
    ang = positions.astype(jnp.float32)[:, None] * inv_freq[None, :]
    ang = ang.reshape((ang.shape[0],) + (1,) * (x.ndim - 3) + (half,))
    cos, sin = jnp.cos(ang), jnp.sin(ang)
    xf = x.astype(jnp.float32)
    x1, x2 = xf[..., :half], xf[..., half:]
    return jnp.concatenate([x1 * cos - x2 * sin, x1 * sin + x2 * cos], axis=-1).astype(x.dtype)


def _sweep_query_blocks(block_fn, per_query):
    b, h, s = per_query[0].shape[:3]
    nb = s // BLOCK_Q
    blocks = tuple(jnp.moveaxis(a.reshape((b, h, nb, BLOCK_Q) + a.shape[3:]), 2, 0) for a in per_query)
    out = lax.map(lambda args: block_fn(args[0], *args[1:]), (jnp.arange(nb),) + blocks)
    out = jnp.moveaxis(out, 0, 2)
    return out.reshape(b, h, s, out.shape[-1])


def _causal_softmax_attention(q, k, v, scale, log_forget_cum=None):
    key_pos = jnp.arange(k.shape[2])

    def block(i, qb, *fq):
        q_pos = i * BLOCK_Q + jnp.arange(BLOCK_Q)
        logits = jnp.einsum('bhqd,bhkd->bhqk', qb, k).astype(jnp.float32) * scale
        if log_forget_cum is not None:
            logits = logits + fq[0][..., :, None] - log_forget_cum[:, :, None, :]
        mask = key_pos[None, :] <= q_pos[:, None]
        probs = jax.nn.softmax(jnp.where(mask, logits, -jnp.inf), axis=-1)
        return jnp.einsum('bhqk,bhkd->bhqd', probs.astype(v.dtype), v)

    per_query = (q,) if log_forget_cum is None else (q, log_forget_cum)
    return _sweep_query_blocks(block, per_query)


def _stick_breaking_attention(q, k, v, scale):
    key_pos = jnp.arange(k.shape[2])

    def block(i, qb):
        q_pos = i * BLOCK_Q + jnp.arange(BLOCK_Q)
        valid = key_pos[None, :] < q_pos[:, None]
        z = jnp.einsum('bhqd,bhkd->bhqk', qb, k).astype(jnp.float32) * scale
        log_keep = jnp.where(valid, jax.nn.log_sigmoid(-z), 0.0)
        log_tail = lax.cumsum(log_keep, axis=3, reverse=True) - log_keep
        w = jnp.where(valid, jnp.exp(jax.nn.log_sigmoid(z) + log_tail), 0.0)
        return jnp.einsum('bhqk,bhkd->bhqd', w.astype(v.dtype), v)

    return _sweep_query_blocks(block, (q,))


def setup_inputs(seed: int = 0) -> dict:
    key = jax.random.key(seed)
    ks = jax.random.split(key, 18)
    f32 = jnp.float32
    nrm = lambda k, shape: jax.random.normal(k, shape, f32)
    return {
        'x': nrm(ks[0], (BATCH, SEQ, D_MODEL)),
        'mem': nrm(ks[1], (BATCH, MEM_LEN, D_MODEL)),
        'ln_in_g': 1.0 + 0.02 * nrm(ks[2], (D_MODEL,)),
        'ln_in_b': 0.02 * nrm(ks[3], (D_MODEL,)),
        'mem_ln_g': 1.0 + 0.02 * nrm(ks[4], (D_MODEL,)),
        'mem_ln_b': 0.02 * nrm(ks[5], (D_MODEL,)),
        'w_in': nrm(ks[6], (DEPTH, D_MODEL, IN_COLS)) * D_MODEL ** -0.5,
        'b_forget': FOX_FORGET_BIAS_INIT + 0.1 * nrm(ks[7], (DEPTH, N_FOX_HEADS)),
        'mla_q_norm_g': 1.0 + 0.02 * nrm(ks[8], (DEPTH, MLA_Q_RANK)),
        'w_mla_q_up': nrm(ks[9], (DEPTH, MLA_Q_RANK, N_MLA_HEADS * (MLA_NOPE + MLA_ROPE))) * MLA_Q_RANK ** -0.5,
        'mla_kv_norm_g': 1.0 + 0.02 * nrm(ks[10], (DEPTH, MLA_KV_RANK)),
        'w_mla_kv_up': nrm(ks[11], (DEPTH, MLA_KV_RANK, N_MLA_HEADS * (MLA_NOPE + MLA_V))) * MLA_KV_RANK ** -0.5,
        'w_mem_kv': nrm(ks[12], (DEPTH, D_MODEL, 2 * GROUP_W)) * D_MODEL ** -0.5,
        'w_out': nrm(ks[13], (DEPTH, MIX_W, D_MODEL)) * (MIX_W ** -0.5 * DEEPNORM_BETA),
        'ln_g': 1.0 + 0.02 * nrm(ks[14], (DEPTH, D_MODEL)),
        'ln_b': 0.02 * nrm(ks[15], (DEPTH, D_MODEL)),
    }


def reference(x, mem, ln_in_g, ln_in_b, mem_ln_g, mem_ln_b, w_in, b_forget,
              mla_q_norm_g, w_mla_q_up, mla_kv_norm_g, w_mla_kv_up, w_mem_kv,
              w_out, ln_g, ln_b):
    b, s, _ = x.shape
    positions = jnp.arange(s)
    offsets = [int(o) for o in np.cumsum(SPLIT_SIZES)[:-1]]
    head_scale = HEAD_DIM ** -0.5
    mla_scale = (MLA_NOPE + MLA_ROPE) ** -0.5

    h_res = _layer_norm(x, ln_in_g, ln_in_b)
    mem_n = _layer_norm(mem, mem_ln_g, mem_ln_b)

    for l in range(DEPTH):
        proj = jnp.einsum('bsd,dc->bsc', h_res, w_in[l])
        (fq, fk, fv, f_logit, sq, sk, sv, c_q, c_kv, k_rot, mq, gate) = jnp.split(proj, offsets, axis=-1)

        log_f = jax.nn.log_sigmoid((f_logit + b_forget[l]).astype(jnp.float32))
        f_cum = jnp.cumsum(log_f, axis=1).transpose(0, 2, 1)
        out_fox = _causal_softmax_attention(_heads(fq, N_FOX_HEADS), _heads(fk, N_FOX_HEADS),
                                            _heads(fv, N_FOX_HEADS), head_scale, f_cum)

        out_sb = _stick_breaking_attention(_heads(sq, N_SB_HEADS), _heads(sk, N_SB_HEADS),
                                           _heads(sv, N_SB_HEADS), head_scale)

        q_mla = jnp.einsum('bsr,rc->bsc', _rms_norm(c_q, mla_q_norm_g[l]), w_mla_q_up[l])
        q_mla = q_mla.reshape(b, s, N_MLA_HEADS, MLA_NOPE + MLA_ROPE)
        q_full = jnp.concatenate([q_mla[..., :MLA_NOPE], _rope(q_mla[..., MLA_NOPE:], positions)], axis=-1)
        kv_mla = jnp.einsum('bsr,rc->bsc', _rms_norm(c_kv, mla_kv_norm_g[l]), w_mla_kv_up[l])
        kv_mla = kv_mla.reshape(b, s, N_MLA_HEADS, MLA_NOPE + MLA_V)
        k_rope = jnp.broadcast_to(_rope(k_rot, positions)[:, :, None, :], (b, s, N_MLA_HEADS, MLA_ROPE))
        k_full = jnp.concatenate([kv_mla[..., :MLA_NOPE], k_rope], axis=-1)
        v_mla = kv_mla[..., MLA_NOPE:]
        out_mla = _causal_softmax_attention(q_full.transpose(0, 2, 1, 3), k_full.transpose(0, 2, 1, 3),
                                            v_mla.transpose(0, 2, 1, 3), mla_scale)

        mkv = jnp.einsum('bmd,dc->bmc', mem_n, w_mem_kv[l])
        mk, mv = _heads(mkv[..., :GROUP_W], N_MEM_HEADS), _heads(mkv[..., GROUP_W:], N_MEM_HEADS)
        mem_logits = jnp.einsum('bhsd,bhmd->bhsm', _heads(mq, N_MEM_HEADS), mk).astype(jnp.float32) * head_scale
        mem_p = jax.nn.softmax(mem_logits, axis=-1)
        out_mem = jnp.einsum('bhsm,bhmd->bhsd', mem_p.astype(mv.dtype), mv)

        mixed = jnp.concatenate([_merge(out_fox), _merge(out_sb), _merge(out_mla), _merge(out_mem)], axis=-1)
        y = jnp.einsum('bsc,cd->bsd', mixed * jax.nn.silu(gate), w_out[l])

        h_res = _layer_norm(DEEPNORM_ALPHA * h_res + y, ln_g[l], ln_b[l])

    return h_res
```

```python
import functools
import math

import numpy as np
import jax
import jax.numpy as jnp
from jax import lax
from jax.experimental import pallas as pl
from jax.experimental.pallas import tpu as pltpu

N_HEADS = 4
HEAD_DIM = 64
GROUP_W = N_HEADS * HEAD_DIM
MLA_NOPE = 64
MLA_ROPE = 32
ROPE_THETA = 10000.0
LN_EPS = 1e-5
RMS_EPS = 1e-6
LOG2E = 1.4426950408889634

LANES = 128
VMEM_LIMIT_BYTES = 48 * 1024 * 1024

C_FQ, C_FK, C_FV = 0, 256, 512
C_SQ, C_SK, C_SV = 768, 1024, 1280
C_CQ = 1536
C_CKV = 1792
C_KRP = 1920
C_KRRP = 2048
C_MQ = 2176
C_GATE = 2432
C_MISC = 3456
W_COLS = 3584
FORGET_DUP = 6
FORGET_STRIDE = 8


def _cparams(n_grid):
    return pltpu.CompilerParams(dimension_semantics=("arbitrary",) * n_grid,
                                vmem_limit_bytes=VMEM_LIMIT_BYTES)


def _split3(x):
    p0 = x.astype(jnp.bfloat16)
    r1 = x - p0.astype(jnp.float32)
    p1 = r1.astype(jnp.bfloat16)
    r2 = r1 - p1.astype(jnp.float32)
    p2 = r2.astype(jnp.bfloat16)
    return p0, p1, p2


def _layer_norm_rows(x, g, b):
    mu = jnp.mean(x, axis=-1, keepdims=True)
    xc = x - mu
    var = jnp.mean(xc * xc, axis=-1, keepdims=True)
    return xc * lax.rsqrt(var + LN_EPS) * g + b


def _rms_norm_rows(x, g):
    ms = jnp.mean(x * x, axis=-1, keepdims=True)
    return x * lax.rsqrt(ms + RMS_EPS) * g


def _ln_kernel(x_ref, g_ref, b_ref, o_ref):
    o_ref[...] = _layer_norm_rows(x_ref[...], g_ref[...], b_ref[...])


def _layer_norm_call(x2d, g, b, tm):
    n, d = x2d.shape
    return pl.pallas_call(
        _ln_kernel,
        grid=(n // tm,),
        in_specs=[pl.BlockSpec((tm, d), lambda i: (i, 0)),
                  pl.BlockSpec((1, d), lambda i: (0, 0)),
                  pl.BlockSpec((1, d), lambda i: (0, 0))],
        out_specs=pl.BlockSpec((tm, d), lambda i: (i, 0)),
        out_shape=jax.ShapeDtypeStruct((n, d), jnp.float32),
        compiler_params=_cparams(1),
        name="ln_in",
    )(x2d, g.reshape(1, d), b.reshape(1, d))


def _mem_kv_kernel(mem_ref, g_ref, b_ref, w_ref, o_ref):
    mem_n = _layer_norm_rows(mem_ref[0], g_ref[...], b_ref[...]).astype(jnp.bfloat16)
    for l in range(w_ref.shape[0]):
        o_ref[l, 0] = jnp.dot(mem_n, w_ref[l], preferred_element_type=jnp.float32).astype(jnp.bfloat16)


def _mem_kv_call(mem, g, b, w_mem_kv_bf16):
    bsz, m, d = mem.shape
    depth, _, c = w_mem_kv_bf16.shape
    return pl.pallas_call(
        _mem_kv_kernel,
        grid=(bsz,),
        in_specs=[pl.BlockSpec((1, m, d), lambda i: (i, 0, 0)),
                  pl.BlockSpec((1, d), lambda i: (0, 0)),
                  pl.BlockSpec((1, d), lambda i: (0, 0)),
                  pl.BlockSpec((depth, d, c), lambda i: (0, 0, 0))],
        out_specs=pl.BlockSpec((depth, 1, m, c), lambda i: (0, i, 0, 0)),
        out_shape=jax.ShapeDtypeStruct((depth, bsz, m, c), jnp.bfloat16),
        compiler_params=_cparams(1),
        name="mem_kv",
    )(mem, g.reshape(1, d), b.reshape(1, d), w_mem_kv_bf16)


def _proj_kernel(h_ref, w_ref, bfor_ref, gq_ref, wqup_ref, gkv_ref, wkvup_ref,
                 cq_ref, sq_ref, ck_ref, sk_ref, mkv_ref,
                 fq_ref, fk_ref, fv_ref, sbq_ref, sbk_ref, sbv_ref,
                 mq_ref, mk_ref, mv_ref, omem_ref, gate_ref,
                 carry_ref):
    i = pl.program_id(1)
    tm = h_ref.shape[1]
    bf16, f32 = jnp.bfloat16, jnp.float32
    head_c = (HEAD_DIM ** -0.5) * LOG2E

    proj = jnp.dot(h_ref[0].astype(bf16), w_ref[...], preferred_element_type=f32)

    @pl.when(i == 0)
    def _():
        carry_ref[...] = jnp.zeros_like(carry_ref)

    lane = lax.broadcasted_iota(jnp.int32, (tm, LANES), 1)
    sub = lane % FORGET_STRIDE
    used = (lane < N_HEADS * FORGET_STRIDE) & (sub < FORGET_DUP)
    xf = proj[:, C_MISC:C_MISC + LANES] + bfor_ref[...]
    log_f = jnp.minimum(xf, 0.0) - jnp.log1p(jnp.exp(-jnp.abs(xf)))
    log_f = jnp.where(used, log_f, 0.0)
    row = lax.broadcasted_iota(jnp.int32, (tm, tm), 0)
    col = lax.broadcasted_iota(jnp.int32, (tm, tm), 1)
    tril = jnp.where(col <= row, 1.0, 0.0).astype(bf16)
    parts = jnp.concatenate(_split3(log_f), axis=1)
    csum = jnp.dot(tril, parts, preferred_element_type=f32)
    f_cum = (csum[:, :LANES] + csum[:, LANES:2 * LANES]) + csum[:, 2 * LANES:] + carry_ref[...]
    carry_ref[...] = f_cum[tm - 1:tm, :]
    p0, p1, p2 = _split3(f_cum * LOG2E)
    one = jnp.ones((tm, LANES), bf16)
    zero = jnp.zeros((tm, LANES), bf16)
    bias_q = jnp.where(sub == 0, p0, jnp.where(sub == 1, p1, jnp.where(sub == 2, p2, one)))
    bias_q = jnp.where(used, bias_q, zero)
    bias_k = jnp.where(sub == 3, -p0, jnp.where(sub == 4, -p1, jnp.where(sub == 5, -p2, one)))
    bias_k = jnp.where(used, bias_k, zero)

    for p in range(2):
        fq_ref[0, :, 2 * LANES * p:2 * LANES * p + LANES] = (
            proj[:, C_FQ + LANES * p:C_FQ + LANES * (p + 1)] * head_c).astype(bf16)
        fq_ref[0, :, 2 * LANES * p + LANES:2 * LANES * (p + 1)] = bias_q
        fk_ref[0, :, 2 * LANES * p:2 * LANES * p + LANES] = (
            proj[:, C_FK + LANES * p:C_FK + LANES * (p + 1)]).astype(bf16)
        fk_ref[0, :, 2 * LANES * p + LANES:2 * LANES * (p + 1)] = bias_k
    fv_ref[0] = proj[:, C_FV:C_FV + GROUP_W].astype(bf16)

    sbq_ref[0] = (proj[:, C_SQ:C_SQ + GROUP_W] * head_c).astype(bf16)
    sbk_ref[0] = proj[:, C_SK:C_SK + GROUP_W].astype(bf16)
    sbv_ref[0] = proj[:, C_SV:C_SV + GROUP_W].astype(bf16)

    cqn = _rms_norm_rows(proj[:, C_CQ:C_CQ + 256], gq_ref[...]).astype(bf16)
    q_up = jnp.dot(cqn, wqup_ref[...], preferred_element_type=f32)
    ckvn = _rms_norm_rows(proj[:, C_CKV:C_CKV + 128], gkv_ref[...]).astype(bf16)
    kv_up = jnp.dot(ckvn, wkvup_ref[...], preferred_element_type=f32)
    k_rope = (proj[:, C_KRP:C_KRP + LANES] * ck_ref[...]
              + proj[:, C_KRRP:C_KRRP + LANES] * sk_ref[...])
    nq = N_HEADS * LANES
    for hh in range(N_HEADS):
        sl = slice(LANES * hh, LANES * (hh + 1))
        mq_ref[0, :, sl] = (q_up[:, sl] * cq_ref[...]
                            + q_up[:, nq + LANES * hh:nq + LANES * (hh + 1)] * sq_ref[...]).astype(bf16)
        mk_ref[0, :, sl] = (kv_up[:, sl] + k_rope).astype(bf16)
    mv_ref[0] = kv_up[:, nq:nq + GROUP_W].astype(bf16)

    mem_q = proj[:, C_MQ:C_MQ + GROUP_W] * head_c
    lane_h = lax.broadcasted_iota(jnp.int32, (tm, LANES), 1)
    for p in range(2):
        qp = mem_q[:, LANES * p:LANES * (p + 1)]
        kp = mkv_ref[0, :, LANES * p:LANES * (p + 1)]
        vp = mkv_ref[0, :, GROUP_W + LANES * p:GROUP_W + LANES * (p + 1)]
        outs = []
        for hh in range(2):
            in_head = (lane_h >= HEAD_DIM * hh) & (lane_h < HEAD_DIM * (hh + 1))
            qh = jnp.where(in_head, qp, 0.0).astype(bf16)
            s = lax.dot_general(qh, kp, (((1,), (1,)), ((), ())), preferred_element_type=f32)
            m = jnp.max(s, axis=-1, keepdims=True)
            e = jnp.exp2(s - m)
            pr = e / jnp.sum(e, axis=-1, keepdims=True)
            outs.append(jnp.dot(pr.astype(bf16), vp, preferred_element_type=f32))
        omem_ref[0, :, LANES * p:LANES * (p + 1)] = jnp.where(lane_h < HEAD_DIM, outs[0], outs[1])

    g = proj[:, C_GATE:C_GATE + 4 * GROUP_W]
    gate_ref[0] = g / (1.0 + jnp.exp(-g))


def _proj_call(h, w_packed, bfor_row, gq, wqup, gkv, wkvup, tabs, mkv_l, tm):
    bsz, s, d = h.shape
    m = mkv_l.shape[1]
    cq, sq, ck, sk = tabs
    bf16 = jnp.bfloat16
    full2 = lambda shape: pl.BlockSpec(shape, lambda b, i: (0, 0))
    row_blk = lambda c: pl.BlockSpec((1, tm, c), lambda b, i: (b, i, 0))
    tab_blk = pl.BlockSpec((tm, LANES), lambda b, i: (i, 0))
    out_cols = [(2 * GROUP_W, bf16), (2 * GROUP_W, bf16), (GROUP_W, bf16),
                (GROUP_W, bf16), (GROUP_W, bf16), (GROUP_W, bf16),
                (2 * GROUP_W, bf16), (2 * GROUP_W, bf16), (GROUP_W, bf16),
                (GROUP_W, jnp.float32), (4 * GROUP_W, jnp.float32)]
    return pl.pallas_call(
        _proj_kernel,
        grid=(bsz, s // tm),
        in_specs=[row_blk(d),
                  full2(w_packed.shape),
                  full2((1, LANES)),
                  full2((1, 256)), full2(wqup.shape),
                  full2((1, 128)), full2(wkvup.shape),
                  tab_blk, tab_blk, tab_blk, tab_blk,
                  pl.BlockSpec((1, m, 2 * GROUP_W), lambda b, i: (b, 0, 0))],
        out_specs=[row_blk(c) for c, _ in out_cols],
        out_shape=[jax.ShapeDtypeStruct((bsz, s, c), dt) for c, dt in out_cols],
        scratch_shapes=[pltpu.VMEM((1, LANES), jnp.float32)],
        compiler_params=_cparams(2),
        name="proj",
    )(h, w_packed, bfor_row, gq, wqup, gkv, wkvup, cq, sq, ck, sk, mkv_l)


def _softmax_attn_kernel(q_ref, k_ref, v_ref, o_ref, m_sc, l_sc, acc_sc, *, fox, tq, tk):
    pair = pl.program_id(1)
    qi = pl.program_id(2)
    bf16, f32 = jnp.bfloat16, jnp.float32
    kw = q_ref.shape[2]
    lane_o = lax.broadcasted_iota(jnp.int32, (tq, LANES), 1)
    outs = []
    for hh in range(2):
        if fox:
            lane = lax.broadcasted_iota(jnp.int32, (tq, kw), 1)
            head = 2 * pair + hh
            lo = LANES + FORGET_STRIDE * head
            keep = ((lane >= HEAD_DIM * hh) & (lane < HEAD_DIM * (hh + 1))) | (
                (lane >= lo) & (lane < lo + FORGET_STRIDE))
            q = jnp.where(keep, q_ref[0], jnp.zeros((tq, kw), bf16))
            ksl = slice(0, kw)
        else:
            ksl = slice(LANES * hh, LANES * (hh + 1))
            q = q_ref[0, :, ksl]

        m_sc[...] = jnp.full(m_sc.shape, -jnp.inf, f32)
        l_sc[...] = jnp.zeros(l_sc.shape, f32)
        acc_sc[...] = jnp.zeros(acc_sc.shape, f32)

        def tile(kb, masked):
            k0 = pl.multiple_of(kb * tk, tk)
            k = k_ref[0, pl.ds(k0, tk), ksl]
            v = v_ref[0, pl.ds(k0, tk), :]
            s = lax.dot_general(q, k, (((1,), (1,)), ((), ())), preferred_element_type=f32)
            if masked:
                r = lax.broadcasted_iota(jnp.int32, (tq, tk), 0) + qi * tq
                c = lax.broadcasted_iota(jnp.int32, (tq, tk), 1) + k0
                s = jnp.where(c <= r, s, -jnp.inf)
            m_prev = m_sc[...]
            m_new = jnp.maximum(m_prev, jnp.max(s, axis=-1, keepdims=True))
            alpha = jnp.exp2(m_prev - m_new)
            p = jnp.exp2(s - m_new)
            l_sc[...] = alpha * l_sc[...] + jnp.sum(p, axis=-1, keepdims=True)
            acc_sc[...] = alpha * acc_sc[...] + jnp.dot(p.astype(bf16), v, preferred_element_type=f32)
            m_sc[...] = m_new

        n_diag = tq // tk
        for d in range(n_diag):
            tile(qi * n_diag + d, True)

        def body(kb, c):
            tile(kb, False)
            return c

        lax.fori_loop(0, qi * n_diag, body, 0)
        outs.append(acc_sc[...] / l_sc[...])
    o_ref[0] = jnp.where(lane_o < HEAD_DIM, outs[0], outs[1])


def _softmax_attn_call(q, k, v, *, fox, tq, tk, name):
    bsz, s, _ = q.shape
    kw = 2 * LANES
    kern = functools.partial(_softmax_attn_kernel, fox=fox, tq=tq, tk=tk)
    return pl.pallas_call(
        kern,
        grid=(bsz, 2, s // tq),
        in_specs=[pl.BlockSpec((1, tq, kw), lambda b, p, i: (b, i, p)),
                  pl.BlockSpec((1, s, kw), lambda b, p, i: (b, 0, p)),
                  pl.BlockSpec((1, s, LANES), lambda b, p, i: (b, 0, p))],
        out_specs=pl.BlockSpec((1, tq, LANES), lambda b, p, i: (b, i, p)),
        out_shape=jax.ShapeDtypeStruct((bsz, s, GROUP_W), jnp.float32),
        scratch_shapes=[pltpu.VMEM((tq, 1), jnp.float32),
                        pltpu.VMEM((tq, 1), jnp.float32),
                        pltpu.VMEM((tq, LANES), jnp.float32)],
        compiler_params=_cparams(3),
        name=name,
    )(q, k, v)


def _sb_attn_kernel(q_ref, k_ref, v_ref, o_ref, c_sc, acc_sc, *, tq, tk):
    qi = pl.program_id(2)
    bf16, f32 = jnp.bfloat16, jnp.float32
    lane_q = lax.broadcasted_iota(jnp.int32, (tq, LANES), 1)
    rj = lax.broadcasted_iota(jnp.int32, (tk, tk), 0)
    cs = lax.broadcasted_iota(jnp.int32, (tk, tk), 1)
    upper = jnp.where(rj >= cs, 1.0, 0.0).astype(bf16)
    outs = []
    for hh in range(2):
        in_head = (lane_q >= HEAD_DIM * hh) & (lane_q < HEAD_DIM * (hh + 1))
        q = jnp.where(in_head, q_ref[0], jnp.zeros((tq, LANES), bf16))
        c_sc[...] = jnp.zeros(c_sc.shape, f32)
        acc_sc[...] = jnp.zeros(acc_sc.shape, f32)

        def tile(kb, masked):
            k0 = pl.multiple_of(kb * tk, tk)
            k = k_ref[0, pl.ds(k0, tk), :]
            v = v_ref[0, pl.ds(k0, tk), :]
            z = lax.dot_general(q, k, (((1,), (1,)), ((), ())), preferred_element_type=f32)
            sp = jnp.maximum(z, 0.0) + jnp.log2(1.0 + jnp.exp2(-jnp.abs(z)))
            if masked:
                r = lax.broadcasted_iota(jnp.int32, (tq, tk), 0) + qi * tq
                c = lax.broadcasted_iota(jnp.int32, (tq, tk), 1) + k0
                valid = c < r
                sp = jnp.where(valid, sp, 0.0)
            csum = jnp.dot(sp.astype(bf16), upper, preferred_element_type=f32) + c_sc[...]
            w = jnp.exp2(z - csum)
            if masked:
                w = jnp.where(valid, w, 0.0)
            acc_sc[...] += jnp.dot(w.astype(bf16), v, preferred_element_type=f32)
            c_sc[...] = csum[:, 0:1]

        n_diag = tq // tk
        for d in range(n_diag):
            tile(qi * n_diag + (n_diag - 1 - d), True)

        def body(n, c):
            tile(qi * n_diag - 1 - n, False)
            return c

        lax.fori_loop(0, qi * n_diag, body, 0)
        outs.append(acc_sc[...])
    o_ref[0] = jnp.where(lane_q < HEAD_DIM, outs[0], outs[1])


def _sb_attn_call(q, k, v, *, tq, tk):
    bsz, s, _ = q.shape
    kern = functools.partial(_sb_attn_kernel, tq=tq, tk=tk)
    return pl.pallas_call(
        kern,
        grid=(bsz, 2, s // tq),
        in_specs=[pl.BlockSpec((1, tq, LANES), lambda b, p, i: (b, i, p)),
                  pl.BlockSpec((1, s, LANES), lambda b, p, i: (b, 0, p)),
                  pl.BlockSpec((1, s, LANES), lambda b, p, i: (b, 0, p))],
        out_specs=pl.BlockSpec((1, tq, LANES), lambda b, p, i: (b, i, p)),
        out_shape=jax.ShapeDtypeStruct((bsz, s, GROUP_W), jnp.float32),
        scratch_shapes=[pltpu.VMEM((tq, 1), jnp.float32),
                        pltpu.VMEM((tq, LANES), jnp.float32)],
        compiler_params=_cparams(3),
        name="sb_attn",
    )(q, k, v)


def _out_kernel(of_ref, os_ref, om_ref, ox_ref, gate_ref, h_ref, w_ref, g_ref, b_ref, o_ref, *, alpha):
    bf16, f32 = jnp.bfloat16, jnp.float32
    mixed = jnp.concatenate([of_ref[...], os_ref[...], om_ref[...], ox_ref[...]], axis=1)
    y = jnp.dot((mixed * gate_ref[...]).astype(bf16), w_ref[...], preferred_element_type=f32)
    o_ref[...] = _layer_norm_rows(alpha * h_ref[...] + y, g_ref[...], b_ref[...])


def _out_call(o_fox, o_sb, o_mla, o_mem, gate, h2d, w_out_bf16, g, b, alpha, tm):
    n, d = h2d.shape
    gw = o_fox.shape[1]
    kern = functools.partial(_out_kernel, alpha=alpha)
    blk = lambda c: pl.BlockSpec((tm, c), lambda i: (i, 0))
    return pl.pallas_call(
        kern,
        grid=(n // tm,),
        in_specs=[blk(gw), blk(gw), blk(gw), blk(gw), blk(4 * gw), blk(d),
                  pl.BlockSpec(w_out_bf16.shape, lambda i: (0, 0)),
                  pl.BlockSpec((1, d), lambda i: (0, 0)),
                  pl.BlockSpec((1, d), lambda i: (0, 0))],
        out_specs=blk(d),
        out_shape=jax.ShapeDtypeStruct((n, d), jnp.float32),
        compiler_params=_cparams(1),
        name="out_proj",
    )(o_fox, o_sb, o_mla, o_mem, gate, h2d, w_out_bf16, g.reshape(1, d), b.reshape(1, d))


def _pack_w_in(w):
    d = w.shape[0]
    o = 0
    cols = {}
    for name, width in (("fq", 256), ("fk", 256), ("fv", 256), ("fl", 4), ("sq", 256), ("sk", 256),
                        ("sv", 256), ("cq", 256), ("ckv", 128), ("kr", 32), ("mq", 256), ("gate", 1024)):
        cols[name] = w[:, o:o + width]
        o += width
    z = lambda n: jnp.zeros((d, n), w.dtype)
    half = MLA_ROPE // 2
    kr = cols["kr"]
    kr_rot = jnp.concatenate([-kr[:, half:], kr[:, :half]], axis=1)
    krp = jnp.concatenate([z(MLA_NOPE), kr, z(LANES - MLA_NOPE - MLA_ROPE)], axis=1)
    krrp = jnp.concatenate([z(MLA_NOPE), kr_rot, z(LANES - MLA_NOPE - MLA_ROPE)], axis=1)
    misc = []
    for hh in range(N_HEADS):
        misc += [cols["fl"][:, hh:hh + 1]] * FORGET_DUP + [z(FORGET_STRIDE - FORGET_DUP)]
    misc.append(z(LANES - N_HEADS * FORGET_STRIDE))
    packed = jnp.concatenate([cols["fq"], cols["fk"], cols["fv"], cols["sq"], cols["sk"], cols["sv"],
                              cols["cq"], cols["ckv"], krp, krrp, cols["mq"], cols["gate"]] + misc, axis=1)
    assert packed.shape[1] == W_COLS
    return packed.astype(jnp.bfloat16)


def _pack_forget_bias(b_forget_l):
    row = jnp.zeros((LANES,), jnp.float32)
    for hh in range(N_HEADS):
        row = row.at[FORGET_STRIDE * hh:FORGET_STRIDE * hh + FORGET_DUP].set(b_forget_l[hh])
    return row.reshape(1, LANES)


def _pack_mla_q_up(w):
    r = w.shape[0]
    per = MLA_NOPE + MLA_ROPE
    half = MLA_ROPE // 2
    z = lambda n: jnp.zeros((r, n), w.dtype)
    plain, rot = [], []
    for hh in range(N_HEADS):
        nope = w[:, per * hh:per * hh + MLA_NOPE]
        rope = w[:, per * hh + MLA_NOPE:per * (hh + 1)]
        rope_rot = jnp.concatenate([-rope[:, half:], rope[:, :half]], axis=1)
        plain += [nope, rope, z(LANES - per)]
        rot += [z(MLA_NOPE), rope_rot, z(LANES - per)]
    return jnp.concatenate(plain + rot, axis=1).astype(jnp.bfloat16)


def _pack_mla_kv_up(w):
    r = w.shape[0]
    per = MLA_NOPE + HEAD_DIM
    z = jnp.zeros((r, LANES - MLA_NOPE), w.dtype)
    ks, vs = [], []
    for hh in range(N_HEADS):
        ks += [w[:, per * hh:per * hh + MLA_NOPE], z]
        vs.append(w[:, per * hh + MLA_NOPE:per * (hh + 1)])
    return jnp.concatenate(ks + vs, axis=1).astype(jnp.bfloat16)


def _rope_tables(s, q_scale):
    half = MLA_ROPE // 2
    inv_freq = ROPE_THETA ** (-jnp.arange(half, dtype=jnp.float32) / half)
    ang = jnp.arange(s).astype(jnp.float32)[:, None] * inv_freq[None, :]
    cos, sin = jnp.cos(ang), jnp.sin(ang)
    ones = jnp.ones((s, MLA_NOPE), jnp.float32)
    z_nope = jnp.zeros((s, MLA_NOPE), jnp.float32)
    z_pad = jnp.zeros((s, LANES - MLA_NOPE - MLA_ROPE), jnp.float32)
    cos_q = jnp.concatenate([ones, cos, cos, z_pad], axis=1) * q_scale
    sin_q = jnp.concatenate([z_nope, sin, sin, z_pad], axis=1) * q_scale
    cos_k = jnp.concatenate([z_nope, cos, cos, z_pad], axis=1)
    sin_k = jnp.concatenate([z_nope, sin, sin, z_pad], axis=1)
    return cos_q, sin_q, cos_k, sin_k


def kernel(x, mem, ln_in_g, ln_in_b, mem_ln_g, mem_ln_b, w_in, b_forget, mla_q_norm_g, w_mla_q_up,
           mla_kv_norm_g, w_mla_kv_up, w_mem_kv, w_out, ln_g, ln_b):
    bsz, s, d = x.shape
    depth = w_in.shape[0]
    alpha = (2 * depth) ** 0.25
    tm = min(512, s)
    tq = min(512, s)
    tk_sm = min(512, s)
    tk_sb = min(256, s)
    mla_scale = (MLA_NOPE + MLA_ROPE) ** -0.5
    tabs = _rope_tables(s, mla_scale * LOG2E)

    h = _layer_norm_call(x.reshape(bsz * s, d), ln_in_g, ln_in_b, tm).reshape(bsz, s, d)
    mkv = _mem_kv_call(mem, mem_ln_g, mem_ln_b, w_mem_kv.astype(jnp.bfloat16))

    for l in range(depth):
        (fq, fk, fv, sbq, sbk, sbv, mq, mk, mv, o_mem, gate) = _proj_call(
            h, _pack_w_in(w_in[l]), _pack_forget_bias(b_forget[l]),
            mla_q_norm_g[l].reshape(1, -1), _pack_mla_q_up(w_mla_q_up[l]),
            mla_kv_norm_g[l].reshape(1, -1), _pack_mla_kv_up(w_mla_kv_up[l]),
            tabs, mkv[l], tm)
        o_fox = _softmax_attn_call(fq, fk, fv, fox=True, tq=tq, tk=tk_sm, name="fox_attn")
        o_sb = _sb_attn_call(sbq, sbk, sbv, tq=tq, tk=tk_sb)
        o_mla = _softmax_attn_call(mq, mk, mv, fox=False, tq=tq, tk=tk_sm, name="mla_attn")
        n = bsz * s
        h = _out_call(o_fox.reshape(n, -1), o_sb.reshape(n, -1), o_mla.reshape(n, -1), o_mem.reshape(n, -1),
                      gate.reshape(n, -1), h.reshape(n, d), w_out[l].astype(jnp.bfloat16),
                      ln_g[l], ln_b[l], alpha, tm).reshape(bsz, s, d)
    return h
```

```python
import functools
import math

import numpy as np
import jax
import jax.numpy as jnp
from jax import lax
from jax.experimental import pallas as pl
from jax.experimental.pallas import tpu as pltpu

N_HEADS = 4
HEAD_DIM = 64
GROUP_W = N_HEADS * HEAD_DIM
MLA_NOPE = 64
MLA_ROPE = 32
ROPE_THETA = 10000.0
LN_EPS = 1e-5
RMS_EPS = 1e-6
LOG2E = 1.4426950408889634

LANES = 128
VMEM_LIMIT_BYTES = 48 * 1024 * 1024

C_FQ, C_FK, C_FV = 0, 256, 512
C_SQ, C_SK, C_SV = 768, 1024, 1280
C_CQ = 1536
C_CKV = 1792
C_KRP = 1920
C_KRRP = 2048
C_MQ = 2176
C_GATE = 2432
C_MISC = 3456
W_COLS = 3584
FORGET_DUP = 6
FORGET_STRIDE = 8


def _cparams(n_grid):
    return pltpu.CompilerParams(dimension_semantics=("arbitrary",) * n_grid,
                                vmem_limit_bytes=VMEM_LIMIT_BYTES)


def _split3(x):
    p0 = x.astype(jnp.bfloat16)
    r1 = x - p0.astype(jnp.float32)
    p1 = r1.astype(jnp.bfloat16)
    r2 = r1 - p1.astype(jnp.float32)
    p2 = r2.astype(jnp.bfloat16)
    return p0, p1, p2


def _layer_norm_rows(x, g, b):
    mu = jnp.mean(x, axis=-1, keepdims=True)
    xc = x - mu
    var = jnp.mean(xc * xc, axis=-1, keepdims=True)
    return xc * lax.rsqrt(var + LN_EPS) * g + b


def _rms_norm_rows(x, g):
    ms = jnp.mean(x * x, axis=-1, keepdims=True)
    return x * lax.rsqrt(ms + RMS_EPS) * g


def _ln_kernel(x_ref, g_ref, b_ref, o_ref):
    o_ref[...] = _layer_norm_rows(x_ref[...], g_ref[...], b_ref[...])


def _layer_norm_call(x2d, g, b, tm):
    n, d = x2d.shape
    return pl.pallas_call(
        _ln_kernel,
        grid=(n // tm,),
        in_specs=[pl.BlockSpec((tm, d), lambda i: (i, 0)),
                  pl.BlockSpec((1, d), lambda i: (0, 0)),
                  pl.BlockSpec((1, d), lambda i: (0, 0))],
        out_specs=pl.BlockSpec((tm, d), lambda i: (i, 0)),
        out_shape=jax.ShapeDtypeStruct((n, d), jnp.float32),
        compiler_params=_cparams(1),
        name="ln_in",
    )(x2d, g.reshape(1, d), b.reshape(1, d))


def _mem_kv_kernel(mem_ref, g_ref, b_ref, w_ref, o_ref):
    mem_n = _layer_norm_rows(mem_ref[0], g_ref[...], b_ref[...]).astype(jnp.bfloat16)
    for l in range(w_ref.shape[0]):
        o_ref[l, 0] = jnp.dot(mem_n, w_ref[l], preferred_element_type=jnp.float32).astype(jnp.bfloat16)


def _mem_kv_call(mem, g, b, w_mem_kv_bf16):
    bsz, m, d = mem.shape
    depth, _, c = w_mem_kv_bf16.shape
    return pl.pallas_call(
        _mem_kv_kernel,
        grid=(bsz,),
        in_specs=[pl.BlockSpec((1, m, d), lambda i: (i, 0, 0)),
                  pl.BlockSpec((1, d), lambda i: (0, 0)),
                  pl.BlockSpec((1, d), lambda i: (0, 0)),
                  pl.BlockSpec((depth, d, c), lambda i: (0, 0, 0))],
        out_specs=pl.BlockSpec((depth, 1, m, c), lambda i: (0, i, 0, 0)),
        out_shape=jax.ShapeDtypeStruct((depth, bsz, m, c), jnp.bfloat16),
        compiler_params=_cparams(1),
        name="mem_kv",
    )(mem, g.reshape(1, d), b.reshape(1, d), w_mem_kv_bf16)


def _proj_kernel(h_ref, w_ref, bfor_ref, gq_ref, wqup_ref, gkv_ref, wkvup_ref,
                 cq_ref, sq_ref, ck_ref, sk_ref, mkv_ref,
                 fq_ref, fk_ref, fv_ref, sbq_ref, sbk_ref, sbv_ref,
                 mq_ref, mk_ref, mv_ref, omem_ref, gate_ref,
                 carry_ref):
    i = pl.program_id(1)
    tm = h_ref.shape[1]
    bf16, f32 = jnp.bfloat16, jnp.float32
    head_c = (HEAD_DIM ** -0.5) * LOG2E

    proj = jnp.dot(h_ref[0].astype(bf16), w_ref[...], preferred_element_type=f32)

    @pl.when(i == 0)
    def _():
        carry_ref[...] = jnp.zeros_like(carry_ref)

    lane = lax.broadcasted_iota(jnp.int32, (tm, LANES), 1)
    sub = lane % FORGET_STRIDE
    used = (lane < N_HEADS * FORGET_STRIDE) & (sub < FORGET_DUP)
    xf = proj[:, C_MISC:C_MISC + LANES] + bfor_ref[...]
    log_f = jnp.minimum(xf, 0.0) - jnp.log1p(jnp.exp(-jnp.abs(xf)))
    log_f = jnp.where(used, log_f, 0.0)
    row = lax.broadcasted_iota(jnp.int32, (tm, tm), 0)
    col = lax.broadcasted_iota(jnp.int32, (tm, tm), 1)
    tril = jnp.where(col <= row, 1.0, 0.0).astype(bf16)
    parts = jnp.concatenate(_split3(log_f), axis=1)
    csum = jnp.dot(tril, parts, preferred_element_type=f32)
    f_cum = (csum[:, :LANES] + csum[:, LANES:2 * LANES]) + csum[:, 2 * LANES:] + carry_ref[...]
    carry_ref[...] = f_cum[tm - 1:tm, :]
    p0, p1, p2 = _split3(f_cum * LOG2E)
    one = jnp.ones((tm, LANES), bf16)
    zero = jnp.zeros((tm, LANES), bf16)
    bias_q = jnp.where(sub == 0, p0, jnp.where(sub == 1, p1, jnp.where(sub == 2, p2, one)))
    bias_q = jnp.where(used, bias_q, zero)
    bias_k = jnp.where(sub == 3, -p0, jnp.where(sub == 4, -p1, jnp.where(sub == 5, -p2, one)))
    bias_k = jnp.where(used, bias_k, zero)

    for p in range(2):
        fq_ref[0, :, 2 * LANES * p:2 * LANES * p + LANES] = (
            proj[:, C_FQ + LANES * p:C_FQ + LANES * (p + 1)] * head_c).astype(bf16)
        fq_ref[0, :, 2 * LANES * p + LANES:2 * LANES * (p + 1)] = bias_q
        fk_ref[0, :, 2 * LANES * p:2 * LANES * p + LANES] = (
            proj[:, C_FK + LANES * p:C_FK + LANES * (p + 1)]).astype(bf16)
        fk_ref[0, :, 2 * LANES * p + LANES:2 * LANES * (p + 1)] = bias_k
    fv_ref[0] = proj[:, C_FV:C_FV + GROUP_W].astype(bf16)

    sbq_ref[0] = (proj[:, C_SQ:C_SQ + GROUP_W] * head_c).astype(bf16)
    sbk_ref[0] = proj[:, C_SK:C_SK + GROUP_W].astype(bf16)
    sbv_ref[0] = proj[:, C_SV:C_SV + GROUP_W].astype(bf16)

    cqn = _rms_norm_rows(proj[:, C_CQ:C_CQ + 256], gq_ref[...]).astype(bf16)
    q_up = jnp.dot(cqn, wqup_ref[...], preferred_element_type=f32)
    ckvn = _rms_norm_rows(proj[:, C_CKV:C_CKV + 128], gkv_ref[...]).astype(bf16)
    kv_up = jnp.dot(ckvn, wkvup_ref[...], preferred_element_type=f32)
    k_rope = (proj[:, C_KRP:C_KRP + LANES] * ck_ref[...]
              + proj[:, C_KRRP:C_KRRP + LANES] * sk_ref[...])
    nq = N_HEADS * LANES
    for hh in range(N_HEADS):
        sl = slice(LANES * hh, LANES * (hh + 1))
        mq_ref[0, :, sl] = (q_up[:, sl] * cq_ref[...]
                            + q_up[:, nq + LANES * hh:nq + LANES * (hh + 1)] * sq_ref[...]).astype(bf16)
        mk_ref[0, :, sl] = (kv_up[:, sl] + k_rope).astype(bf16)
    mv_ref[0] = kv_up[:, nq:nq + GROUP_W].astype(bf16)

    mem_q = proj[:, C_MQ:C_MQ + GROUP_W] * head_c
    lane_h = lax.broadcasted_iota(jnp.int32, (tm, LANES), 1)
    for p in range(2):
        qp = mem_q[:, LANES * p:LANES * (p + 1)]
        kp = mkv_ref[0, :, LANES * p:LANES * (p + 1)]
        vp = mkv_ref[0, :, GROUP_W + LANES * p:GROUP_W + LANES * (p + 1)]
        outs = []
        for hh in range(2):
            in_head = (lane_h >= HEAD_DIM * hh) & (lane_h < HEAD_DIM * (hh + 1))
            qh = jnp.where(in_head, qp, 0.0).astype(bf16)
            s = lax.dot_general(qh, kp, (((1,), (1,)), ((), ())), preferred_element_type=f32)
            m = jnp.max(s, axis=-1, keepdims=True)
            e = jnp.exp2(s - m)
            pr = e / jnp.sum(e, axis=-1, keepdims=True)
            outs.append(jnp.dot(pr.astype(bf16), vp, preferred_element_type=f32))
        omem_ref[0, :, LANES * p:LANES * (p + 1)] = jnp.where(lane_h < HEAD_DIM, outs[0], outs[1])

    g = proj[:, C_GATE:C_GATE + 4 * GROUP_W]
    gate_ref[0] = g / (1.0 + jnp.exp(-g))


def _proj_call(h, w_packed, bfor_row, gq, wqup, gkv, wkvup, tabs, mkv_l, tm):
    bsz, s, d = h.shape
    m = mkv_l.shape[1]
    cq, sq, ck, sk = tabs
    bf16 = jnp.bfloat16
    full2 = lambda shape: pl.BlockSpec(shape, lambda b, i: (0, 0))
    row_blk = lambda c: pl.BlockSpec((1, tm, c), lambda b, i: (b, i, 0))
    tab_blk = pl.BlockSpec((tm, LANES), lambda b, i: (i, 0))
    out_cols = [(2 * GROUP_W, bf16), (2 * GROUP_W, bf16), (GROUP_W, bf16),
                (GROUP_W, bf16), (GROUP_W, bf16), (GROUP_W, bf16),
                (2 * GROUP_W, bf16), (2 * GROUP_W, bf16), (GROUP_W, bf16),
                (GROUP_W, jnp.float32), (4 * GROUP_W, jnp.float32)]
    return pl.pallas_call(
        _proj_kernel,
        grid=(bsz, s // tm),
        in_specs=[row_blk(d),
                  full2(w_packed.shape),
                  full2((1, LANES)),
                  full2((1, 256)), full2(wqup.shape),
                  full2((1, 128)), full2(wkvup.shape),
                  tab_blk, tab_blk, tab_blk, tab_blk,
                  pl.BlockSpec((1, m, 2 * GROUP_W), lambda b, i: (b, 0, 0))],
        out_specs=[row_blk(c) for c, _ in out_cols],
        out_shape=[jax.ShapeDtypeStruct((bsz, s, c), dt) for c, dt in out_cols],
        scratch_shapes=[pltpu.VMEM((1, LANES), jnp.float32)],
        compiler_params=_cparams(2),
        name="proj",
    )(h, w_packed, bfor_row, gq, wqup, gkv, wkvup, cq, sq, ck, sk, mkv_l)


def _softmax_attn_kernel(q_ref, k_ref, v_ref, o_ref, q_sc, s_sc, m_sc, acc_sc, *, fox, tq, tk):
    pair = pl.program_id(1)
    qi = pl.program_id(2)
    bf16, f32 = jnp.bfloat16, jnp.float32
    kw = q_ref.shape[2]
    for hh in range(2):
        if fox:
            lane = lax.broadcasted_iota(jnp.int32, (tq, kw), 1)
            lo = LANES + FORGET_STRIDE * (2 * pair + hh)
            keep = ((lane >= HEAD_DIM * hh) & (lane < HEAD_DIM * (hh + 1))) | (
                (lane >= lo) & (lane < lo + FORGET_STRIDE))
            q_sc[hh] = jnp.where(keep, q_ref[0], jnp.zeros((tq, kw), bf16))
        else:
            q_sc[hh] = q_ref[0, :, LANES * hh:LANES * (hh + 1)]
    m_sc[...] = jnp.full(m_sc.shape, -jnp.inf, f32)
    acc_sc[...] = jnp.zeros(acc_sc.shape, f32)
    ones_v = jnp.ones((tk, LANES), bf16)

    def scores(kb, slot):
        k0 = pl.multiple_of(kb * tk, tk)
        for hh in range(2):
            k = k_ref[0, pl.ds(k0, tk), :] if fox else k_ref[0, pl.ds(k0, tk), LANES * hh:LANES * (hh + 1)]
            s_sc[slot, hh] = lax.dot_general(q_sc[hh], k, (((1,), (1,)), ((), ())),
                                             preferred_element_type=f32)

    def update(kb, slot, masked):
        k0 = pl.multiple_of(kb * tk, tk)
        v = jnp.concatenate([v_ref[0, pl.ds(k0, tk), :], ones_v], axis=1)
        if masked:
            r = lax.broadcasted_iota(jnp.int32, (tq, tk), 0) + qi * tq
            c = lax.broadcasted_iota(jnp.int32, (tq, tk), 1) + k0
            causal = c <= r
        for hh in range(2):
            s = s_sc[slot, hh]
            if masked:
                s = jnp.where(causal, s, -jnp.inf)
            m_prev = m_sc[hh]
            m_new = jnp.maximum(m_prev, jnp.max(s, axis=-1, keepdims=True))
            alpha = jnp.exp2(m_prev - m_new)
            p = jnp.exp2(s - jnp.tile(m_new, (1, tk // LANES)))
            acc_sc[hh] = (jnp.tile(alpha, (1, 2)) * acc_sc[hh]
                          + jnp.dot(p.astype(bf16), v, preferred_element_type=f32))
            m_sc[hh] = m_new

    n_diag = tq // tk
    assert n_diag % 2 == 0
    n_full = qi * n_diag
    scores(0, 0)

    def body(j, c):
        kb = 2 * j
        scores(kb + 1, 1)
        update(kb, 0, False)
        scores(kb + 2, 0)
        update(kb + 1, 1, False)
        return c

    lax.fori_loop(0, n_full // 2, body, 0)
    for d in range(n_diag):
        if d + 1 < n_diag:
            scores(n_full + d + 1, (d + 1) % 2)
        update(n_full + d, d % 2, True)
    lane_o = lax.broadcasted_iota(jnp.int32, (tq, LANES), 1)
    outs = [acc_sc[hh, :, :LANES] / acc_sc[hh, :, LANES:] for hh in range(2)]
    o_ref[0] = jnp.where(lane_o < HEAD_DIM, outs[0], outs[1])


def _softmax_attn_call(q, k, v, *, fox, tq, tk, name):
    bsz, s, _ = q.shape
    kw = 2 * LANES
    kern = functools.partial(_softmax_attn_kernel, fox=fox, tq=tq, tk=tk)
    return pl.pallas_call(
        kern,
        grid=(bsz, 2, s // tq),
        in_specs=[pl.BlockSpec((1, tq, kw), lambda b, p, i: (b, i, p)),
                  pl.BlockSpec((1, s, kw), lambda b, p, i: (b, 0, p)),
                  pl.BlockSpec((1, s, LANES), lambda b, p, i: (b, 0, p))],
        out_specs=pl.BlockSpec((1, tq, LANES), lambda b, p, i: (b, i, p)),
        out_shape=jax.ShapeDtypeStruct((bsz, s, GROUP_W), jnp.float32),
        scratch_shapes=[pltpu.VMEM((2, tq, kw if fox else LANES), jnp.bfloat16),
                        pltpu.VMEM((2, 2, tq, tk), jnp.float32),
                        pltpu.VMEM((2, tq, LANES), jnp.float32),
                        pltpu.VMEM((2, tq, 2 * LANES), jnp.float32)],
        compiler_params=_cparams(3),
        name=name,
    )(q, k, v)


def _sb_attn_kernel(q_ref, k_ref, v_ref, o_ref, q_sc, z_sc, c_sc, acc_sc, *, tq, tk):
    qi = pl.program_id(2)
    bf16, f32 = jnp.bfloat16, jnp.float32
    lane_q = lax.broadcasted_iota(jnp.int32, (tq, LANES), 1)
    rj = lax.broadcasted_iota(jnp.int32, (tk, tk), 0)
    cs = lax.broadcasted_iota(jnp.int32, (tk, tk), 1)
    upper = jnp.where(rj >= cs, 1.0, 0.0).astype(bf16)
    for hh in range(2):
        in_head = (lane_q >= HEAD_DIM * hh) & (lane_q < HEAD_DIM * (hh + 1))
        q_sc[hh] = jnp.where(in_head, q_ref[0], jnp.zeros((tq, LANES), bf16))
    c_sc[...] = jnp.zeros(c_sc.shape, f32)
    acc_sc[...] = jnp.zeros(acc_sc.shape, f32)

    def scores(kb, slot):
        k0 = pl.multiple_of(kb * tk, tk)
        k = k_ref[0, pl.ds(k0, tk), :]
        for hh in range(2):
            z_sc[slot, hh] = lax.dot_general(q_sc[hh], k, (((1,), (1,)), ((), ())),
                                             preferred_element_type=f32)

    def update(kb, slot, masked):
        k0 = pl.multiple_of(kb * tk, tk)
        v = v_ref[0, pl.ds(k0, tk), :]
        if masked:
            r = lax.broadcasted_iota(jnp.int32, (tq, tk), 0) + qi * tq
            c = lax.broadcasted_iota(jnp.int32, (tq, tk), 1) + k0
            valid = c < r
        for hh in range(2):
            z = z_sc[slot, hh]
            sp = jnp.maximum(z, 0.0) + jnp.log2(1.0 + jnp.exp2(-jnp.abs(z)))
            if masked:
                sp = jnp.where(valid, sp, 0.0)
            csum = (jnp.dot(sp.astype(bf16), upper, preferred_element_type=f32)
                    + jnp.tile(c_sc[hh], (1, tk // LANES)))
            w = jnp.exp2(z - csum)
            if masked:
                w = jnp.where(valid, w, 0.0)
            acc_sc[hh] += jnp.dot(w.astype(bf16), v, preferred_element_type=f32)
            c_sc[hh] = jnp.broadcast_to(csum[:, 0:1], (tq, LANES))

    n_diag = tq // tk
    assert n_diag % 2 == 0
    n_full = qi * n_diag
    last = n_full + n_diag - 1
    scores(last, 1)
    for d in range(n_diag):
        kb = last - d
        scores(jnp.maximum(kb - 1, 0), d % 2)
        update(kb, (d + 1) % 2, True)

    def body(n, c):
        kb = n_full - 1 - 2 * n
        scores(kb - 1, 0)
        update(kb, 1, False)
        scores(jnp.maximum(kb - 2, 0), 1)
        update(kb - 1, 0, False)
        return c

    lax.fori_loop(0, n_full // 2, body, 0)
    o_ref[0] = jnp.where(lane_q < HEAD_DIM, acc_sc[0], acc_sc[1])


def _sb_attn_call(q, k, v, *, tq, tk):
    bsz, s, _ = q.shape
    kern = functools.partial(_sb_attn_kernel, tq=tq, tk=tk)
    return pl.pallas_call(
        kern,
        grid=(bsz, 2, s // tq),
        in_specs=[pl.BlockSpec((1, tq, LANES), lambda b, p, i: (b, i, p)),
                  pl.BlockSpec((1, s, LANES), lambda b, p, i: (b, 0, p)),
                  pl.BlockSpec((1, s, LANES), lambda b, p, i: (b, 0, p))],
        out_specs=pl.BlockSpec((1, tq, LANES), lambda b, p, i: (b, i, p)),
        out_shape=jax.ShapeDtypeStruct((bsz, s, GROUP_W), jnp.float32),
        scratch_shapes=[pltpu.VMEM((2, tq, LANES), jnp.bfloat16),
                        pltpu.VMEM((2, 2, tq, tk), jnp.float32),
                        pltpu.VMEM((2, tq, LANES), jnp.float32),
                        pltpu.VMEM((2, tq, LANES), jnp.float32)],
        compiler_params=_cparams(3),
        name="sb_attn",
    )(q, k, v)


def _out_kernel(of_ref, os_ref, om_ref, ox_ref, gate_ref, h_ref, w_ref, g_ref, b_ref, o_ref, *, alpha):
    bf16, f32 = jnp.bfloat16, jnp.float32
    mixed = jnp.concatenate([of_ref[...], os_ref[...], om_ref[...], ox_ref[...]], axis=1)
    y = jnp.dot((mixed * gate_ref[...]).astype(bf16), w_ref[...], preferred_element_type=f32)
    o_ref[...] = _layer_norm_rows(alpha * h_ref[...] + y, g_ref[...], b_ref[...])


def _out_call(o_fox, o_sb, o_mla, o_mem, gate, h2d, w_out_bf16, g, b, alpha, tm):
    n, d = h2d.shape
    gw = o_fox.shape[1]
    kern = functools.partial(_out_kernel, alpha=alpha)
    blk = lambda c: pl.BlockSpec((tm, c), lambda i: (i, 0))
    return pl.pallas_call(
        kern,
        grid=(n // tm,),
        in_specs=[blk(gw), blk(gw), blk(gw), blk(gw), blk(4 * gw), blk(d),
                  pl.BlockSpec(w_out_bf16.shape, lambda i: (0, 0)),
                  pl.BlockSpec((1, d), lambda i: (0, 0)),
                  pl.BlockSpec((1, d), lambda i: (0, 0))],
        out_specs=blk(d),
        out_shape=jax.ShapeDtypeStruct((n, d), jnp.float32),
        compiler_params=_cparams(1),
        name="out_proj",
    )(o_fox, o_sb, o_mla, o_mem, gate, h2d, w_out_bf16, g.reshape(1, d), b.reshape(1, d))


def _pack_w_in(w):
    d = w.shape[0]
    o = 0
    cols = {}
    for name, width in (("fq", 256), ("fk", 256), ("fv", 256), ("fl", 4), ("sq", 256), ("sk", 256),
                        ("sv", 256), ("cq", 256), ("ckv", 128), ("kr", 32), ("mq", 256), ("gate", 1024)):
        cols[name] = w[:, o:o + width]
        o += width
    z = lambda n: jnp.zeros((d, n), w.dtype)
    half = MLA_ROPE // 2
    kr = cols["kr"]
    kr_rot = jnp.concatenate([-kr[:, half:], kr[:, :half]], axis=1)
    krp = jnp.concatenate([z(MLA_NOPE), kr, z(LANES - MLA_NOPE - MLA_ROPE)], axis=1)
    krrp = jnp.concatenate([z(MLA_NOPE), kr_rot, z(LANES - MLA_NOPE - MLA_ROPE)], axis=1)
    misc = []
    for hh in range(N_HEADS):
        misc += [cols["fl"][:, hh:hh + 1]] * FORGET_DUP + [z(FORGET_STRIDE - FORGET_DUP)]
    misc.append(z(LANES - N_HEADS * FORGET_STRIDE))
    packed = jnp.concatenate([cols["fq"], cols["fk"], cols["fv"], cols["sq"], cols["sk"], cols["sv"],
                              cols["cq"], cols["ckv"], krp, krrp, cols["mq"], cols["gate"]] + misc, axis=1)
    assert packed.shape[1] == W_COLS
    return packed.astype(jnp.bfloat16)


def _pack_forget_bias(b_forget_l):
    row = jnp.zeros((LANES,), jnp.float32)
    for hh in range(N_HEADS):
        row = row.at[FORGET_STRIDE * hh:FORGET_STRIDE * hh + FORGET_DUP].set(b_forget_l[hh])
    return row.reshape(1, LANES)


def _pack_mla_q_up(w):
    r = w.shape[0]
    per = MLA_NOPE + MLA_ROPE
    half = MLA_ROPE // 2
    z = lambda n: jnp.zeros((r, n), w.dtype)
    plain, rot = [], []
    for hh in range(N_HEADS):
        nope = w[:, per * hh:per * hh + MLA_NOPE]
        rope = w[:, per * hh + MLA_NOPE:per * (hh + 1)]
        rope_rot = jnp.concatenate([-rope[:, half:], rope[:, :half]], axis=1)
        plain += [nope, rope, z(LANES - per)]
        rot += [z(MLA_NOPE), rope_rot, z(LANES - per)]
    return jnp.concatenate(plain + rot, axis=1).astype(jnp.bfloat16)


def _pack_mla_kv_up(w):
    r = w.shape[0]
    per = MLA_NOPE + HEAD_DIM
    z = jnp.zeros((r, LANES - MLA_NOPE), w.dtype)
    ks, vs = [], []
    for hh in range(N_HEADS):
        ks += [w[:, per * hh:per * hh + MLA_NOPE], z]
        vs.append(w[:, per * hh + MLA_NOPE:per * (hh + 1)])
    return jnp.concatenate(ks + vs, axis=1).astype(jnp.bfloat16)


def _rope_tables(s, q_scale):
    half = MLA_ROPE // 2
    inv_freq = ROPE_THETA ** (-jnp.arange(half, dtype=jnp.float32) / half)
    ang = jnp.arange(s).astype(jnp.float32)[:, None] * inv_freq[None, :]
    cos, sin = jnp.cos(ang), jnp.sin(ang)
    ones = jnp.ones((s, MLA_NOPE), jnp.float32)
    z_nope = jnp.zeros((s, MLA_NOPE), jnp.float32)
    z_pad = jnp.zeros((s, LANES - MLA_NOPE - MLA_ROPE), jnp.float32)
    cos_q = jnp.concatenate([ones, cos, cos, z_pad], axis=1) * q_scale
    sin_q = jnp.concatenate([z_nope, sin, sin, z_pad], axis=1) * q_scale
    cos_k = jnp.concatenate([z_nope, cos, cos, z_pad], axis=1)
    sin_k = jnp.concatenate([z_nope, sin, sin, z_pad], axis=1)
    return cos_q, sin_q, cos_k, sin_k


def kernel(x, mem, ln_in_g, ln_in_b, mem_ln_g, mem_ln_b, w_in, b_forget, mla_q_norm_g, w_mla_q_up,
           mla_kv_norm_g, w_mla_kv_up, w_mem_kv, w_out, ln_g, ln_b):
    bsz, s, d = x.shape
    depth = w_in.shape[0]
    alpha = (2 * depth) ** 0.25
    tm = min(512, s)
    tq = min(1024, s)
    tk_sm = min(512, tq // 2)
    tk_sb = min(256, tq // 2)
    mla_scale = (MLA_NOPE + MLA_ROPE) ** -0.5
    tabs = _rope_tables(s, mla_scale * LOG2E)

    h = _layer_norm_call(x.reshape(bsz * s, d), ln_in_g, ln_in_b, tm).reshape(bsz, s, d)
    mkv = _mem_kv_call(mem, mem_ln_g, mem_ln_b, w_mem_kv.astype(jnp.bfloat16))

    for l in range(depth):
        (fq, fk, fv, sbq, sbk, sbv, mq, mk, mv, o_mem, gate) = _proj_call(
            h, _pack_w_in(w_in[l]), _pack_forget_bias(b_forget[l]),
            mla_q_norm_g[l].reshape(1, -1), _pack_mla_q_up(w_mla_q_up[l]),
            mla_kv_norm_g[l].reshape(1, -1), _pack_mla_kv_up(w_mla_kv_up[l]),
            tabs, mkv[l], tm)
        o_fox = _softmax_attn_call(fq, fk, fv, fox=True, tq=tq, tk=tk_sm, name="fox_attn")
        o_sb = _sb_attn_call(sbq, sbk, sbv, tq=tq, tk=tk_sb)
        o_mla = _softmax_attn_call(mq, mk, mv, fox=False, tq=tq, tk=tk_sm, name="mla_attn")
        n = bsz * s
        h = _out_call(o_fox.reshape(n, -1), o_sb.reshape(n, -1), o_mla.reshape(n, -1), o_mem.reshape(n, -1),
                      gate.reshape(n, -1), h.reshape(n, d), w_out[l].astype(jnp.bfloat16),
                      ln_g[l], ln_b[l], alpha, tm).reshape(bsz, s, d)
    return h
```

```python
import functools
import math

import numpy as np
import jax
import jax.numpy as jnp
from jax import lax
from jax.experimental import pallas as pl
from jax.experimental.pallas import tpu as pltpu

N_HEADS = 4
HEAD_DIM = 64
GROUP_W = N_HEADS * HEAD_DIM
MLA_NOPE = 64
MLA_ROPE = 32
ROPE_THETA = 10000.0
LN_EPS = 1e-5
RMS_EPS = 1e-6
LOG2E = 1.4426950408889634

LANES = 128
VMEM_LIMIT_BYTES = 48 * 1024 * 1024

C_FQ, C_FK, C_FV = 0, 256, 512
C_SQ, C_SK, C_SV = 768, 1024, 1280
C_CQ = 1536
C_CKV = 1792
C_KRP = 1920
C_KRRP = 2048
C_MQ = 2176
C_GATE = 2432
C_MISC = 3456
W_COLS = 3584
FORGET_DUP = 6
FORGET_STRIDE = 8


def _cparams(n_grid):
    return pltpu.CompilerParams(dimension_semantics=("arbitrary",) * n_grid,
                                vmem_limit_bytes=VMEM_LIMIT_BYTES)


def _split3(x):
    p0 = x.astype(jnp.bfloat16)
    r1 = x - p0.astype(jnp.float32)
    p1 = r1.astype(jnp.bfloat16)
    r2 = r1 - p1.astype(jnp.float32)
    p2 = r2.astype(jnp.bfloat16)
    return p0, p1, p2


def _layer_norm_rows(x, g, b):
    mu = jnp.mean(x, axis=-1, keepdims=True)
    xc = x - mu
    var = jnp.mean(xc * xc, axis=-1, keepdims=True)
    return xc * lax.rsqrt(var + LN_EPS) * g + b


def _rms_norm_rows(x, g):
    ms = jnp.mean(x * x, axis=-1, keepdims=True)
    return x * lax.rsqrt(ms + RMS_EPS) * g


def _ln_kernel(x_ref, g_ref, b_ref, o_ref):
    o_ref[...] = _layer_norm_rows(x_ref[...], g_ref[...], b_ref[...])


def _layer_norm_call(x2d, g, b, tm):
    n, d = x2d.shape
    return pl.pallas_call(
        _ln_kernel,
        grid=(n // tm,),
        in_specs=[pl.BlockSpec((tm, d), lambda i: (i, 0)),
                  pl.BlockSpec((1, d), lambda i: (0, 0)),
                  pl.BlockSpec((1, d), lambda i: (0, 0))],
        out_specs=pl.BlockSpec((tm, d), lambda i: (i, 0)),
        out_shape=jax.ShapeDtypeStruct((n, d), jnp.float32),
        compiler_params=_cparams(1),
        name="ln_in",
    )(x2d, g.reshape(1, d), b.reshape(1, d))


def _mem_kv_kernel(mem_ref, g_ref, b_ref, w_ref, o_ref):
    mem_n = _layer_norm_rows(mem_ref[0], g_ref[...], b_ref[...]).astype(jnp.bfloat16)
    for l in range(w_ref.shape[0]):
        o_ref[l, 0] = jnp.dot(mem_n, w_ref[l], preferred_element_type=jnp.float32).astype(jnp.bfloat16)


def _mem_kv_call(mem, g, b, w_mem_kv_bf16):
    bsz, m, d = mem.shape
    depth, _, c = w_mem_kv_bf16.shape
    return pl.pallas_call(
        _mem_kv_kernel,
        grid=(bsz,),
        in_specs=[pl.BlockSpec((1, m, d), lambda i: (i, 0, 0)),
                  pl.BlockSpec((1, d), lambda i: (0, 0)),
                  pl.BlockSpec((1, d), lambda i: (0, 0)),
                  pl.BlockSpec((depth, d, c), lambda i: (0, 0, 0))],
        out_specs=pl.BlockSpec((depth, 1, m, c), lambda i: (0, i, 0, 0)),
        out_shape=jax.ShapeDtypeStruct((depth, bsz, m, c), jnp.bfloat16),
        compiler_params=_cparams(1),
        name="mem_kv",
    )(mem, g.reshape(1, d), b.reshape(1, d), w_mem_kv_bf16)


def _proj_kernel(h_ref, w_ref, bfor_ref, gq_ref, wqup_ref, gkv_ref, wkvup_ref,
                 cq_ref, sq_ref, ck_ref, sk_ref, mkv_ref,
                 fq_ref, fk_ref, fv_ref, sbq_ref, sbk_ref, sbv_ref,
                 mq_ref, mk_ref, mv_ref, omem_ref, gate_ref,
                 carry_ref):
    i = pl.program_id(1)
    tm = h_ref.shape[1]
    bf16, f32 = jnp.bfloat16, jnp.float32
    head_c = (HEAD_DIM ** -0.5) * LOG2E

    proj = jnp.dot(h_ref[0].astype(bf16), w_ref[...], preferred_element_type=f32)

    @pl.when(i == 0)
    def _():
        carry_ref[...] = jnp.zeros_like(carry_ref)

    lane = lax.broadcasted_iota(jnp.int32, (tm, LANES), 1)
    sub = lane % FORGET_STRIDE
    used = (lane < N_HEADS * FORGET_STRIDE) & (sub < FORGET_DUP)
    xf = proj[:, C_MISC:C_MISC + LANES] + bfor_ref[...]
    log_f = jnp.minimum(xf, 0.0) - jnp.log1p(jnp.exp(-jnp.abs(xf)))
    log_f = jnp.where(used, log_f, 0.0)
    row = lax.broadcasted_iota(jnp.int32, (tm, tm), 0)
    col = lax.broadcasted_iota(jnp.int32, (tm, tm), 1)
    tril = jnp.where(col <= row, 1.0, 0.0).astype(bf16)
    parts = jnp.concatenate(_split3(log_f), axis=1)
    csum = jnp.dot(tril, parts, preferred_element_type=f32)
    f_cum = (csum[:, :LANES] + csum[:, LANES:2 * LANES]) + csum[:, 2 * LANES:] + carry_ref[...]
    carry_ref[...] = f_cum[tm - 1:tm, :]
    p0, p1, p2 = _split3(f_cum * LOG2E)
    one = jnp.ones((tm, LANES), bf16)
    zero = jnp.zeros((tm, LANES), bf16)
    bias_q = jnp.where(sub == 0, p0, jnp.where(sub == 1, p1, jnp.where(sub == 2, p2, one)))
    bias_q = jnp.where(used, bias_q, zero)
    bias_k = jnp.where(sub == 3, -p0, jnp.where(sub == 4, -p1, jnp.where(sub == 5, -p2, one)))
    bias_k = jnp.where(used, bias_k, zero)

    for p in range(2):
        fq_ref[0, :, 2 * LANES * p:2 * LANES * p + LANES] = (
            proj[:, C_FQ + LANES * p:C_FQ + LANES * (p + 1)] * head_c).astype(bf16)
        fq_ref[0, :, 2 * LANES * p + LANES:2 * LANES * (p + 1)] = bias_q
        fk_ref[0, :, 2 * LANES * p:2 * LANES * p + LANES] = (
            proj[:, C_FK + LANES * p:C_FK + LANES * (p + 1)]).astype(bf16)
        fk_ref[0, :, 2 * LANES * p + LANES:2 * LANES * (p + 1)] = bias_k
    fv_ref[0] = proj[:, C_FV:C_FV + GROUP_W].astype(bf16)

    sbq_ref[0] = (proj[:, C_SQ:C_SQ + GROUP_W] * head_c).astype(bf16)
    sbk_ref[0] = proj[:, C_SK:C_SK + GROUP_W].astype(bf16)
    sbv_ref[0] = proj[:, C_SV:C_SV + GROUP_W].astype(bf16)

    cqn = _rms_norm_rows(proj[:, C_CQ:C_CQ + 256], gq_ref[...]).astype(bf16)
    q_up = jnp.dot(cqn, wqup_ref[...], preferred_element_type=f32)
    ckvn = _rms_norm_rows(proj[:, C_CKV:C_CKV + 128], gkv_ref[...]).astype(bf16)
    kv_up = jnp.dot(ckvn, wkvup_ref[...], preferred_element_type=f32)
    k_rope = (proj[:, C_KRP:C_KRP + LANES] * ck_ref[...]
              + proj[:, C_KRRP:C_KRRP + LANES] * sk_ref[...])
    nq = N_HEADS * LANES
    for hh in range(N_HEADS):
        sl = slice(LANES * hh, LANES * (hh + 1))
        mq_ref[0, :, sl] = (q_up[:, sl] * cq_ref[...]
                            + q_up[:, nq + LANES * hh:nq + LANES * (hh + 1)] * sq_ref[...]).astype(bf16)
        mk_ref[0, :, sl] = (kv_up[:, sl] + k_rope).astype(bf16)
    mv_ref[0] = kv_up[:, nq:nq + GROUP_W].astype(bf16)

    mem_q = proj[:, C_MQ:C_MQ + GROUP_W] * head_c
    lane_h = lax.broadcasted_iota(jnp.int32, (tm, LANES), 1)
    for p in range(2):
        qp = mem_q[:, LANES * p:LANES * (p + 1)]
        kp = mkv_ref[0, :, LANES * p:LANES * (p + 1)]
        vp = mkv_ref[0, :, GROUP_W + LANES * p:GROUP_W + LANES * (p + 1)]
        outs = []
        for hh in range(2):
            in_head = (lane_h >= HEAD_DIM * hh) & (lane_h < HEAD_DIM * (hh + 1))
            qh = jnp.where(in_head, qp, 0.0).astype(bf16)
            s = lax.dot_general(qh, kp, (((1,), (1,)), ((), ())), preferred_element_type=f32)
            m = jnp.max(s, axis=-1, keepdims=True)
            e = jnp.exp2(s - m)
            pr = e / jnp.sum(e, axis=-1, keepdims=True)
            outs.append(jnp.dot(pr.astype(bf16), vp, preferred_element_type=f32))
        omem_ref[0, :, LANES * p:LANES * (p + 1)] = jnp.where(lane_h < HEAD_DIM, outs[0], outs[1])

    g = proj[:, C_GATE:C_GATE + 4 * GROUP_W]
    gate_ref[0] = g / (1.0 + jnp.exp(-g))


def _proj_call(h, w_packed, bfor_row, gq, wqup, gkv, wkvup, tabs, mkv_l, tm):
    bsz, s, d = h.shape
    m = mkv_l.shape[1]
    cq, sq, ck, sk = tabs
    bf16 = jnp.bfloat16
    full2 = lambda shape: pl.BlockSpec(shape, lambda b, i: (0, 0))
    row_blk = lambda c: pl.BlockSpec((1, tm, c), lambda b, i: (b, i, 0))
    tab_blk = pl.BlockSpec((tm, LANES), lambda b, i: (i, 0))
    out_cols = [(2 * GROUP_W, bf16), (2 * GROUP_W, bf16), (GROUP_W, bf16),
                (GROUP_W, bf16), (GROUP_W, bf16), (GROUP_W, bf16),
                (2 * GROUP_W, bf16), (2 * GROUP_W, bf16), (GROUP_W, bf16),
                (GROUP_W, jnp.float32), (4 * GROUP_W, jnp.float32)]
    return pl.pallas_call(
        _proj_kernel,
        grid=(bsz, s // tm),
        in_specs=[row_blk(d),
                  full2(w_packed.shape),
                  full2((1, LANES)),
                  full2((1, 256)), full2(wqup.shape),
                  full2((1, 128)), full2(wkvup.shape),
                  tab_blk, tab_blk, tab_blk, tab_blk,
                  pl.BlockSpec((1, m, 2 * GROUP_W), lambda b, i: (b, 0, 0))],
        out_specs=[row_blk(c) for c, _ in out_cols],
        out_shape=[jax.ShapeDtypeStruct((bsz, s, c), dt) for c, dt in out_cols],
        scratch_shapes=[pltpu.VMEM((1, LANES), jnp.float32)],
        compiler_params=_cparams(2),
        name="proj",
    )(h, w_packed, bfor_row, gq, wqup, gkv, wkvup, cq, sq, ck, sk, mkv_l)


def _softmax_attn_kernel(q_ref, k_ref, v_ref, o_ref, q_sc, s_sc, m_sc, acc_sc, *, fox, tq, tk):
    pair = pl.program_id(1)
    qi = pl.program_id(2)
    bf16, f32 = jnp.bfloat16, jnp.float32
    kw = q_ref.shape[2]
    for hh in range(2):
        if fox:
            lane = lax.broadcasted_iota(jnp.int32, (tq, kw), 1)
            lo = LANES + FORGET_STRIDE * (2 * pair + hh)
            keep = ((lane >= HEAD_DIM * hh) & (lane < HEAD_DIM * (hh + 1))) | (
                (lane >= lo) & (lane < lo + FORGET_STRIDE))
            q_sc[hh] = jnp.where(keep, q_ref[0], jnp.zeros((tq, kw), bf16))
        else:
            q_sc[hh] = q_ref[0, :, LANES * hh:LANES * (hh + 1)]
    m_sc[...] = jnp.full(m_sc.shape, -jnp.inf, f32)
    acc_sc[...] = jnp.zeros(acc_sc.shape, f32)
    ones_v = jnp.ones((tk, LANES), bf16)

    def scores(kb, slot, r0=0):
        k0 = pl.multiple_of(kb * tk, tk)
        for hh in range(2):
            k = k_ref[0, pl.ds(k0, tk), :] if fox else k_ref[0, pl.ds(k0, tk), LANES * hh:LANES * (hh + 1)]
            s_sc[slot, hh, r0:] = lax.dot_general(q_sc[hh, r0:], k, (((1,), (1,)), ((), ())),
                                                  preferred_element_type=f32)

    def update(kb, slot, r0=0, nr=tq, triangle=False):
        k0 = pl.multiple_of(kb * tk, tk)
        v = jnp.concatenate([v_ref[0, pl.ds(k0, tk), :], ones_v], axis=1)
        rows = slice(r0, r0 + nr)
        if triangle:
            causal = (lax.broadcasted_iota(jnp.int32, (nr, tk), 1)
                      <= lax.broadcasted_iota(jnp.int32, (nr, tk), 0))
        for hh in range(2):
            s = s_sc[slot, hh, rows]
            if triangle:
                s = jnp.where(causal, s, -jnp.inf)
            m_prev = m_sc[hh, rows]
            m_new = jnp.maximum(m_prev, jnp.max(s, axis=-1, keepdims=True))
            alpha = jnp.exp2(m_prev - m_new)
            p = jnp.exp2(s - jnp.tile(m_new, (1, tk // LANES)))
            acc_sc[hh, rows] = (jnp.tile(alpha, (1, 2)) * acc_sc[hh, rows]
                                + jnp.dot(p.astype(bf16), v, preferred_element_type=f32))
            m_sc[hh, rows] = m_new

    n_diag = tq // tk
    assert n_diag % 2 == 0
    n_full = qi * n_diag
    scores(0, 0)

    def body(j, c):
        kb = 2 * j
        scores(kb + 1, 1)
        update(kb, 0)
        scores(kb + 2, 0)
        update(kb + 1, 1)
        return c

    lax.fori_loop(0, n_full // 2, body, 0)
    for d in range(n_diag):
        if d + 1 < n_diag:
            scores(n_full + d + 1, (d + 1) % 2, r0=(d + 1) * tk)
        update(n_full + d, d % 2, r0=d * tk, nr=tk, triangle=True)
        if d + 1 < n_diag:
            update(n_full + d, d % 2, r0=(d + 1) * tk, nr=tq - (d + 1) * tk)
    lane_o = lax.broadcasted_iota(jnp.int32, (tq, LANES), 1)
    outs = [acc_sc[hh, :, :LANES] / acc_sc[hh, :, LANES:] for hh in range(2)]
    o_ref[0] = jnp.where(lane_o < HEAD_DIM, outs[0], outs[1])


def _softmax_attn_call(q, k, v, *, fox, tq, tk, name):
    bsz, s, _ = q.shape
    kw = 2 * LANES
    kern = functools.partial(_softmax_attn_kernel, fox=fox, tq=tq, tk=tk)
    return pl.pallas_call(
        kern,
        grid=(bsz, 2, s // tq),
        in_specs=[pl.BlockSpec((1, tq, kw), lambda b, p, i: (b, i, p)),
                  pl.BlockSpec((1, s, kw), lambda b, p, i: (b, 0, p)),
                  pl.BlockSpec((1, s, LANES), lambda b, p, i: (b, 0, p))],
        out_specs=pl.BlockSpec((1, tq, LANES), lambda b, p, i: (b, i, p)),
        out_shape=jax.ShapeDtypeStruct((bsz, s, GROUP_W), jnp.float32),
        scratch_shapes=[pltpu.VMEM((2, tq, kw if fox else LANES), jnp.bfloat16),
                        pltpu.VMEM((2, 2, tq, tk), jnp.float32),
                        pltpu.VMEM((2, tq, LANES), jnp.float32),
                        pltpu.VMEM((2, tq, 2 * LANES), jnp.float32)],
        compiler_params=_cparams(3),
        name=name,
    )(q, k, v)


def _sb_attn_kernel(q_ref, k_ref, v_ref, o_ref, q_sc, z_sc, c_sc, acc_sc, *, tq, tk):
    qi = pl.program_id(2)
    bf16, f32 = jnp.bfloat16, jnp.float32
    lane_q = lax.broadcasted_iota(jnp.int32, (tq, LANES), 1)
    rj = lax.broadcasted_iota(jnp.int32, (tk, tk), 0)
    cs = lax.broadcasted_iota(jnp.int32, (tk, tk), 1)
    upper = jnp.where(rj >= cs, 1.0, 0.0).astype(bf16)
    for hh in range(2):
        in_head = (lane_q >= HEAD_DIM * hh) & (lane_q < HEAD_DIM * (hh + 1))
        q_sc[hh] = jnp.where(in_head, q_ref[0], jnp.zeros((tq, LANES), bf16))
    c_sc[...] = jnp.zeros(c_sc.shape, f32)
    acc_sc[...] = jnp.zeros(acc_sc.shape, f32)

    def scores(kb, slot, r0=0):
        k0 = pl.multiple_of(kb * tk, tk)
        k = k_ref[0, pl.ds(k0, tk), :]
        for hh in range(2):
            z_sc[slot, hh, r0:] = lax.dot_general(q_sc[hh, r0:], k, (((1,), (1,)), ((), ())),
                                                  preferred_element_type=f32)

    def update(kb, slot, r0=0, nr=tq, triangle=False):
        k0 = pl.multiple_of(kb * tk, tk)
        v = v_ref[0, pl.ds(k0, tk), :]
        rows = slice(r0, r0 + nr)
        if triangle:
            valid = (lax.broadcasted_iota(jnp.int32, (nr, tk), 1)
                     < lax.broadcasted_iota(jnp.int32, (nr, tk), 0))
        for hh in range(2):
            z = z_sc[slot, hh, rows]
            sp = jnp.maximum(z, 0.0) + jnp.log2(1.0 + jnp.exp2(-jnp.abs(z)))
            if triangle:
                sp = jnp.where(valid, sp, 0.0)
            csum = (jnp.dot(sp.astype(bf16), upper, preferred_element_type=f32)
                    + jnp.tile(c_sc[hh, rows], (1, tk // LANES)))
            w = jnp.exp2(z - csum)
            if triangle:
                w = jnp.where(valid, w, 0.0)
            acc_sc[hh, rows] += jnp.dot(w.astype(bf16), v, preferred_element_type=f32)
            c_sc[hh, rows] = jnp.broadcast_to(csum[:, 0:1], (nr, LANES))

    n_diag = tq // tk
    assert n_diag % 2 == 0
    n_full = qi * n_diag
    scores(n_full + n_diag - 1, 1, r0=(n_diag - 1) * tk)
    for d in reversed(range(n_diag)):
        kb = n_full + d
        scores(jnp.maximum(kb - 1, 0), (d + 1) % 2, r0=max(d - 1, 0) * tk)
        update(kb, d % 2, r0=d * tk, nr=tk, triangle=True)
        if d + 1 < n_diag:
            update(kb, d % 2, r0=(d + 1) * tk, nr=tq - (d + 1) * tk)

    def body(n, c):
        kb = n_full - 1 - 2 * n
        scores(kb - 1, 0)
        update(kb, 1)
        scores(jnp.maximum(kb - 2, 0), 1)
        update(kb - 1, 0)
        return c

    lax.fori_loop(0, n_full // 2, body, 0)
    o_ref[0] = jnp.where(lane_q < HEAD_DIM, acc_sc[0], acc_sc[1])


def _sb_attn_call(q, k, v, *, tq, tk):
    bsz, s, _ = q.shape
    kern = functools.partial(_sb_attn_kernel, tq=tq, tk=tk)
    return pl.pallas_call(
        kern,
        grid=(bsz, 2, s // tq),
        in_specs=[pl.BlockSpec((1, tq, LANES), lambda b, p, i: (b, i, p)),
                  pl.BlockSpec((1, s, LANES), lambda b, p, i: (b, 0, p)),
                  pl.BlockSpec((1, s, LANES), lambda b, p, i: (b, 0, p))],
        out_specs=pl.BlockSpec((1, tq, LANES), lambda b, p, i: (b, i, p)),
        out_shape=jax.ShapeDtypeStruct((bsz, s, GROUP_W), jnp.float32),
        scratch_shapes=[pltpu.VMEM((2, tq, LANES), jnp.bfloat16),
                        pltpu.VMEM((2, 2, tq, tk), jnp.float32),
                        pltpu.VMEM((2, tq, LANES), jnp.float32),
                        pltpu.VMEM((2, tq, LANES), jnp.float32)],
        compiler_params=_cparams(3),
        name="sb_attn",
    )(q, k, v)


def _out_kernel(of_ref, os_ref, om_ref, ox_ref, gate_ref, h_ref, w_ref, g_ref, b_ref, o_ref, *, alpha):
    bf16, f32 = jnp.bfloat16, jnp.float32
    mixed = jnp.concatenate([of_ref[...], os_ref[...], om_ref[...], ox_ref[...]], axis=1)
    y = jnp.dot((mixed * gate_ref[...]).astype(bf16), w_ref[...], preferred_element_type=f32)
    o_ref[...] = _layer_norm_rows(alpha * h_ref[...] + y, g_ref[...], b_ref[...])


def _out_call(o_fox, o_sb, o_mla, o_mem, gate, h2d, w_out_bf16, g, b, alpha, tm):
    n, d = h2d.shape
    gw = o_fox.shape[1]
    kern = functools.partial(_out_kernel, alpha=alpha)
    blk = lambda c: pl.BlockSpec((tm, c), lambda i: (i, 0))
    return pl.pallas_call(
        kern,
        grid=(n // tm,),
        in_specs=[blk(gw), blk(gw), blk(gw), blk(gw), blk(4 * gw), blk(d),
                  pl.BlockSpec(w_out_bf16.shape, lambda i: (0, 0)),
                  pl.BlockSpec((1, d), lambda i: (0, 0)),
                  pl.BlockSpec((1, d), lambda i: (0, 0))],
        out_specs=blk(d),
        out_shape=jax.ShapeDtypeStruct((n, d), jnp.float32),
        compiler_params=_cparams(1),
        name="out_proj",
    )(o_fox, o_sb, o_mla, o_mem, gate, h2d, w_out_bf16, g.reshape(1, d), b.reshape(1, d))


def _pack_w_in(w):
    d = w.shape[0]
    o = 0
    cols = {}
    for name, width in (("fq", 256), ("fk", 256), ("fv", 256), ("fl", 4), ("sq", 256), ("sk", 256),
                        ("sv", 256), ("cq", 256), ("ckv", 128), ("kr", 32), ("mq", 256), ("gate", 1024)):
        cols[name] = w[:, o:o + width]
        o += width
    z = lambda n: jnp.zeros((d, n), w.dtype)
    half = MLA_ROPE // 2
    kr = cols["kr"]
    kr_rot = jnp.concatenate([-kr[:, half:], kr[:, :half]], axis=1)
    krp = jnp.concatenate([z(MLA_NOPE), kr, z(LANES - MLA_NOPE - MLA_ROPE)], axis=1)
    krrp = jnp.concatenate([z(MLA_NOPE), kr_rot, z(LANES - MLA_NOPE - MLA_ROPE)], axis=1)
    misc = []
    for hh in range(N_HEADS):
        misc += [cols["fl"][:, hh:hh + 1]] * FORGET_DUP + [z(FORGET_STRIDE - FORGET_DUP)]
    misc.append(z(LANES - N_HEADS * FORGET_STRIDE))
    packed = jnp.concatenate([cols["fq"], cols["fk"], cols["fv"], cols["sq"], cols["sk"], cols["sv"],
                              cols["cq"], cols["ckv"], krp, krrp, cols["mq"], cols["gate"]] + misc, axis=1)
    assert packed.shape[1] == W_COLS
    return packed.astype(jnp.bfloat16)


def _pack_forget_bias(b_forget_l):
    row = jnp.zeros((LANES,), jnp.float32)
    for hh in range(N_HEADS):
        row = row.at[FORGET_STRIDE * hh:FORGET_STRIDE * hh + FORGET_DUP].set(b_forget_l[hh])
    return row.reshape(1, LANES)


def _pack_mla_q_up(w):
    r = w.shape[0]
    per = MLA_NOPE + MLA_ROPE
    half = MLA_ROPE // 2
    z = lambda n: jnp.zeros((r, n), w.dtype)
    plain, rot = [], []
    for hh in range(N_HEADS):
        nope = w[:, per * hh:per * hh + MLA_NOPE]
        rope = w[:, per * hh + MLA_NOPE:per * (hh + 1)]
        rope_rot = jnp.concatenate([-rope[:, half:], rope[:, :half]], axis=1)
        plain += [nope, rope, z(LANES - per)]
        rot += [z(MLA_NOPE), rope_rot, z(LANES - per)]
    return jnp.concatenate(plain + rot, axis=1).astype(jnp.bfloat16)


def _pack_mla_kv_up(w):
    r = w.shape[0]
    per = MLA_NOPE + HEAD_DIM
    z = jnp.zeros((r, LANES - MLA_NOPE), w.dtype)
    ks, vs = [], []
    for hh in range(N_HEADS):
        ks += [w[:, per * hh:per * hh + MLA_NOPE], z]
        vs.append(w[:, per * hh + MLA_NOPE:per * (hh + 1)])
    return jnp.concatenate(ks + vs, axis=1).astype(jnp.bfloat16)


def _rope_tables(s, q_scale):
    half = MLA_ROPE // 2
    inv_freq = ROPE_THETA ** (-jnp.arange(half, dtype=jnp.float32) / half)
    ang = jnp.arange(s).astype(jnp.float32)[:, None] * inv_freq[None, :]
    cos, sin = jnp.cos(ang), jnp.sin(ang)
    ones = jnp.ones((s, MLA_NOPE), jnp.float32)
    z_nope = jnp.zeros((s, MLA_NOPE), jnp.float32)
    z_pad = jnp.zeros((s, LANES - MLA_NOPE - MLA_ROPE), jnp.float32)
    cos_q = jnp.concatenate([ones, cos, cos, z_pad], axis=1) * q_scale
    sin_q = jnp.concatenate([z_nope, sin, sin, z_pad], axis=1) * q_scale
    cos_k = jnp.concatenate([z_nope, cos, cos, z_pad], axis=1)
    sin_k = jnp.concatenate([z_nope, sin, sin, z_pad], axis=1)
    return cos_q, sin_q, cos_k, sin_k


def kernel(x, mem, ln_in_g, ln_in_b, mem_ln_g, mem_ln_b, w_in, b_forget, mla_q_norm_g, w_mla_q_up,
           mla_kv_norm_g, w_mla_kv_up, w_mem_kv, w_out, ln_g, ln_b):
    bsz, s, d = x.shape
    depth = w_in.shape[0]
    alpha = (2 * depth) ** 0.25
    tm = min(512, s)
    tq = min(1024, s)
    tk_sm = min(512, tq // 2)
    tk_sb = min(256, tq // 2)
    mla_scale = (MLA_NOPE + MLA_ROPE) ** -0.5
    tabs = _rope_tables(s, mla_scale * LOG2E)

    h = _layer_norm_call(x.reshape(bsz * s, d), ln_in_g, ln_in_b, tm).reshape(bsz, s, d)
    mkv = _mem_kv_call(mem, mem_ln_g, mem_ln_b, w_mem_kv.astype(jnp.bfloat16))

    for l in range(depth):
        (fq, fk, fv, sbq, sbk, sbv, mq, mk, mv, o_mem, gate) = _proj_call(
            h, _pack_w_in(w_in[l]), _pack_forget_bias(b_forget[l]),
            mla_q_norm_g[l].reshape(1, -1), _pack_mla_q_up(w_mla_q_up[l]),
            mla_kv_norm_g[l].reshape(1, -1), _pack_mla_kv_up(w_mla_kv_up[l]),
            tabs, mkv[l], tm)
        o_fox = _softmax_attn_call(fq, fk, fv, fox=True, tq=tq, tk=tk_sm, name="fox_attn")
        o_sb = _sb_attn_call(sbq, sbk, sbv, tq=tq, tk=tk_sb)
        o_mla = _softmax_attn_call(mq, mk, mv, fox=False, tq=tq, tk=tk_sm, name="mla_attn")
        n = bsz * s
        h = _out_call(o_fox.reshape(n, -1), o_sb.reshape(n, -1), o_mla.reshape(n, -1), o_mem.reshape(n, -1),
                      gate.reshape(n, -1), h.reshape(n, d), w_out[l].astype(jnp.bfloat16),
                      ln_g[l], ln_b[l], alpha, tm).reshape(bsz, s, d)
    return h
```

```python
import functools
import math

import numpy as np
import jax
import jax.numpy as jnp
from jax import lax
from jax.experimental import pallas as pl
from jax.experimental.pallas import tpu as pltpu

N_HEADS = 4
HEAD_DIM = 64
GROUP_W = N_HEADS * HEAD_DIM
MLA_NOPE = 64
MLA_ROPE = 32
ROPE_THETA = 10000.0
LN_EPS = 1e-5
RMS_EPS = 1e-6
LOG2E = 1.4426950408889634

LANES = 128
VMEM_LIMIT_BYTES = 48 * 1024 * 1024

C_FQ, C_FK = 0, 256
C_SQ, C_SK, C_SV = 512, 768, 1024
C_CQ = 1280
C_CKV = 1536
C_KRP = 1664
C_KRRP = 1792
C_MQ = 1920
C_GATE = 2176
C_MISC = 3200
W_COLS = 3328
FORGET_DUP = 6
FORGET_STRIDE = 8
VT_ROWS = HEAD_DIM + 16


def _cparams(n_grid):
    return pltpu.CompilerParams(dimension_semantics=("arbitrary",) * n_grid,
                                vmem_limit_bytes=VMEM_LIMIT_BYTES)


def _split3(x):
    p0 = x.astype(jnp.bfloat16)
    r1 = x - p0.astype(jnp.float32)
    p1 = r1.astype(jnp.bfloat16)
    r2 = r1 - p1.astype(jnp.float32)
    p2 = r2.astype(jnp.bfloat16)
    return p0, p1, p2


def _layer_norm_rows(x, g, b):
    mu = jnp.mean(x, axis=-1, keepdims=True)
    xc = x - mu
    var = jnp.mean(xc * xc, axis=-1, keepdims=True)
    return xc * lax.rsqrt(var + LN_EPS) * g + b


def _rms_norm_rows(x, g):
    ms = jnp.mean(x * x, axis=-1, keepdims=True)
    return x * lax.rsqrt(ms + RMS_EPS) * g


def _ln_kernel(x_ref, g_ref, b_ref, o_ref):
    o_ref[...] = _layer_norm_rows(x_ref[...], g_ref[...], b_ref[...])


def _layer_norm_call(x2d, g, b, tm):
    n, d = x2d.shape
    return pl.pallas_call(
        _ln_kernel,
        grid=(n // tm,),
        in_specs=[pl.BlockSpec((tm, d), lambda i: (i, 0)),
                  pl.BlockSpec((1, d), lambda i: (0, 0)),
                  pl.BlockSpec((1, d), lambda i: (0, 0))],
        out_specs=pl.BlockSpec((tm, d), lambda i: (i, 0)),
        out_shape=jax.ShapeDtypeStruct((n, d), jnp.float32),
        compiler_params=_cparams(1),
        name="ln_in",
    )(x2d, g.reshape(1, d), b.reshape(1, d))


def _mem_kv_kernel(mem_ref, g_ref, b_ref, w_ref, o_ref):
    mem_n = _layer_norm_rows(mem_ref[0], g_ref[...], b_ref[...]).astype(jnp.bfloat16)
    for l in range(w_ref.shape[0]):
        o_ref[l, 0] = jnp.dot(mem_n, w_ref[l], preferred_element_type=jnp.float32).astype(jnp.bfloat16)


def _mem_kv_call(mem, g, b, w_mem_kv_bf16):
    bsz, m, d = mem.shape
    depth, _, c = w_mem_kv_bf16.shape
    return pl.pallas_call(
        _mem_kv_kernel,
        grid=(bsz,),
        in_specs=[pl.BlockSpec((1, m, d), lambda i: (i, 0, 0)),
                  pl.BlockSpec((1, d), lambda i: (0, 0)),
                  pl.BlockSpec((1, d), lambda i: (0, 0)),
                  pl.BlockSpec((depth, d, c), lambda i: (0, 0, 0))],
        out_specs=pl.BlockSpec((depth, 1, m, c), lambda i: (0, i, 0, 0)),
        out_shape=jax.ShapeDtypeStruct((depth, bsz, m, c), jnp.bfloat16),
        compiler_params=_cparams(1),
        name="mem_kv",
    )(mem, g.reshape(1, d), b.reshape(1, d), w_mem_kv_bf16)


def _proj_kernel(h_ref, w_ref, wfvt_ref, bfor_ref, gq_ref, wqup_ref, gkv_ref, wkup_ref, wvupt_ref,
                 cq_ref, sq_ref, ck_ref, sk_ref, mkv_ref,
                 fq_ref, fk_ref, fvt_ref, sbq_ref, sbk_ref, sbv_ref,
                 mq_ref, mk_ref, mvt_ref, omem_ref, gate_ref,
                 carry_ref):
    i = pl.program_id(1)
    tm = h_ref.shape[1]
    bf16, f32 = jnp.bfloat16, jnp.float32
    head_c = (HEAD_DIM ** -0.5) * LOG2E
    nt_dims = (((1,), (1,)), ((), ()))

    hb = h_ref[0].astype(bf16)
    proj = jnp.dot(hb, w_ref[...], preferred_element_type=f32)

    def cols(c0, n):
        return proj[:, c0:c0 + n]

    def store_values_t(vt, out_ref):
        for hh in range(N_HEADS):
            out_ref[0, hh, :HEAD_DIM, :] = vt[HEAD_DIM * hh:HEAD_DIM * (hh + 1), :].astype(bf16)
            out_ref[0, hh, HEAD_DIM:, :] = jnp.ones((VT_ROWS - HEAD_DIM, tm), bf16)

    @pl.when(i == 0)
    def _():
        carry_ref[...] = jnp.zeros_like(carry_ref)

    lane = lax.broadcasted_iota(jnp.int32, (tm, LANES), 1)
    sub = lane % FORGET_STRIDE
    used = (lane < N_HEADS * FORGET_STRIDE) & (sub < FORGET_DUP)
    xf = cols(C_MISC, LANES) + bfor_ref[...]
    log_f = jnp.minimum(xf, 0.0) - jnp.log1p(jnp.exp(-jnp.abs(xf)))
    log_f = jnp.where(used, log_f, 0.0)
    row = lax.broadcasted_iota(jnp.int32, (tm, tm), 0)
    col = lax.broadcasted_iota(jnp.int32, (tm, tm), 1)
    tril = jnp.where(col <= row, 1.0, 0.0).astype(bf16)
    parts = jnp.concatenate(_split3(log_f), axis=1)
    csum = jnp.dot(tril, parts, preferred_element_type=f32)
    f_cum = (csum[:, :LANES] + csum[:, LANES:2 * LANES]) + csum[:, 2 * LANES:] + carry_ref[...]
    carry_ref[...] = f_cum[tm - 1:tm, :]
    p0, p1, p2 = _split3(f_cum * LOG2E)
    one = jnp.ones((tm, LANES), bf16)
    zero = jnp.zeros((tm, LANES), bf16)
    bias_q = jnp.where(sub == 0, p0, jnp.where(sub == 1, p1, jnp.where(sub == 2, p2, one)))
    bias_q = jnp.where(used, bias_q, zero)
    bias_k = jnp.where(sub == 3, -p0, jnp.where(sub == 4, -p1, jnp.where(sub == 5, -p2, one)))
    bias_k = jnp.where(used, bias_k, zero)

    fq = cols(C_FQ, GROUP_W) * head_c
    fk = cols(C_FK, GROUP_W)
    for p in range(2):
        fq_ref[0, :, 2 * LANES * p:2 * LANES * p + LANES] = fq[:, LANES * p:LANES * (p + 1)].astype(bf16)
        fq_ref[0, :, 2 * LANES * p + LANES:2 * LANES * (p + 1)] = bias_q
        fk_ref[0, :, 2 * LANES * p:2 * LANES * p + LANES] = fk[:, LANES * p:LANES * (p + 1)].astype(bf16)
        fk_ref[0, :, 2 * LANES * p + LANES:2 * LANES * (p + 1)] = bias_k
    store_values_t(lax.dot_general(wfvt_ref[...], hb, nt_dims, preferred_element_type=f32), fvt_ref)

    sbq_ref[0] = (cols(C_SQ, GROUP_W) * head_c).astype(bf16)
    sbk_ref[0] = cols(C_SK, GROUP_W).astype(bf16)
    sbv_ref[0] = cols(C_SV, GROUP_W).astype(bf16)

    cqn = _rms_norm_rows(cols(C_CQ, 256), gq_ref[...]).astype(bf16)
    q_up = jnp.dot(cqn, wqup_ref[...], preferred_element_type=f32)
    ckvn = _rms_norm_rows(cols(C_CKV, 128), gkv_ref[...]).astype(bf16)
    k_up = jnp.dot(ckvn, wkup_ref[...], preferred_element_type=f32)
    k_rope = cols(C_KRP, LANES) * ck_ref[...] + cols(C_KRRP, LANES) * sk_ref[...]
    nq = N_HEADS * LANES
    for hh in range(N_HEADS):
        sl = slice(LANES * hh, LANES * (hh + 1))
        mq_ref[0, :, sl] = (q_up[:, sl] * cq_ref[...]
                            + q_up[:, nq + LANES * hh:nq + LANES * (hh + 1)] * sq_ref[...]).astype(bf16)
        mk_ref[0, :, sl] = (k_up[:, sl] + k_rope).astype(bf16)
    store_values_t(lax.dot_general(wvupt_ref[...], ckvn, nt_dims, preferred_element_type=f32), mvt_ref)

    mem_q = cols(C_MQ, GROUP_W) * head_c
    lane_h = lax.broadcasted_iota(jnp.int32, (tm, LANES), 1)
    for p in range(2):
        qp = mem_q[:, LANES * p:LANES * (p + 1)]
        kp = mkv_ref[0, :, LANES * p:LANES * (p + 1)]
        vp = mkv_ref[0, :, GROUP_W + LANES * p:GROUP_W + LANES * (p + 1)]
        outs = []
        for hh in range(2):
            in_head = (lane_h >= HEAD_DIM * hh) & (lane_h < HEAD_DIM * (hh + 1))
            qh = jnp.where(in_head, qp, 0.0).astype(bf16)
            s = lax.dot_general(qh, kp, (((1,), (1,)), ((), ())), preferred_element_type=f32)
            m = jnp.max(s, axis=-1, keepdims=True)
            e = jnp.exp2(s - m)
            pr = e / jnp.sum(e, axis=-1, keepdims=True)
            outs.append(jnp.dot(pr.astype(bf16), vp, preferred_element_type=f32))
        omem_ref[0, :, LANES * p:LANES * (p + 1)] = jnp.where(lane_h < HEAD_DIM, outs[0], outs[1])

    for c in range(4):
        g = cols(C_GATE + GROUP_W * c, GROUP_W)
        gate_ref[0, :, GROUP_W * c:GROUP_W * (c + 1)] = g / (1.0 + jnp.exp(-g))


def _proj_call(h, w_packed, wfvt, bfor_row, gq, wqup, gkv, wkup, wvupt, tabs, mkv_l, tm):
    bsz, s, d = h.shape
    m = mkv_l.shape[1]
    cq, sq, ck, sk = tabs
    bf16 = jnp.bfloat16
    full2 = lambda shape: pl.BlockSpec(shape, lambda b, i: (0, 0))
    row_blk = lambda c: pl.BlockSpec((1, tm, c), lambda b, i: (b, i, 0))
    tab_blk = pl.BlockSpec((tm, LANES), lambda b, i: (i, 0))
    vt_blk = pl.BlockSpec((1, N_HEADS, VT_ROWS, tm), lambda b, i: (b, 0, 0, i))
    vt_shape = jax.ShapeDtypeStruct((bsz, N_HEADS, VT_ROWS, s), bf16)
    rows = lambda c, dt: (row_blk(c), jax.ShapeDtypeStruct((bsz, s, c), dt))
    outs = [rows(2 * GROUP_W, bf16), rows(2 * GROUP_W, bf16), (vt_blk, vt_shape),
            rows(GROUP_W, bf16), rows(GROUP_W, bf16), rows(GROUP_W, bf16),
            rows(2 * GROUP_W, bf16), rows(2 * GROUP_W, bf16), (vt_blk, vt_shape),
            rows(GROUP_W, jnp.float32), rows(4 * GROUP_W, jnp.float32)]
    return pl.pallas_call(
        _proj_kernel,
        grid=(bsz, s // tm),
        in_specs=[row_blk(d),
                  full2(w_packed.shape), full2(wfvt.shape),
                  full2((1, LANES)),
                  full2((1, 256)), full2(wqup.shape),
                  full2((1, 128)), full2(wkup.shape), full2(wvupt.shape),
                  tab_blk, tab_blk, tab_blk, tab_blk,
                  pl.BlockSpec((1, m, 2 * GROUP_W), lambda b, i: (b, 0, 0))],
        out_specs=[spec for spec, _ in outs],
        out_shape=[shape for _, shape in outs],
        scratch_shapes=[pltpu.VMEM((1, LANES), jnp.float32)],
        compiler_params=_cparams(2),
        name="proj",
    )(h, w_packed, wfvt, bfor_row, gq, wqup, gkv, wkup, wvupt, cq, sq, ck, sk, mkv_l)


def _softmax_attn_kernel(q_ref, k_ref, vt_ref, o_ref, q_sc, s_sc, m_sc, acc_sc, *, fox, tq, tk):
    pair = pl.program_id(1)
    qi = pl.program_id(2)
    bf16, f32 = jnp.bfloat16, jnp.float32
    kw = q_ref.shape[2]
    for hh in range(2):
        if fox:
            lane = lax.broadcasted_iota(jnp.int32, (tq, kw), 1)
            lo = LANES + FORGET_STRIDE * (2 * pair + hh)
            keep = ((lane >= HEAD_DIM * hh) & (lane < HEAD_DIM * (hh + 1))) | (
                (lane >= lo) & (lane < lo + FORGET_STRIDE))
            q_sc[hh] = jnp.where(keep, q_ref[0], jnp.zeros((tq, kw), bf16))
        else:
            q_sc[hh] = q_ref[0, :, LANES * hh:LANES * (hh + 1)]
    m_sc[...] = jnp.full(m_sc.shape, -jnp.inf, f32)
    acc_sc[...] = jnp.zeros(acc_sc.shape, f32)

    def scores(kb, slot, c0=0):
        k0 = pl.multiple_of(kb * tk, tk)
        for hh in range(2):
            k = k_ref[0, pl.ds(k0, tk), :] if fox else k_ref[0, pl.ds(k0, tk), LANES * hh:LANES * (hh + 1)]
            s_sc[slot, hh, :, c0:] = lax.dot_general(k, q_sc[hh, c0:], (((1,), (1,)), ((), ())),
                                                     preferred_element_type=f32)

    def update(kb, slot, c0=0, nc=tq, triangle=False):
        k0 = pl.multiple_of(kb * tk, tk)
        qs = slice(c0, c0 + nc)
        if triangle:
            causal = (lax.broadcasted_iota(jnp.int32, (tk, nc), 0)
                      <= lax.broadcasted_iota(jnp.int32, (tk, nc), 1))
        for hh in range(2):
            s = s_sc[slot, hh, :, qs]
            if triangle:
                s = jnp.where(causal, s, -jnp.inf)
            m_prev = m_sc[hh, :, qs]
            m_new = jnp.maximum(m_prev, jnp.max(s, axis=0, keepdims=True))
            alpha = jnp.exp2(m_prev - m_new)
            p = jnp.exp2(s - m_new)
            acc_sc[hh, :, qs] = (alpha * acc_sc[hh, :, qs]
                                 + jnp.dot(vt_ref[0, hh, :, pl.ds(k0, tk)], p.astype(bf16),
                                           preferred_element_type=f32))
            m_sc[hh, :, qs] = m_new

    n_diag = tq // tk
    assert n_diag % 2 == 0
    n_full = qi * n_diag
    scores(0, 0)

    def body(j, c):
        kb = 2 * j
        scores(kb + 1, 1)
        update(kb, 0)
        scores(kb + 2, 0)
        update(kb + 1, 1)
        return c

    lax.fori_loop(0, n_full // 2, body, 0)
    for d in range(n_diag):
        if d + 1 < n_diag:
            scores(n_full + d + 1, (d + 1) % 2, c0=(d + 1) * tk)
        update(n_full + d, d % 2, c0=d * tk, nc=tk, triangle=True)
        if d + 1 < n_diag:
            update(n_full + d, d % 2, c0=(d + 1) * tk, nc=tq - (d + 1) * tk)
    out_t = jnp.concatenate([acc_sc[hh, :HEAD_DIM] / acc_sc[hh, HEAD_DIM:HEAD_DIM + 1] for hh in range(2)],
                            axis=0)
    o_ref[0] = out_t.T


def _softmax_attn_call(q, k, vt, *, fox, tq, tk, name):
    bsz, s, _ = q.shape
    kw = 2 * LANES
    kern = functools.partial(_softmax_attn_kernel, fox=fox, tq=tq, tk=tk)
    return pl.pallas_call(
        kern,
        grid=(bsz, 2, s // tq),
        in_specs=[pl.BlockSpec((1, tq, kw), lambda b, p, i: (b, i, p)),
                  pl.BlockSpec((1, s, kw), lambda b, p, i: (b, 0, p)),
                  pl.BlockSpec((1, 2, VT_ROWS, s), lambda b, p, i: (b, p, 0, 0))],
        out_specs=pl.BlockSpec((1, tq, LANES), lambda b, p, i: (b, i, p)),
        out_shape=jax.ShapeDtypeStruct((bsz, s, GROUP_W), jnp.float32),
        scratch_shapes=[pltpu.VMEM((2, tq, kw if fox else LANES), jnp.bfloat16),
                        pltpu.VMEM((2, 2, tk, tq), jnp.float32),
                        pltpu.VMEM((2, 1, tq), jnp.float32),
                        pltpu.VMEM((2, VT_ROWS, tq), jnp.float32)],
        compiler_params=_cparams(3),
        name=name,
    )(q, k, vt)


def _sb_attn_kernel(q_ref, k_ref, v_ref, o_ref, q_sc, z_sc, c_sc, acc_sc, *, tq, tk):
    qi = pl.program_id(2)
    bf16, f32 = jnp.bfloat16, jnp.float32
    lane_q = lax.broadcasted_iota(jnp.int32, (tq, LANES), 1)
    rj = lax.broadcasted_iota(jnp.int32, (tk, tk), 0)
    cs = lax.broadcasted_iota(jnp.int32, (tk, tk), 1)
    upper = jnp.where(rj >= cs, 1.0, 0.0).astype(bf16)
    for hh in range(2):
        in_head = (lane_q >= HEAD_DIM * hh) & (lane_q < HEAD_DIM * (hh + 1))
        q_sc[hh] = jnp.where(in_head, q_ref[0], jnp.zeros((tq, LANES), bf16))
    c_sc[...] = jnp.zeros(c_sc.shape, f32)
    acc_sc[...] = jnp.zeros(acc_sc.shape, f32)

    def scores(kb, slot, r0=0):
        k0 = pl.multiple_of(kb * tk, tk)
        k = k_ref[0, pl.ds(k0, tk), :]
        for hh in range(2):
            z_sc[slot, hh, r0:] = lax.dot_general(q_sc[hh, r0:], k, (((1,), (1,)), ((), ())),
                                                  preferred_element_type=f32)

    def update(kb, slot, r0=0, nr=tq, triangle=False):
        k0 = pl.multiple_of(kb * tk, tk)
        v = v_ref[0, pl.ds(k0, tk), :]
        rows = slice(r0, r0 + nr)
        if triangle:
            valid = (lax.broadcasted_iota(jnp.int32, (nr, tk), 1)
                     < lax.broadcasted_iota(jnp.int32, (nr, tk), 0))
        for hh in range(2):
            z = z_sc[slot, hh, rows]
            sp = jnp.maximum(z, 0.0) + jnp.log2(1.0 + jnp.exp2(-jnp.abs(z)))
            if triangle:
                sp = jnp.where(valid, sp, 0.0)
            csum = (jnp.dot(sp.astype(bf16), upper, preferred_element_type=f32)
                    + jnp.tile(c_sc[hh, rows], (1, tk // LANES)))
            w = jnp.exp2(z - csum)
            if triangle:
                w = jnp.where(valid, w, 0.0)
            acc_sc[hh, rows] += jnp.dot(w.astype(bf16), v, preferred_element_type=f32)
            c_sc[hh, rows] = jnp.broadcast_to(csum[:, 0:1], (nr, LANES))

    n_diag = tq // tk
    assert n_diag % 2 == 0
    n_full = qi * n_diag
    scores(n_full + n_diag - 1, 1, r0=(n_diag - 1) * tk)
    for d in reversed(range(n_diag)):
        kb = n_full + d
        scores(jnp.maximum(kb - 1, 0), (d + 1) % 2, r0=max(d - 1, 0) * tk)
        update(kb, d % 2, r0=d * tk, nr=tk, triangle=True)
        if d + 1 < n_diag:
            update(kb, d % 2, r0=(d + 1) * tk, nr=tq - (d + 1) * tk)

    def body(n, c):
        kb = n_full - 1 - 2 * n
        scores(kb - 1, 0)
        update(kb, 1)
        scores(jnp.maximum(kb - 2, 0), 1)
        update(kb - 1, 0)
        return c

    lax.fori_loop(0, n_full // 2, body, 0)
    o_ref[0] = jnp.where(lane_q < HEAD_DIM, acc_sc[0], acc_sc[1])


def _sb_attn_call(q, k, v, *, tq, tk):
    bsz, s, _ = q.shape
    kern = functools.partial(_sb_attn_kernel, tq=tq, tk=tk)
    return pl.pallas_call(
        kern,
        grid=(bsz, 2, s // tq),
        in_specs=[pl.BlockSpec((1, tq, LANES), lambda b, p, i: (b, i, p)),
                  pl.BlockSpec((1, s, LANES), lambda b, p, i: (b, 0, p)),
                  pl.BlockSpec((1, s, LANES), lambda b, p, i: (b, 0, p))],
        out_specs=pl.BlockSpec((1, tq, LANES), lambda b, p, i: (b, i, p)),
        out_shape=jax.ShapeDtypeStruct((bsz, s, GROUP_W), jnp.float32),
        scratch_shapes=[pltpu.VMEM((2, tq, LANES), jnp.bfloat16),
                        pltpu.VMEM((2, 2, tq, tk), jnp.float32),
                        pltpu.VMEM((2, tq, LANES), jnp.float32),
                        pltpu.VMEM((2, tq, LANES), jnp.float32)],
        compiler_params=_cparams(3),
        name="sb_attn",
    )(q, k, v)


def _out_kernel(of_ref, os_ref, om_ref, ox_ref, gate_ref, h_ref, w_ref, g_ref, b_ref, o_ref, *, alpha):
    bf16, f32 = jnp.bfloat16, jnp.float32
    mixed = jnp.concatenate([of_ref[...], os_ref[...], om_ref[...], ox_ref[...]], axis=1)
    y = jnp.dot((mixed * gate_ref[...]).astype(bf16), w_ref[...], preferred_element_type=f32)
    o_ref[...] = _layer_norm_rows(alpha * h_ref[...] + y, g_ref[...], b_ref[...])


def _out_call(o_fox, o_sb, o_mla, o_mem, gate, h2d, w_out_bf16, g, b, alpha, tm):
    n, d = h2d.shape
    gw = o_fox.shape[1]
    kern = functools.partial(_out_kernel, alpha=alpha)
    blk = lambda c: pl.BlockSpec((tm, c), lambda i: (i, 0))
    return pl.pallas_call(
        kern,
        grid=(n // tm,),
        in_specs=[blk(gw), blk(gw), blk(gw), blk(gw), blk(4 * gw), blk(d),
                  pl.BlockSpec(w_out_bf16.shape, lambda i: (0, 0)),
                  pl.BlockSpec((1, d), lambda i: (0, 0)),
                  pl.BlockSpec((1, d), lambda i: (0, 0))],
        out_specs=blk(d),
        out_shape=jax.ShapeDtypeStruct((n, d), jnp.float32),
        compiler_params=_cparams(1),
        name="out_proj",
    )(o_fox, o_sb, o_mla, o_mem, gate, h2d, w_out_bf16, g.reshape(1, d), b.reshape(1, d))


def _pack_w_in(w):
    d = w.shape[0]
    o = 0
    cols = {}
    for name, width in (("fq", 256), ("fk", 256), ("fv", 256), ("fl", 4), ("sq", 256), ("sk", 256),
                        ("sv", 256), ("cq", 256), ("ckv", 128), ("kr", 32), ("mq", 256), ("gate", 1024)):
        cols[name] = w[:, o:o + width]
        o += width
    z = lambda n: jnp.zeros((d, n), w.dtype)
    half = MLA_ROPE // 2
    kr = cols["kr"]
    kr_rot = jnp.concatenate([-kr[:, half:], kr[:, :half]], axis=1)
    krp = jnp.concatenate([z(MLA_NOPE), kr, z(LANES - MLA_NOPE - MLA_ROPE)], axis=1)
    krrp = jnp.concatenate([z(MLA_NOPE), kr_rot, z(LANES - MLA_NOPE - MLA_ROPE)], axis=1)
    misc = []
    for hh in range(N_HEADS):
        misc += [cols["fl"][:, hh:hh + 1]] * FORGET_DUP + [z(FORGET_STRIDE - FORGET_DUP)]
    misc.append(z(LANES - N_HEADS * FORGET_STRIDE))
    packed = jnp.concatenate([cols["fq"], cols["fk"], cols["sq"], cols["sk"], cols["sv"],
                              cols["cq"], cols["ckv"], krp, krrp, cols["mq"], cols["gate"]] + misc, axis=1)
    assert packed.shape[1] == W_COLS
    return packed.astype(jnp.bfloat16), cols["fv"].T.astype(jnp.bfloat16)


def _pack_forget_bias(b_forget_l):
    row = jnp.zeros((LANES,), jnp.float32)
    for hh in range(N_HEADS):
        row = row.at[FORGET_STRIDE * hh:FORGET_STRIDE * hh + FORGET_DUP].set(b_forget_l[hh])
    return row.reshape(1, LANES)


def _pack_mla_q_up(w):
    r = w.shape[0]
    per = MLA_NOPE + MLA_ROPE
    half = MLA_ROPE // 2
    z = lambda n: jnp.zeros((r, n), w.dtype)
    plain, rot = [], []
    for hh in range(N_HEADS):
        nope = w[:, per * hh:per * hh + MLA_NOPE]
        rope = w[:, per * hh + MLA_NOPE:per * (hh + 1)]
        rope_rot = jnp.concatenate([-rope[:, half:], rope[:, :half]], axis=1)
        plain += [nope, rope, z(LANES - per)]
        rot += [z(MLA_NOPE), rope_rot, z(LANES - per)]
    return jnp.concatenate(plain + rot, axis=1).astype(jnp.bfloat16)


def _pack_mla_kv_up(w):
    r = w.shape[0]
    per = MLA_NOPE + HEAD_DIM
    z = jnp.zeros((r, LANES - MLA_NOPE), w.dtype)
    ks, vs = [], []
    for hh in range(N_HEADS):
        ks += [w[:, per * hh:per * hh + MLA_NOPE], z]
        vs.append(w[:, per * hh + MLA_NOPE:per * (hh + 1)])
    return (jnp.concatenate(ks, axis=1).astype(jnp.bfloat16),
            jnp.concatenate(vs, axis=1).T.astype(jnp.bfloat16))


def _rope_tables(s, q_scale):
    half = MLA_ROPE // 2
    inv_freq = ROPE_THETA ** (-jnp.arange(half, dtype=jnp.float32) / half)
    ang = jnp.arange(s).astype(jnp.float32)[:, None] * inv_freq[None, :]
    cos, sin = jnp.cos(ang), jnp.sin(ang)
    ones = jnp.ones((s, MLA_NOPE), jnp.float32)
    z_nope = jnp.zeros((s, MLA_NOPE), jnp.float32)
    z_pad = jnp.zeros((s, LANES - MLA_NOPE - MLA_ROPE), jnp.float32)
    cos_q = jnp.concatenate([ones, cos, cos, z_pad], axis=1) * q_scale
    sin_q = jnp.concatenate([z_nope, sin, sin, z_pad], axis=1) * q_scale
    cos_k = jnp.concatenate([z_nope, cos, cos, z_pad], axis=1)
    sin_k = jnp.concatenate([z_nope, sin, sin, z_pad], axis=1)
    return cos_q, sin_q, cos_k, sin_k


def kernel(x, mem, ln_in_g, ln_in_b, mem_ln_g, mem_ln_b, w_in, b_forget, mla_q_norm_g, w_mla_q_up,
           mla_kv_norm_g, w_mla_kv_up, w_mem_kv, w_out, ln_g, ln_b):
    bsz, s, d = x.shape
    depth = w_in.shape[0]
    alpha = (2 * depth) ** 0.25
    tm = min(512, s)
    tq = min(1024, s)
    tk_sm = min(512, tq // 2)
    tk_sb = min(256, tq // 2)
    mla_scale = (MLA_NOPE + MLA_ROPE) ** -0.5
    tabs = _rope_tables(s, mla_scale * LOG2E)

    h = _layer_norm_call(x.reshape(bsz * s, d), ln_in_g, ln_in_b, tm).reshape(bsz, s, d)
    mkv = _mem_kv_call(mem, mem_ln_g, mem_ln_b, w_mem_kv.astype(jnp.bfloat16))

    for l in range(depth):
        w_packed, w_fv_t = _pack_w_in(w_in[l])
        w_k_up, w_v_up_t = _pack_mla_kv_up(w_mla_kv_up[l])
        (fq, fk, fvt, sbq, sbk, sbv, mq, mk, mvt, o_mem, gate) = _proj_call(
            h, w_packed, w_fv_t, _pack_forget_bias(b_forget[l]),
            mla_q_norm_g[l].reshape(1, -1), _pack_mla_q_up(w_mla_q_up[l]),
            mla_kv_norm_g[l].reshape(1, -1), w_k_up, w_v_up_t,
            tabs, mkv[l], tm)
        o_fox = _softmax_attn_call(fq, fk, fvt, fox=True, tq=tq, tk=tk_sm, name="fox_attn")
        o_sb = _sb_attn_call(sbq, sbk, sbv, tq=tq, tk=tk_sb)
        o_mla = _softmax_attn_call(mq, mk, mvt, fox=False, tq=tq, tk=tk_sm, name="mla_attn")
        n = bsz * s
        h = _out_call(o_fox.reshape(n, -1), o_sb.reshape(n, -1), o_mla.reshape(n, -1), o_mem.reshape(n, -1),
                      gate.reshape(n, -1), h.reshape(n, d), w_out[l].astype(jnp.bfloat16),
                      ln_g[l], ln_b[l], alpha, tm).reshape(bsz, s, d)
    return h
```

```python
import functools

import jax
import jax.numpy as jnp
from jax import lax
from jax.experimental import pallas as pl
from jax.experimental.pallas import tpu as pltpu

N_HEADS = 4
HEAD_DIM = 64
GROUP_W = N_HEADS * HEAD_DIM
MLA_NOPE = 64
MLA_ROPE = 32
ROPE_THETA = 10000.0
LN_EPS = 1e-5
RMS_EPS = 1e-6
LOG2E = 1.4426950408889634

LANES = 128
VMEM_LIMIT_BYTES = 48 * 1024 * 1024

C_FK = 0
C_SQ, C_SK, C_SV = 256, 512, 768
C_CQ = 1024
C_CKV = 1280
C_KRP = 1408
C_KRRP = 1536
C_MQ = 1664
C_GATE = 1920
C_MISC = 2944
W_COLS = 3072
R_FQ, R_FV = 0, 256
WT_ROWS = 512
FORGET_DUP = 6
FORGET_STRIDE = 8
VT_ROWS = HEAD_DIM + 16
NT_DIMS = (((1,), (1,)), ((), ()))


def _cparams(n_grid):
    return pltpu.CompilerParams(dimension_semantics=("arbitrary",) * n_grid,
                                vmem_limit_bytes=VMEM_LIMIT_BYTES)


def _split3(x):
    p0 = x.astype(jnp.bfloat16).astype(jnp.float32)
    r1 = x - p0
    p1 = r1.astype(jnp.bfloat16).astype(jnp.float32)
    p2 = (r1 - p1).astype(jnp.bfloat16).astype(jnp.float32)
    return p0, p1, p2


def _layer_norm_rows(x, g, b):
    mu = jnp.mean(x, axis=-1, keepdims=True)
    xc = x - mu
    var = jnp.mean(xc * xc, axis=-1, keepdims=True)
    return xc * lax.rsqrt(var + LN_EPS) * g + b


def _rms_norm_rows(x, g):
    ms = jnp.mean(x * x, axis=-1, keepdims=True)
    return x * lax.rsqrt(ms + RMS_EPS) * g


def _ln_kernel(x_ref, g_ref, b_ref, o_ref):
    o_ref[...] = _layer_norm_rows(x_ref[...], g_ref[...], b_ref[...])


def _layer_norm_call(x2d, g, b, tm):
    n, d = x2d.shape
    return pl.pallas_call(
        _ln_kernel,
        grid=(n // tm,),
        in_specs=[pl.BlockSpec((tm, d), lambda i: (i, 0)),
                  pl.BlockSpec((1, d), lambda i: (0, 0)),
                  pl.BlockSpec((1, d), lambda i: (0, 0))],
        out_specs=pl.BlockSpec((tm, d), lambda i: (i, 0)),
        out_shape=jax.ShapeDtypeStruct((n, d), jnp.float32),
        compiler_params=_cparams(1),
        name="ln_in",
    )(x2d, g.reshape(1, d), b.reshape(1, d))


def _mem_kv_kernel(mem_ref, g_ref, b_ref, w_ref, o_ref):
    mem_n = _layer_norm_rows(mem_ref[0], g_ref[...], b_ref[...]).astype(jnp.bfloat16)
    for l in range(w_ref.shape[0]):
        o_ref[l, 0] = jnp.dot(mem_n, w_ref[l], preferred_element_type=jnp.float32).astype(jnp.bfloat16)


def _mem_kv_call(mem, g, b, w_mem_kv_bf16):
    bsz, m, d = mem.shape
    depth, _, c = w_mem_kv_bf16.shape
    return pl.pallas_call(
        _mem_kv_kernel,
        grid=(bsz,),
        in_specs=[pl.BlockSpec((1, m, d), lambda i: (i, 0, 0)),
                  pl.BlockSpec((1, d), lambda i: (0, 0)),
                  pl.BlockSpec((1, d), lambda i: (0, 0)),
                  pl.BlockSpec((depth, d, c), lambda i: (0, 0, 0))],
        out_specs=pl.BlockSpec((depth, 1, m, c), lambda i: (0, i, 0, 0)),
        out_shape=jax.ShapeDtypeStruct((depth, bsz, m, c), jnp.bfloat16),
        compiler_params=_cparams(1),
        name="mem_kv",
    )(mem, g.reshape(1, d), b.reshape(1, d), w_mem_kv_bf16)


def _proj_kernel(h_ref, w_ref, wt_ref, bfor_ref, gq_ref, wqupt_ref, gkv_ref, wkup_ref, wvupt_ref,
                 cqt_ref, sqt_ref, ck_ref, sk_ref, mkv_ref,
                 fqt_ref, fk_ref, fvt_ref, sbq_ref, sbk_ref, sbv_ref,
                 mqt_ref, mk_ref, mvt_ref, omem_ref, gate_ref,
                 carry_ref):
    i = pl.program_id(1)
    tm = h_ref.shape[1]
    bf16, f32 = jnp.bfloat16, jnp.float32
    head_c = (HEAD_DIM ** -0.5) * LOG2E

    hb = h_ref[0].astype(bf16)
    proj = jnp.dot(hb, w_ref[...], preferred_element_type=f32)
    proj_t = lax.dot_general(wt_ref[...], hb, NT_DIMS, preferred_element_type=f32)

    def cols(c0, n):
        return proj[:, c0:c0 + n]

    def store_values_t(vt, out_ref):
        for hh in range(N_HEADS):
            out_ref[0, hh, :HEAD_DIM, :] = vt[HEAD_DIM * hh:HEAD_DIM * (hh + 1), :].astype(bf16)
            out_ref[0, hh, HEAD_DIM:, :] = jnp.ones((VT_ROWS - HEAD_DIM, tm), bf16)

    @pl.when(i == 0)
    def _():
        carry_ref[...] = jnp.zeros_like(carry_ref)

    lane = lax.broadcasted_iota(jnp.int32, (tm, LANES), 1)
    sub = lane % FORGET_STRIDE
    used = (lane < N_HEADS * FORGET_STRIDE) & (sub < FORGET_DUP)
    xf = cols(C_MISC, LANES) + bfor_ref[...]
    log_f = jnp.minimum(xf, 0.0) - jnp.log1p(jnp.exp(-jnp.abs(xf)))
    log_f = jnp.where(used, log_f, 0.0)
    row = lax.broadcasted_iota(jnp.int32, (tm, tm), 0)
    col = lax.broadcasted_iota(jnp.int32, (tm, tm), 1)
    tril = jnp.where(col <= row, 1.0, 0.0).astype(bf16)
    parts = jnp.concatenate([p.astype(bf16) for p in _split3(log_f)], axis=1)
    csum = jnp.dot(tril, parts, preferred_element_type=f32)
    f_cum = (csum[:, :LANES] + csum[:, LANES:2 * LANES]) + csum[:, 2 * LANES:] + carry_ref[...]
    carry_ref[...] = f_cum[tm - 1:tm, :]
    p0, p1, p2 = _split3(f_cum * LOG2E)
    bias_q = jnp.where(sub == 0, p0, jnp.where(sub == 1, p1, jnp.where(sub == 2, p2, 1.0)))
    bias_q_t = jnp.where(used, bias_q, 0.0).T.astype(bf16)
    bias_k = jnp.where(sub == 3, -p0, jnp.where(sub == 4, -p1, jnp.where(sub == 5, -p2, 1.0)))
    bias_k = jnp.where(used, bias_k, 0.0).astype(bf16)

    fk = cols(C_FK, GROUP_W)
    for p in range(2):
        fqt_ref[0, p, :LANES, :] = (proj_t[R_FQ + LANES * p:R_FQ + LANES * (p + 1), :] * head_c).astype(bf16)
        fqt_ref[0, p, LANES:, :] = bias_q_t
        fk_ref[0, :, 2 * LANES * p:2 * LANES * p + LANES] = fk[:, LANES * p:LANES * (p + 1)].astype(bf16)
        fk_ref[0, :, 2 * LANES * p + LANES:2 * LANES * (p + 1)] = bias_k
    store_values_t(proj_t[R_FV:R_FV + GROUP_W, :], fvt_ref)

    sbq_ref[0] = (cols(C_SQ, GROUP_W) * head_c).astype(bf16)
    sbk_ref[0] = cols(C_SK, GROUP_W).astype(bf16)
    sbv_ref[0] = cols(C_SV, GROUP_W).astype(bf16)

    cqn = _rms_norm_rows(cols(C_CQ, 256), gq_ref[...]).astype(bf16)
    q_up_t = lax.dot_general(wqupt_ref[...], cqn, NT_DIMS, preferred_element_type=f32)
    ckvn = _rms_norm_rows(cols(C_CKV, 128), gkv_ref[...]).astype(bf16)
    k_up = jnp.dot(ckvn, wkup_ref[...], preferred_element_type=f32)
    k_rope = cols(C_KRP, LANES) * ck_ref[...] + cols(C_KRRP, LANES) * sk_ref[...]
    nq = N_HEADS * LANES
    for hh in range(N_HEADS):
        sl = slice(LANES * hh, LANES * (hh + 1))
        mqt_ref[0, hh] = (q_up_t[sl, :] * cqt_ref[...]
                          + q_up_t[nq + LANES * hh:nq + LANES * (hh + 1), :] * sqt_ref[...]).astype(bf16)
        mk_ref[0, :, sl] = (k_up[:, sl] + k_rope).astype(bf16)
    store_values_t(lax.dot_general(wvupt_ref[...], ckvn, NT_DIMS, preferred_element_type=f32), mvt_ref)

    mem_q = cols(C_MQ, GROUP_W) * head_c
    lane_h = lax.broadcasted_iota(jnp.int32, (tm, LANES), 1)
    for p in range(2):
        qp = mem_q[:, LANES * p:LANES * (p + 1)]
        kp = mkv_ref[0, :, LANES * p:LANES * (p + 1)]
        vp = mkv_ref[0, :, GROUP_W + LANES * p:GROUP_W + LANES * (p + 1)]
        outs = []
        for hh in range(2):
            in_head = (lane_h >= HEAD_DIM * hh) & (lane_h < HEAD_DIM * (hh + 1))
            qh = jnp.where(in_head, qp, 0.0).astype(bf16)
            s = lax.dot_general(qh, kp, NT_DIMS, preferred_element_type=f32)
            m = jnp.max(s, axis=-1, keepdims=True)
            e = jnp.exp2(s - m)
            pr = e / jnp.sum(e, axis=-1, keepdims=True)
            outs.append(jnp.dot(pr.astype(bf16), vp, preferred_element_type=f32))
        omem_ref[0, :, LANES * p:LANES * (p + 1)] = jnp.where(lane_h < HEAD_DIM, outs[0], outs[1]).astype(bf16)

    for c in range(4):
        g = cols(C_GATE + GROUP_W * c, GROUP_W)
        gate_ref[0, :, GROUP_W * c:GROUP_W * (c + 1)] = (g / (1.0 + jnp.exp(-g))).astype(bf16)


def _proj_call(h, w_packed, w_t, bfor_row, gq, wqupt, gkv, wkup, wvupt, tabs, mkv_l, tm):
    bsz, s, d = h.shape
    m = mkv_l.shape[1]
    cqt, sqt, ck, sk = tabs
    bf16 = jnp.bfloat16
    full2 = lambda shape: pl.BlockSpec(shape, lambda b, i: (0, 0))
    row_blk = lambda c: pl.BlockSpec((1, tm, c), lambda b, i: (b, i, 0))
    tab_blk = pl.BlockSpec((tm, LANES), lambda b, i: (i, 0))
    tab_t_blk = pl.BlockSpec((LANES, tm), lambda b, i: (0, i))
    rows = lambda c, dt: (row_blk(c), jax.ShapeDtypeStruct((bsz, s, c), dt))
    feat = lambda n, r: (pl.BlockSpec((1, n, r, tm), lambda b, i: (b, 0, 0, i)),
                         jax.ShapeDtypeStruct((bsz, n, r, s), bf16))
    outs = [feat(2, 2 * LANES), rows(2 * GROUP_W, bf16), feat(N_HEADS, VT_ROWS),
            rows(GROUP_W, bf16), rows(GROUP_W, bf16), rows(GROUP_W, bf16),
            feat(N_HEADS, LANES), rows(2 * GROUP_W, bf16), feat(N_HEADS, VT_ROWS),
            rows(GROUP_W, bf16), rows(4 * GROUP_W, bf16)]
    return pl.pallas_call(
        _proj_kernel,
        grid=(bsz, s // tm),
        in_specs=[row_blk(d),
                  full2(w_packed.shape), full2(w_t.shape),
                  full2((1, LANES)),
                  full2((1, 256)), full2(wqupt.shape),
                  full2((1, 128)), full2(wkup.shape), full2(wvupt.shape),
                  tab_t_blk, tab_t_blk, tab_blk, tab_blk,
                  pl.BlockSpec((1, m, 2 * GROUP_W), lambda b, i: (b, 0, 0))],
        out_specs=[spec for spec, _ in outs],
        out_shape=[shape for _, shape in outs],
        scratch_shapes=[pltpu.VMEM((1, LANES), jnp.float32)],
        compiler_params=_cparams(2),
        name="proj",
    )(h, w_packed, w_t, bfor_row, gq, wqupt, gkv, wkup, wvupt, cqt, sqt, ck, sk, mkv_l)


def _softmax_attn_kernel(qt_ref, k_ref, vt_ref, o_ref, q_sc, s_sc, m_sc, acc_sc, *, fox, tq, tk):
    pair = pl.program_id(1)
    qi = pl.program_id(2)
    bf16, f32 = jnp.bfloat16, jnp.float32
    kq = q_sc.shape[1]
    for hh in range(2):
        if fox:
            feat = lax.broadcasted_iota(jnp.int32, (kq, tq), 0)
            lo = LANES + FORGET_STRIDE * (2 * pair + hh)
            keep = ((feat >= HEAD_DIM * hh) & (feat < HEAD_DIM * (hh + 1))) | (
                (feat >= lo) & (feat < lo + FORGET_STRIDE))
            q_sc[hh] = jnp.where(keep, qt_ref[0, 0], jnp.zeros((kq, tq), bf16))
        else:
            q_sc[hh] = qt_ref[0, hh]
    m_sc[...] = jnp.full(m_sc.shape, -jnp.inf, f32)
    acc_sc[...] = jnp.zeros(acc_sc.shape, f32)

    def scores(kb, slot, c0=0):
        k0 = pl.multiple_of(kb * tk, tk)
        for hh in range(2):
            k = k_ref[0, pl.ds(k0, tk), :] if fox else k_ref[0, pl.ds(k0, tk), LANES * hh:LANES * (hh + 1)]
            s_sc[slot, hh, :, c0:] = jnp.dot(k, q_sc[hh, :, c0:], preferred_element_type=f32)

    def update(kb, slot, c0=0, nc=tq, triangle=False):
        k0 = pl.multiple_of(kb * tk, tk)
        qs = slice(c0, c0 + nc)
        if triangle:
            causal = (lax.broadcasted_iota(jnp.int32, (tk, nc), 0)
                      <= lax.broadcasted_iota(jnp.int32, (tk, nc), 1))
        for hh in range(2):
            s = s_sc[slot, hh, :, qs]
            if triangle:
                s = jnp.where(causal, s, -jnp.inf)
            m_prev = m_sc[hh, :, qs]
            m_new = jnp.maximum(m_prev, jnp.max(s, axis=0, keepdims=True))
            alpha = jnp.exp2(m_prev - m_new)
            p = jnp.exp2(s - m_new)
            acc_sc[hh, :, qs] = (alpha * acc_sc[hh, :, qs]
                                 + jnp.dot(vt_ref[0, hh, :, pl.ds(k0, tk)], p.astype(bf16),
                                           preferred_element_type=f32))
            m_sc[hh, :, qs] = m_new

    n_diag = tq // tk
    assert n_diag % 2 == 0
    n_full = qi * n_diag
    scores(0, 0)

    def body(j, c):
        kb = 2 * j
        scores(kb + 1, 1)
        update(kb, 0)
        scores(kb + 2, 0)
        update(kb + 1, 1)
        return c

    lax.fori_loop(0, n_full // 2, body, 0)
    for d in range(n_diag):
        if d + 1 < n_diag:
            scores(n_full + d + 1, (d + 1) % 2, c0=(d + 1) * tk)
        update(n_full + d, d % 2, c0=d * tk, nc=tk, triangle=True)
        if d + 1 < n_diag:
            update(n_full + d, d % 2, c0=(d + 1) * tk, nc=tq - (d + 1) * tk)
    out_t = jnp.concatenate([acc_sc[hh, :HEAD_DIM] / acc_sc[hh, HEAD_DIM:HEAD_DIM + 1] for hh in range(2)],
                            axis=0)
    o_ref[0] = out_t.T.astype(o_ref.dtype)


def _softmax_attn_call(qt, k, vt, *, fox, tq, tk, name):
    bsz, nqt, kq, s = qt.shape
    kw = 2 * LANES
    kern = functools.partial(_softmax_attn_kernel, fox=fox, tq=tq, tk=tk)
    return pl.pallas_call(
        kern,
        grid=(bsz, 2, s // tq),
        in_specs=[pl.BlockSpec((1, nqt // 2, kq, tq), lambda b, p, i: (b, p, 0, i)),
                  pl.BlockSpec((1, s, kw), lambda b, p, i: (b, 0, p)),
                  pl.BlockSpec((1, 2, VT_ROWS, s), lambda b, p, i: (b, p, 0, 0))],
        out_specs=pl.BlockSpec((1, tq, LANES), lambda b, p, i: (b, i, p)),
        out_shape=jax.ShapeDtypeStruct((bsz, s, GROUP_W), jnp.bfloat16),
        scratch_shapes=[pltpu.VMEM((2, kq, tq), jnp.bfloat16),
                        pltpu.VMEM((2, 2, tk, tq), jnp.float32),
                        pltpu.VMEM((2, 1, tq), jnp.float32),
                        pltpu.VMEM((2, VT_ROWS, tq), jnp.float32)],
        compiler_params=_cparams(3),
        name=name,
    )(qt, k, vt)


def _sb_attn_kernel(q_ref, k_ref, v_ref, o_ref, q_sc, z_sc, c_sc, acc_sc, *, tq, tk):
    qi = pl.program_id(2)
    bf16, f32 = jnp.bfloat16, jnp.float32
    lane_q = lax.broadcasted_iota(jnp.int32, (tq, LANES), 1)
    rj = lax.broadcasted_iota(jnp.int32, (tk, tk), 0)
    cs = lax.broadcasted_iota(jnp.int32, (tk, tk), 1)
    upper = jnp.where(rj >= cs, 1.0, 0.0).astype(bf16)
    for hh in range(2):
        in_head = (lane_q >= HEAD_DIM * hh) & (lane_q < HEAD_DIM * (hh + 1))
        q_sc[hh] = jnp.where(in_head, q_ref[0], jnp.zeros((tq, LANES), bf16))
    c_sc[...] = jnp.zeros(c_sc.shape, f32)
    acc_sc[...] = jnp.zeros(acc_sc.shape, f32)

    def scores(kb, slot, r0=0):
        k0 = pl.multiple_of(kb * tk, tk)
        k = k_ref[0, pl.ds(k0, tk), :]
        for hh in range(2):
            z_sc[slot, hh, r0:] = lax.dot_general(q_sc[hh, r0:], k, NT_DIMS, preferred_element_type=f32)

    def update(kb, slot, r0=0, nr=tq, triangle=False):
        k0 = pl.multiple_of(kb * tk, tk)
        v = v_ref[0, pl.ds(k0, tk), :]
        rows = slice(r0, r0 + nr)
        if triangle:
            valid = (lax.broadcasted_iota(jnp.int32, (nr, tk), 1)
                     < lax.broadcasted_iota(jnp.int32, (nr, tk), 0))
        for hh in range(2):
            z = z_sc[slot, hh, rows]
            sp = jnp.maximum(z, 0.0) + jnp.log2(1.0 + jnp.exp2(-jnp.abs(z)))
            if triangle:
                sp = jnp.where(valid, sp, 0.0)
            csum = (jnp.dot(sp.astype(bf16), upper, preferred_element_type=f32)
                    + jnp.tile(c_sc[hh, rows], (1, tk // LANES)))
            w = jnp.exp2(z - csum)
            if triangle:
                w = jnp.where(valid, w, 0.0)
            acc_sc[hh, rows] += jnp.dot(w.astype(bf16), v, preferred_element_type=f32)
            c_sc[hh, rows] = jnp.broadcast_to(csum[:, 0:1], (nr, LANES))

    n_diag = tq // tk
    unroll = 4 if n_diag % 4 == 0 else 2
    assert n_diag % unroll == 0
    n_full = qi * n_diag
    scores(n_full + n_diag - 1, 1, r0=(n_diag - 1) * tk)
    for d in reversed(range(n_diag)):
        kb = n_full + d
        scores(jnp.maximum(kb - 1, 0), (d + 1) % 2, r0=max(d - 1, 0) * tk)
        update(kb, d % 2, r0=d * tk, nr=tk, triangle=True)
        if d + 1 < n_diag:
            update(kb, d % 2, r0=(d + 1) * tk, nr=tq - (d + 1) * tk)

    def body(n, c):
        top = n_full - 1 - unroll * n
        for u in range(unroll):
            kb = top - u
            scores(jnp.maximum(kb - 1, 0), u % 2)
            update(kb, (u + 1) % 2)
        return c

    lax.fori_loop(0, n_full // unroll, body, 0)
    o_ref[0] = jnp.where(lane_q < HEAD_DIM, acc_sc[0], acc_sc[1]).astype(o_ref.dtype)


def _sb_attn_call(q, k, v, *, tq, tk):
    bsz, s, _ = q.shape
    kern = functools.partial(_sb_attn_kernel, tq=tq, tk=tk)
    return pl.pallas_call(
        kern,
        grid=(bsz, 2, s // tq),
        in_specs=[pl.BlockSpec((1, tq, LANES), lambda b, p, i: (b, i, p)),
                  pl.BlockSpec((1, s, LANES), lambda b, p, i: (b, 0, p)),
                  pl.BlockSpec((1, s, LANES), lambda b, p, i: (b, 0, p))],
        out_specs=pl.BlockSpec((1, tq, LANES), lambda b, p, i: (b, i, p)),
        out_shape=jax.ShapeDtypeStruct((bsz, s, GROUP_W), jnp.bfloat16),
        scratch_shapes=[pltpu.VMEM((2, tq, LANES), jnp.bfloat16),
                        pltpu.VMEM((2, 2, tq, tk), jnp.float32),
                        pltpu.VMEM((2, tq, LANES), jnp.float32),
                        pltpu.VMEM((2, tq, LANES), jnp.float32)],
        compiler_params=_cparams(3),
        name="sb_attn",
    )(q, k, v)


def _out_kernel(of_ref, os_ref, om_ref, ox_ref, gate_ref, h_ref, w_ref, g_ref, b_ref, o_ref, *, alpha):
    bf16, f32 = jnp.bfloat16, jnp.float32
    mixed = jnp.concatenate([of_ref[...], os_ref[...], om_ref[...], ox_ref[...]], axis=1)
    y = jnp.dot(mixed * gate_ref[...], w_ref[...], preferred_element_type=f32)
    o_ref[...] = _layer_norm_rows(alpha * h_ref[...] + y, g_ref[...], b_ref[...])


def _out_call(o_fox, o_sb, o_mla, o_mem, gate, h2d, w_out_bf16, g, b, alpha, tm):
    n, d = h2d.shape
    gw = o_fox.shape[1]
    kern = functools.partial(_out_kernel, alpha=alpha)
    blk = lambda c: pl.BlockSpec((tm, c), lambda i: (i, 0))
    return pl.pallas_call(
        kern,
        grid=(n // tm,),
        in_specs=[blk(gw), blk(gw), blk(gw), blk(gw), blk(4 * gw), blk(d),
                  pl.BlockSpec(w_out_bf16.shape, lambda i: (0, 0)),
                  pl.BlockSpec((1, d), lambda i: (0, 0)),
                  pl.BlockSpec((1, d), lambda i: (0, 0))],
        out_specs=blk(d),
        out_shape=jax.ShapeDtypeStruct((n, d), jnp.float32),
        compiler_params=_cparams(1),
        name="out_proj",
    )(o_fox, o_sb, o_mla, o_mem, gate, h2d, w_out_bf16, g.reshape(1, d), b.reshape(1, d))


def _pack_w_in(w):
    d = w.shape[0]
    o = 0
    cols = {}
    for name, width in (("fq", 256), ("fk", 256), ("fv", 256), ("fl", 4), ("sq", 256), ("sk", 256),
                        ("sv", 256), ("cq", 256), ("ckv", 128), ("kr", 32), ("mq", 256), ("gate", 1024)):
        cols[name] = w[:, o:o + width]
        o += width
    z = lambda n: jnp.zeros((d, n), w.dtype)
    half = MLA_ROPE // 2
    kr = cols["kr"]
    kr_rot = jnp.concatenate([-kr[:, half:], kr[:, :half]], axis=1)
    krp = jnp.concatenate([z(MLA_NOPE), kr, z(LANES - MLA_NOPE - MLA_ROPE)], axis=1)
    krrp = jnp.concatenate([z(MLA_NOPE), kr_rot, z(LANES - MLA_NOPE - MLA_ROPE)], axis=1)
    misc = []
    for hh in range(N_HEADS):
        misc += [cols["fl"][:, hh:hh + 1]] * FORGET_DUP + [z(FORGET_STRIDE - FORGET_DUP)]
    misc.append(z(LANES - N_HEADS * FORGET_STRIDE))
    packed = jnp.concatenate([cols["fk"], cols["sq"], cols["sk"], cols["sv"], cols["cq"], cols["ckv"],
                              krp, krrp, cols["mq"], cols["gate"]] + misc, axis=1)
    assert packed.shape[1] == W_COLS
    w_t = jnp.concatenate([cols["fq"], cols["fv"]], axis=1).T
    assert w_t.shape[0] == WT_ROWS
    return packed.astype(jnp.bfloat16), w_t.astype(jnp.bfloat16)


def _pack_forget_bias(b_forget_l):
    row = jnp.zeros((LANES,), jnp.float32)
    for hh in range(N_HEADS):
        row = row.at[FORGET_STRIDE * hh:FORGET_STRIDE * hh + FORGET_DUP].set(b_forget_l[hh])
    return row.reshape(1, LANES)


def _pack_mla_q_up_t(w):
    r = w.shape[0]
    per = MLA_NOPE + MLA_ROPE
    half = MLA_ROPE // 2
    z = lambda n: jnp.zeros((r, n), w.dtype)
    plain, rot = [], []
    for hh in range(N_HEADS):
        nope = w[:, per * hh:per * hh + MLA_NOPE]
        rope = w[:, per * hh + MLA_NOPE:per * (hh + 1)]
        rope_rot = jnp.concatenate([-rope[:, half:], rope[:, :half]], axis=1)
        plain += [nope, rope, z(LANES - per)]
        rot += [z(MLA_NOPE), rope_rot, z(LANES - per)]
    return jnp.concatenate(plain + rot, axis=1).T.astype(jnp.bfloat16)


def _pack_mla_kv_up(w):
    r = w.shape[0]
    per = MLA_NOPE + HEAD_DIM
    z = jnp.zeros((r, LANES - MLA_NOPE), w.dtype)
    ks, vs = [], []
    for hh in range(N_HEADS):
        ks += [w[:, per * hh:per * hh + MLA_NOPE], z]
        vs.append(w[:, per * hh + MLA_NOPE:per * (hh + 1)])
    return (jnp.concatenate(ks, axis=1).astype(jnp.bfloat16),
            jnp.concatenate(vs, axis=1).T.astype(jnp.bfloat16))


def _rope_tables(s, q_scale):
    half = MLA_ROPE // 2
    inv_freq = ROPE_THETA ** (-jnp.arange(half, dtype=jnp.float32) / half)
    ang = jnp.arange(s).astype(jnp.float32)[:, None] * inv_freq[None, :]
    cos, sin = jnp.cos(ang), jnp.sin(ang)
    ones = jnp.ones((s, MLA_NOPE), jnp.float32)
    z_nope = jnp.zeros((s, MLA_NOPE), jnp.float32)
    z_pad = jnp.zeros((s, LANES - MLA_NOPE - MLA_ROPE), jnp.float32)
    cos_q = jnp.concatenate([ones, cos, cos, z_pad], axis=1) * q_scale
    sin_q = jnp.concatenate([z_nope, sin, sin, z_pad], axis=1) * q_scale
    cos_k = jnp.concatenate([z_nope, cos, cos, z_pad], axis=1)
    sin_k = jnp.concatenate([z_nope, sin, sin, z_pad], axis=1)
    return cos_q.T, sin_q.T, cos_k, sin_k


def kernel(x, mem, ln_in_g, ln_in_b, mem_ln_g, mem_ln_b, w_in, b_forget, mla_q_norm_g, w_mla_q_up,
           mla_kv_norm_g, w_mla_kv_up, w_mem_kv, w_out, ln_g, ln_b):
    bsz, s, d = x.shape
    depth = w_in.shape[0]
    alpha = (2 * depth) ** 0.25
    tm = min(512, s)
    tq = min(1024, s)
    tk_sm = min(512, tq // 2)
    tk_sb = min(256, tq // 2)
    mla_scale = (MLA_NOPE + MLA_ROPE) ** -0.5
    tabs = _rope_tables(s, mla_scale * LOG2E)

    h = _layer_norm_call(x.reshape(bsz * s, d), ln_in_g, ln_in_b, tm).reshape(bsz, s, d)
    mkv = _mem_kv_call(mem, mem_ln_g, mem_ln_b, w_mem_kv.astype(jnp.bfloat16))

    for l in range(depth):
        w_packed, w_t = _pack_w_in(w_in[l])
        w_k_up, w_v_up_t = _pack_mla_kv_up(w_mla_kv_up[l])
        (fqt, fk, fvt, sbq, sbk, sbv, mqt, mk, mvt, o_mem, gate) = _proj_call(
            h, w_packed, w_t, _pack_forget_bias(b_forget[l]),
            mla_q_norm_g[l].reshape(1, -1), _pack_mla_q_up_t(w_mla_q_up[l]),
            mla_kv_norm_g[l].reshape(1, -1), w_k_up, w_v_up_t,
            tabs, mkv[l], tm)
        o_fox = _softmax_attn_call(fqt, fk, fvt, fox=True, tq=tq, tk=tk_sm, name="fox_attn")
        o_sb = _sb_attn_call(sbq, sbk, sbv, tq=tq, tk=tk_sb)
        o_mla = _softmax_attn_call(mqt, mk, mvt, fox=False, tq=tq, tk=tk_sm, name="mla_attn")
        n = bsz * s
        h = _out_call(o_fox.reshape(n, -1), o_sb.reshape(n, -1), o_mla.reshape(n, -1), o_mem.reshape(n, -1),
                      gate.reshape(n, -1), h.reshape(n, d), w_out[l].astype(jnp.bfloat16),
                      ln_g[l], ln_b[l], alpha, tm).reshape(bsz, s, d)
    return h
```

```python
import functools

import jax
import jax.numpy as jnp
from jax import lax
from jax.experimental import pallas as pl
from jax.experimental.pallas import tpu as pltpu

N_HEADS = 4
HEAD_DIM = 64
GROUP_W = N_HEADS * HEAD_DIM
MLA_NOPE = 64
MLA_ROPE = 32
ROPE_THETA = 10000.0
LN_EPS = 1e-5
RMS_EPS = 1e-6
LOG2E = 1.4426950408889634

LANES = 128
VMEM_LIMIT_BYTES = 48 * 1024 * 1024

C_FK = 0
C_SQ, C_SK, C_SV = 256, 512, 768
C_CQ = 1024
C_CKV = 1280
C_KRP = 1408
C_KRRP = 1536
C_MQ = 1664
C_GATE = 1920
C_MISC = 2944
W_COLS = 3072
R_FQ, R_FV = 0, 256
WT_ROWS = 512
FORGET_DUP = 6
FORGET_STRIDE = 8
VT_ROWS = HEAD_DIM + 16
NT_DIMS = (((1,), (1,)), ((), ()))


def _cparams(n_grid):
    return pltpu.CompilerParams(dimension_semantics=("arbitrary",) * n_grid,
                                vmem_limit_bytes=VMEM_LIMIT_BYTES)


def _split3(x):
    p0 = x.astype(jnp.bfloat16).astype(jnp.float32)
    r1 = x - p0
    p1 = r1.astype(jnp.bfloat16).astype(jnp.float32)
    p2 = (r1 - p1).astype(jnp.bfloat16).astype(jnp.float32)
    return p0, p1, p2


def _layer_norm_rows(x, g, b):
    mu = jnp.mean(x, axis=-1, keepdims=True)
    xc = x - mu
    var = jnp.mean(xc * xc, axis=-1, keepdims=True)
    return xc * lax.rsqrt(var + LN_EPS) * g + b


def _rms_norm_rows(x, g):
    ms = jnp.mean(x * x, axis=-1, keepdims=True)
    return x * lax.rsqrt(ms + RMS_EPS) * g


def _mem_kv_kernel(mem_ref, g_ref, b_ref, w_ref, o_ref):
    mem_n = _layer_norm_rows(mem_ref[0], g_ref[...], b_ref[...]).astype(jnp.bfloat16)
    for l in range(w_ref.shape[0]):
        o_ref[l, 0] = jnp.dot(mem_n, w_ref[l], preferred_element_type=jnp.float32).astype(jnp.bfloat16)


def _mem_kv_call(mem, g, b, w_mem_kv_bf16):
    bsz, m, d = mem.shape
    depth, _, c = w_mem_kv_bf16.shape
    return pl.pallas_call(
        _mem_kv_kernel,
        grid=(bsz,),
        in_specs=[pl.BlockSpec((1, m, d), lambda i: (i, 0, 0)),
                  pl.BlockSpec((1, d), lambda i: (0, 0)),
                  pl.BlockSpec((1, d), lambda i: (0, 0)),
                  pl.BlockSpec((depth, d, c), lambda i: (0, 0, 0))],
        out_specs=pl.BlockSpec((depth, 1, m, c), lambda i: (0, i, 0, 0)),
        out_shape=jax.ShapeDtypeStruct((depth, bsz, m, c), jnp.bfloat16),
        compiler_params=_cparams(1),
        name="mem_kv",
    )(mem, g.reshape(1, d), b.reshape(1, d), w_mem_kv_bf16)


def _proj_kernel(h_ref, lng_ref, lnb_ref, w_ref, wt_ref, bfor_ref, gq_ref, wqupt_ref, gkv_ref, wkup_ref,
                 wvupt_ref, cqt_ref, sqt_ref, ck_ref, sk_ref, mkv_ref, *rest, pre_ln):
    hout_ref = rest[0] if pre_ln else None
    (fqt_ref, fk_ref, fvt_ref, sbq_ref, sbk_ref, sbv_ref,
     mqt_ref, mk_ref, mvt_ref, omem_ref, gate_ref, carry_ref) = rest[1:] if pre_ln else rest
    i = pl.program_id(1)
    tm = h_ref.shape[1]
    bf16, f32 = jnp.bfloat16, jnp.float32
    head_c = (HEAD_DIM ** -0.5) * LOG2E

    @pl.when(i == 0)
    def _():
        carry_ref[...] = jnp.zeros_like(carry_ref)

    def sub_block(r0, nr):
        rs = slice(r0, r0 + nr)
        h = h_ref[0, rs]
        if pre_ln:
            h = _layer_norm_rows(h, lng_ref[...], lnb_ref[...])
            hout_ref[0, rs] = h
        hb = h.astype(bf16)
        proj = jnp.dot(hb, w_ref[...], preferred_element_type=f32)
        proj_t = lax.dot_general(wt_ref[...], hb, NT_DIMS, preferred_element_type=f32)

        def cols(c0, n):
            return proj[:, c0:c0 + n]

        def store_values_t(vt, out_ref):
            for hh in range(N_HEADS):
                out_ref[0, hh, :HEAD_DIM, rs] = vt[HEAD_DIM * hh:HEAD_DIM * (hh + 1), :].astype(bf16)
                out_ref[0, hh, HEAD_DIM:, rs] = jnp.ones((VT_ROWS - HEAD_DIM, nr), bf16)

        lane = lax.broadcasted_iota(jnp.int32, (nr, LANES), 1)
        sub = lane % FORGET_STRIDE
        used = (lane < N_HEADS * FORGET_STRIDE) & (sub < FORGET_DUP)
        xf = cols(C_MISC, LANES) + bfor_ref[...]
        log_f = jnp.minimum(xf, 0.0) - jnp.log1p(jnp.exp(-jnp.abs(xf)))
        log_f = jnp.where(used, log_f, 0.0)
        row = lax.broadcasted_iota(jnp.int32, (nr, nr), 0)
        col = lax.broadcasted_iota(jnp.int32, (nr, nr), 1)
        tril = jnp.where(col <= row, 1.0, 0.0).astype(bf16)
        parts = jnp.concatenate([p.astype(bf16) for p in _split3(log_f)], axis=1)
        csum = jnp.dot(tril, parts, preferred_element_type=f32)
        f_cum = (csum[:, :LANES] + csum[:, LANES:2 * LANES]) + csum[:, 2 * LANES:] + carry_ref[...]
        carry_ref[...] = f_cum[nr - 1:nr, :]
        p0, p1, p2 = _split3(f_cum * LOG2E)
        bias_q = jnp.where(sub == 0, p0, jnp.where(sub == 1, p1, jnp.where(sub == 2, p2, 1.0)))
        bias_q_t = jnp.where(used, bias_q, 0.0).T.astype(bf16)
        bias_k = jnp.where(sub == 3, -p0, jnp.where(sub == 4, -p1, jnp.where(sub == 5, -p2, 1.0)))
        bias_k = jnp.where(used, bias_k, 0.0).astype(bf16)

        fk = cols(C_FK, GROUP_W)
        for p in range(2):
            fqt_ref[0, p, :LANES, rs] = (proj_t[R_FQ + LANES * p:R_FQ + LANES * (p + 1), :] * head_c).astype(bf16)
            fqt_ref[0, p, LANES:, rs] = bias_q_t
            fk_ref[0, rs, 2 * LANES * p:2 * LANES * p + LANES] = fk[:, LANES * p:LANES * (p + 1)].astype(bf16)
            fk_ref[0, rs, 2 * LANES * p + LANES:2 * LANES * (p + 1)] = bias_k
        store_values_t(proj_t[R_FV:R_FV + GROUP_W, :], fvt_ref)

        sbq_ref[0, rs] = (cols(C_SQ, GROUP_W) * head_c).astype(bf16)
        sbk_ref[0, rs] = cols(C_SK, GROUP_W).astype(bf16)
        sbv_ref[0, rs] = cols(C_SV, GROUP_W).astype(bf16)

        cqn = _rms_norm_rows(cols(C_CQ, 256), gq_ref[...]).astype(bf16)
        q_up_t = lax.dot_general(wqupt_ref[...], cqn, NT_DIMS, preferred_element_type=f32)
        ckvn = _rms_norm_rows(cols(C_CKV, 128), gkv_ref[...]).astype(bf16)
        k_up = jnp.dot(ckvn, wkup_ref[...], preferred_element_type=f32)
        k_rope = cols(C_KRP, LANES) * ck_ref[rs] + cols(C_KRRP, LANES) * sk_ref[rs]
        nq = N_HEADS * LANES
        for hh in range(N_HEADS):
            sl = slice(LANES * hh, LANES * (hh + 1))
            mqt_ref[0, hh, :, rs] = (q_up_t[sl, :] * cqt_ref[:, rs]
                                     + q_up_t[nq + LANES * hh:nq + LANES * (hh + 1), :] * sqt_ref[:, rs]).astype(bf16)
            mk_ref[0, rs, sl] = (k_up[:, sl] + k_rope).astype(bf16)
        store_values_t(lax.dot_general(wvupt_ref[...], ckvn, NT_DIMS, preferred_element_type=f32), mvt_ref)

        mem_q = cols(C_MQ, GROUP_W) * head_c
        lane_h = lax.broadcasted_iota(jnp.int32, (nr, LANES), 1)
        for p in range(2):
            qp = mem_q[:, LANES * p:LANES * (p + 1)]
            kp = mkv_ref[0, :, LANES * p:LANES * (p + 1)]
            vp = mkv_ref[0, :, GROUP_W + LANES * p:GROUP_W + LANES * (p + 1)]
            outs = []
            for hh in range(2):
                in_head = (lane_h >= HEAD_DIM * hh) & (lane_h < HEAD_DIM * (hh + 1))
                qh = jnp.where(in_head, qp, 0.0).astype(bf16)
                s = lax.dot_general(qh, kp, NT_DIMS, preferred_element_type=f32)
                m = jnp.max(s, axis=-1, keepdims=True)
                e = jnp.exp2(s - m)
                pr = e / jnp.sum(e, axis=-1, keepdims=True)
                outs.append(jnp.dot(pr.astype(bf16), vp, preferred_element_type=f32))
            omem_ref[0, rs, LANES * p:LANES * (p + 1)] = jnp.where(lane_h < HEAD_DIM, outs[0], outs[1]).astype(bf16)

        for c in range(4):
            g = cols(C_GATE + GROUP_W * c, GROUP_W)
            gate_ref[0, rs, GROUP_W * c:GROUP_W * (c + 1)] = (g / (1.0 + jnp.exp(-g))).astype(bf16)

    n_sub = 2 if tm % (2 * LANES) == 0 else 1
    for j in range(n_sub):
        sub_block(j * (tm // n_sub), tm // n_sub)


def _proj_call(h, ln_gb, w_packed, w_t, bfor_row, gq, wqupt, gkv, wkup, wvupt, tabs, mkv_l, tm, pre_ln):
    bsz, s, d = h.shape
    ln_g, ln_b = ln_gb
    m = mkv_l.shape[1]
    cqt, sqt, ck, sk = tabs
    bf16 = jnp.bfloat16
    full2 = lambda shape: pl.BlockSpec(shape, lambda b, i: (0, 0))
    row_blk = lambda c: pl.BlockSpec((1, tm, c), lambda b, i: (b, i, 0))
    tab_blk = pl.BlockSpec((tm, LANES), lambda b, i: (i, 0))
    tab_t_blk = pl.BlockSpec((LANES, tm), lambda b, i: (0, i))
    rows = lambda c, dt: (row_blk(c), jax.ShapeDtypeStruct((bsz, s, c), dt))
    feat = lambda n, r: (pl.BlockSpec((1, n, r, tm), lambda b, i: (b, 0, 0, i)),
                         jax.ShapeDtypeStruct((bsz, n, r, s), bf16))
    outs = [feat(2, 2 * LANES), rows(2 * GROUP_W, bf16), feat(N_HEADS, VT_ROWS),
            rows(GROUP_W, bf16), rows(GROUP_W, bf16), rows(GROUP_W, bf16),
            feat(N_HEADS, LANES), rows(2 * GROUP_W, bf16), feat(N_HEADS, VT_ROWS),
            rows(GROUP_W, bf16), rows(4 * GROUP_W, bf16)]
    if pre_ln:
        outs = [rows(d, jnp.float32)] + outs
    return pl.pallas_call(
        functools.partial(_proj_kernel, pre_ln=pre_ln),
        grid=(bsz, s // tm),
        in_specs=[row_blk(d), full2((1, d)), full2((1, d)),
                  full2(w_packed.shape), full2(w_t.shape),
                  full2((1, LANES)),
                  full2((1, 256)), full2(wqupt.shape),
                  full2((1, 128)), full2(wkup.shape), full2(wvupt.shape),
                  tab_t_blk, tab_t_blk, tab_blk, tab_blk,
                  pl.BlockSpec((1, m, 2 * GROUP_W), lambda b, i: (b, 0, 0))],
        out_specs=[spec for spec, _ in outs],
        out_shape=[shape for _, shape in outs],
        scratch_shapes=[pltpu.VMEM((1, LANES), jnp.float32)],
        compiler_params=_cparams(2),
        name="proj",
    )(h, ln_g.reshape(1, d), ln_b.reshape(1, d), w_packed, w_t, bfor_row, gq, wqupt, gkv, wkup, wvupt,
      cqt, sqt, ck, sk, mkv_l)


def _softmax_attn_kernel(qt_ref, k_ref, vt_ref, o_ref, q_sc, s_sc, m_sc, acc_sc, *, fox, tq, tk):
    pair = pl.program_id(1)
    qi = pl.program_id(2)
    bf16, f32 = jnp.bfloat16, jnp.float32
    kq = q_sc.shape[1]
    for hh in range(2):
        if fox:
            feat = lax.broadcasted_iota(jnp.int32, (kq, tq), 0)
            lo = LANES + FORGET_STRIDE * (2 * pair + hh)
            keep = ((feat >= HEAD_DIM * hh) & (feat < HEAD_DIM * (hh + 1))) | (
                (feat >= lo) & (feat < lo + FORGET_STRIDE))
            q_sc[hh] = jnp.where(keep, qt_ref[0, 0], jnp.zeros((kq, tq), bf16))
        else:
            q_sc[hh] = qt_ref[0, hh]
    m_sc[...] = jnp.full(m_sc.shape, -jnp.inf, f32)
    acc_sc[...] = jnp.zeros(acc_sc.shape, f32)

    def scores(kb, slot, c0=0):
        k0 = pl.multiple_of(kb * tk, tk)
        for hh in range(2):
            k = k_ref[0, pl.ds(k0, tk), :] if fox else k_ref[0, pl.ds(k0, tk), LANES * hh:LANES * (hh + 1)]
            s_sc[slot, hh, :, c0:] = jnp.dot(k, q_sc[hh, :, c0:], preferred_element_type=f32)

    def update(kb, slot, c0=0, nc=tq, triangle=False):
        k0 = pl.multiple_of(kb * tk, tk)
        qs = slice(c0, c0 + nc)
        if triangle:
            causal = (lax.broadcasted_iota(jnp.int32, (tk, nc), 0)
                      <= lax.broadcasted_iota(jnp.int32, (tk, nc), 1))
        for hh in range(2):
            s = s_sc[slot, hh, :, qs]
            if triangle:
                s = jnp.where(causal, s, -jnp.inf)
            m_prev = m_sc[hh, :, qs]
            m_new = jnp.maximum(m_prev, jnp.max(s, axis=0, keepdims=True))
            alpha = jnp.exp2(m_prev - m_new)
            p = jnp.exp2(s - m_new)
            acc_sc[hh, :, qs] = (alpha * acc_sc[hh, :, qs]
                                 + jnp.dot(vt_ref[0, hh, :, pl.ds(k0, tk)], p.astype(bf16),
                                           preferred_element_type=f32))
            m_sc[hh, :, qs] = m_new

    n_diag = tq // tk
    assert n_diag % 2 == 0
    n_full = qi * n_diag
    scores(0, 0)

    def body(j, c):
        kb = 2 * j
        scores(kb + 1, 1)
        update(kb, 0)
        scores(kb + 2, 0)
        update(kb + 1, 1)
        return c

    lax.fori_loop(0, n_full // 2, body, 0)
    for d in range(n_diag):
        if d + 1 < n_diag:
            scores(n_full + d + 1, (d + 1) % 2, c0=(d + 1) * tk)
        update(n_full + d, d % 2, c0=d * tk, nc=tk, triangle=True)
        if d + 1 < n_diag:
            update(n_full + d, d % 2, c0=(d + 1) * tk, nc=tq - (d + 1) * tk)
    out_t = jnp.concatenate([acc_sc[hh, :HEAD_DIM] / acc_sc[hh, HEAD_DIM:HEAD_DIM + 1] for hh in range(2)],
                            axis=0)
    o_ref[0] = out_t.T.astype(o_ref.dtype)


def _softmax_attn_call(qt, k, vt, *, fox, tq, tk, name):
    bsz, nqt, kq, s = qt.shape
    kw = 2 * LANES
    kern = functools.partial(_softmax_attn_kernel, fox=fox, tq=tq, tk=tk)
    return pl.pallas_call(
        kern,
        grid=(bsz, 2, s // tq),
        in_specs=[pl.BlockSpec((1, nqt // 2, kq, tq), lambda b, p, i: (b, p, 0, i)),
                  pl.BlockSpec((1, s, kw), lambda b, p, i: (b, 0, p)),
                  pl.BlockSpec((1, 2, VT_ROWS, s), lambda b, p, i: (b, p, 0, 0))],
        out_specs=pl.BlockSpec((1, tq, LANES), lambda b, p, i: (b, i, p)),
        out_shape=jax.ShapeDtypeStruct((bsz, s, GROUP_W), jnp.bfloat16),
        scratch_shapes=[pltpu.VMEM((2, kq, tq), jnp.bfloat16),
                        pltpu.VMEM((2, 2, tk, tq), jnp.float32),
                        pltpu.VMEM((2, 1, tq), jnp.float32),
                        pltpu.VMEM((2, VT_ROWS, tq), jnp.float32)],
        compiler_params=_cparams(3),
        name=name,
    )(qt, k, vt)


def _sb_attn_kernel(q_ref, k_ref, v_ref, o_ref, q_sc, z_sc, c_sc, acc_sc, *, tq, tk):
    qi = pl.program_id(2)
    bf16, f32 = jnp.bfloat16, jnp.float32
    lane_q = lax.broadcasted_iota(jnp.int32, (tq, LANES), 1)
    rj = lax.broadcasted_iota(jnp.int32, (tk, tk), 0)
    cs = lax.broadcasted_iota(jnp.int32, (tk, tk), 1)
    upper = jnp.where(rj >= cs, 1.0, 0.0).astype(bf16)
    for hh in range(2):
        in_head = (lane_q >= HEAD_DIM * hh) & (lane_q < HEAD_DIM * (hh + 1))
        q_sc[hh] = jnp.where(in_head, q_ref[0], jnp.zeros((tq, LANES), bf16))
    c_sc[...] = jnp.zeros(c_sc.shape, f32)
    acc_sc[...] = jnp.zeros(acc_sc.shape, f32)

    def scores(kb, slot, r0=0):
        k0 = pl.multiple_of(kb * tk, tk)
        k = k_ref[0, pl.ds(k0, tk), :]
        for hh in range(2):
            z_sc[slot, hh, r0:] = lax.dot_general(q_sc[hh, r0:], k, NT_DIMS, preferred_element_type=f32)

    def update(kb, slot, r0=0, nr=tq, triangle=False):
        k0 = pl.multiple_of(kb * tk, tk)
        v = v_ref[0, pl.ds(k0, tk), :]
        rows = slice(r0, r0 + nr)
        if triangle:
            valid = (lax.broadcasted_iota(jnp.int32, (nr, tk), 1)
                     < lax.broadcasted_iota(jnp.int32, (nr, tk), 0))
        for hh in range(2):
            z = z_sc[slot, hh, rows]
            sp = jnp.maximum(z, 0.0) + jnp.log2(1.0 + jnp.exp2(-jnp.abs(z)))
            if triangle:
                sp = jnp.where(valid, sp, 0.0)
            csum = (jnp.dot(sp.astype(bf16), upper, preferred_element_type=f32)
                    + jnp.tile(c_sc[hh, rows], (1, tk // LANES)))
            w = jnp.exp2(z - csum)
            if triangle:
                w = jnp.where(valid, w, 0.0)
            acc_sc[hh, rows] += jnp.dot(w.astype(bf16), v, preferred_element_type=f32)
            c_sc[hh, rows] = jnp.broadcast_to(csum[:, 0:1], (nr, LANES))

    n_diag = tq // tk
    unroll = 4 if n_diag % 4 == 0 else 2
    assert n_diag % unroll == 0
    n_full = qi * n_diag
    scores(n_full + n_diag - 1, 1, r0=(n_diag - 1) * tk)
    for d in reversed(range(n_diag)):
        kb = n_full + d
        scores(jnp.maximum(kb - 1, 0), (d + 1) % 2, r0=max(d - 1, 0) * tk)
        update(kb, d % 2, r0=d * tk, nr=tk, triangle=True)
        if d + 1 < n_diag:
            update(kb, d % 2, r0=(d + 1) * tk, nr=tq - (d + 1) * tk)

    def body(n, c):
        top = n_full - 1 - unroll * n
        for u in range(unroll):
            kb = top - u
            scores(jnp.maximum(kb - 1, 0), u % 2)
            update(kb, (u + 1) % 2)
        return c

    lax.fori_loop(0, n_full // unroll, body, 0)
    o_ref[0] = jnp.where(lane_q < HEAD_DIM, acc_sc[0], acc_sc[1]).astype(o_ref.dtype)


def _sb_attn_call(q, k, v, *, tq, tk):
    bsz, s, _ = q.shape
    kern = functools.partial(_sb_attn_kernel, tq=tq, tk=tk)
    return pl.pallas_call(
        kern,
        grid=(bsz, 2, s // tq),
        in_specs=[pl.BlockSpec((1, tq, LANES), lambda b, p, i: (b, i, p)),
                  pl.BlockSpec((1, s, LANES), lambda b, p, i: (b, 0, p)),
                  pl.BlockSpec((1, s, LANES), lambda b, p, i: (b, 0, p))],
        out_specs=pl.BlockSpec((1, tq, LANES), lambda b, p, i: (b, i, p)),
        out_shape=jax.ShapeDtypeStruct((bsz, s, GROUP_W), jnp.bfloat16),
        scratch_shapes=[pltpu.VMEM((2, tq, LANES), jnp.bfloat16),
                        pltpu.VMEM((2, 2, tq, tk), jnp.float32),
                        pltpu.VMEM((2, tq, LANES), jnp.float32),
                        pltpu.VMEM((2, tq, LANES), jnp.float32)],
        compiler_params=_cparams(3),
        name="sb_attn",
    )(q, k, v)


def _out_kernel(of_ref, os_ref, om_ref, ox_ref, gate_ref, h_ref, w_ref, g_ref, b_ref, o_ref, *, alpha):
    bf16, f32 = jnp.bfloat16, jnp.float32
    mixed = jnp.concatenate([of_ref[...], os_ref[...], om_ref[...], ox_ref[...]], axis=1)
    y = jnp.dot(mixed * gate_ref[...], w_ref[...], preferred_element_type=f32)
    o_ref[...] = _layer_norm_rows(alpha * h_ref[...] + y, g_ref[...], b_ref[...])


def _out_call(o_fox, o_sb, o_mla, o_mem, gate, h2d, w_out_bf16, g, b, alpha, tm):
    n, d = h2d.shape
    gw = o_fox.shape[1]
    kern = functools.partial(_out_kernel, alpha=alpha)
    blk = lambda c: pl.BlockSpec((tm, c), lambda i: (i, 0))
    return pl.pallas_call(
        kern,
        grid=(n // tm,),
        in_specs=[blk(gw), blk(gw), blk(gw), blk(gw), blk(4 * gw), blk(d),
                  pl.BlockSpec(w_out_bf16.shape, lambda i: (0, 0)),
                  pl.BlockSpec((1, d), lambda i: (0, 0)),
                  pl.BlockSpec((1, d), lambda i: (0, 0))],
        out_specs=blk(d),
        out_shape=jax.ShapeDtypeStruct((n, d), jnp.float32),
        compiler_params=_cparams(1),
        name="out_proj",
    )(o_fox, o_sb, o_mla, o_mem, gate, h2d, w_out_bf16, g.reshape(1, d), b.reshape(1, d))


def _pack_w_in(w):
    d = w.shape[0]
    o = 0
    cols = {}
    for name, width in (("fq", 256), ("fk", 256), ("fv", 256), ("fl", 4), ("sq", 256), ("sk", 256),
                        ("sv", 256), ("cq", 256), ("ckv", 128), ("kr", 32), ("mq", 256), ("gate", 1024)):
        cols[name] = w[:, o:o + width]
        o += width
    z = lambda n: jnp.zeros((d, n), w.dtype)
    half = MLA_ROPE // 2
    kr = cols["kr"]
    kr_rot = jnp.concatenate([-kr[:, half:], kr[:, :half]], axis=1)
    krp = jnp.concatenate([z(MLA_NOPE), kr, z(LANES - MLA_NOPE - MLA_ROPE)], axis=1)
    krrp = jnp.concatenate([z(MLA_NOPE), kr_rot, z(LANES - MLA_NOPE - MLA_ROPE)], axis=1)
    misc = []
    for hh in range(N_HEADS):
        misc += [cols["fl"][:, hh:hh + 1]] * FORGET_DUP + [z(FORGET_STRIDE - FORGET_DUP)]
    misc.append(z(LANES - N_HEADS * FORGET_STRIDE))
    packed = jnp.concatenate([cols["fk"], cols["sq"], cols["sk"], cols["sv"], cols["cq"], cols["ckv"],
                              krp, krrp, cols["mq"], cols["gate"]] + misc, axis=1)
    assert packed.shape[1] == W_COLS
    w_t = jnp.concatenate([cols["fq"], cols["fv"]], axis=1).T
    assert w_t.shape[0] == WT_ROWS
    return packed.astype(jnp.bfloat16), w_t.astype(jnp.bfloat16)


def _pack_forget_bias(b_forget_l):
    row = jnp.zeros((LANES,), jnp.float32)
    for hh in range(N_HEADS):
        row = row.at[FORGET_STRIDE * hh:FORGET_STRIDE * hh + FORGET_DUP].set(b_forget_l[hh])
    return row.reshape(1, LANES)


def _pack_mla_q_up_t(w):
    r = w.shape[0]
    per = MLA_NOPE + MLA_ROPE
    half = MLA_ROPE // 2
    z = lambda n: jnp.zeros((r, n), w.dtype)
    plain, rot = [], []
    for hh in range(N_HEADS):
        nope = w[:, per * hh:per * hh + MLA_NOPE]
        rope = w[:, per * hh + MLA_NOPE:per * (hh + 1)]
        rope_rot = jnp.concatenate([-rope[:, half:], rope[:, :half]], axis=1)
        plain += [nope, rope, z(LANES - per)]
        rot += [z(MLA_NOPE), rope_rot, z(LANES - per)]
    return jnp.concatenate(plain + rot, axis=1).T.astype(jnp.bfloat16)


def _pack_mla_kv_up(w):
    r = w.shape[0]
    per = MLA_NOPE + HEAD_DIM
    z = jnp.zeros((r, LANES - MLA_NOPE), w.dtype)
    ks, vs = [], []
    for hh in range(N_HEADS):
        ks += [w[:, per * hh:per * hh + MLA_NOPE], z]
        vs.append(w[:, per * hh + MLA_NOPE:per * (hh + 1)])
    return (jnp.concatenate(ks, axis=1).astype(jnp.bfloat16),
            jnp.concatenate(vs, axis=1).T.astype(jnp.bfloat16))


def _rope_tables(s, q_scale):
    half = MLA_ROPE // 2
    inv_freq = ROPE_THETA ** (-jnp.arange(half, dtype=jnp.float32) / half)
    ang = jnp.arange(s).astype(jnp.float32)[:, None] * inv_freq[None, :]
    cos, sin = jnp.cos(ang), jnp.sin(ang)
    ones = jnp.ones((s, MLA_NOPE), jnp.float32)
    z_nope = jnp.zeros((s, MLA_NOPE), jnp.float32)
    z_pad = jnp.zeros((s, LANES - MLA_NOPE - MLA_ROPE), jnp.float32)
    cos_q = jnp.concatenate([ones, cos, cos, z_pad], axis=1) * q_scale
    sin_q = jnp.concatenate([z_nope, sin, sin, z_pad], axis=1) * q_scale
    cos_k = jnp.concatenate([z_nope, cos, cos, z_pad], axis=1)
    sin_k = jnp.concatenate([z_nope, sin, sin, z_pad], axis=1)
    return cos_q.T, sin_q.T, cos_k, sin_k


def kernel(x, mem, ln_in_g, ln_in_b, mem_ln_g, mem_ln_b, w_in, b_forget, mla_q_norm_g, w_mla_q_up,
           mla_kv_norm_g, w_mla_kv_up, w_mem_kv, w_out, ln_g, ln_b):
    bsz, s, d = x.shape
    depth = w_in.shape[0]
    alpha = (2 * depth) ** 0.25
    tm = min(512, s)
    tq = min(1024, s)
    tk_sm = min(512, tq // 2)
    tk_sb = min(256, tq // 2)
    mla_scale = (MLA_NOPE + MLA_ROPE) ** -0.5
    tabs = _rope_tables(s, mla_scale * LOG2E)

    mkv = _mem_kv_call(mem, mem_ln_g, mem_ln_b, w_mem_kv.astype(jnp.bfloat16))

    h = x
    for l in range(depth):
        w_packed, w_t = _pack_w_in(w_in[l])
        w_k_up, w_v_up_t = _pack_mla_kv_up(w_mla_kv_up[l])
        outs = _proj_call(
            h, (ln_in_g, ln_in_b), w_packed, w_t, _pack_forget_bias(b_forget[l]),
            mla_q_norm_g[l].reshape(1, -1), _pack_mla_q_up_t(w_mla_q_up[l]),
            mla_kv_norm_g[l].reshape(1, -1), w_k_up, w_v_up_t,
            tabs, mkv[l], tm, pre_ln=(l == 0))
        if l == 0:
            h, outs = outs[0], outs[1:]
        (fqt, fk, fvt, sbq, sbk, sbv, mqt, mk, mvt, o_mem, gate) = outs
        o_fox = _softmax_attn_call(fqt, fk, fvt, fox=True, tq=tq, tk=tk_sm, name="fox_attn")
        o_sb = _sb_attn_call(sbq, sbk, sbv, tq=tq, tk=tk_sb)
        o_mla = _softmax_attn_call(mqt, mk, mvt, fox=False, tq=tq, tk=tk_sm, name="mla_attn")
        n = bsz * s
        h = _out_call(o_fox.reshape(n, -1), o_sb.reshape(n, -1), o_mla.reshape(n, -1), o_mem.reshape(n, -1),
                      gate.reshape(n, -1), h.reshape(n, d), w_out[l].astype(jnp.bfloat16),
                      ln_g[l], ln_b[l], alpha, tm).reshape(bsz, s, d)
    return h
```

```python
import functools

import jax
import jax.numpy as jnp
from jax import lax
from jax.experimental import pallas as pl
from jax.experimental.pallas import tpu as pltpu

N_HEADS = 4
HEAD_DIM = 64
GROUP_W = N_HEADS * HEAD_DIM
MLA_NOPE = 64
MLA_ROPE = 32
ROPE_THETA = 10000.0
LN_EPS = 1e-5
RMS_EPS = 1e-6
LOG2E = 1.4426950408889634

LANES = 128
VMEM_LIMIT_BYTES = 56 * 1024 * 1024

C_FK = 0
C_SQ, C_SK, C_SV = 256, 512, 768
C_CQ = 1024
C_CKV = 1280
C_KRP = 1408
C_KRRP = 1536
C_MQ = 1664
C_GATE = 1920
C_MISC = 2944
W_COLS = 3072
R_FQ, R_FV = 0, 256
WT_ROWS = 512
FORGET_DUP = 6
FORGET_STRIDE = 8
VT_ROWS = HEAD_DIM + 16
NT_DIMS = (((1,), (1,)), ((), ()))


def _cparams(n_grid):
    return pltpu.CompilerParams(dimension_semantics=("arbitrary",) * n_grid,
                                vmem_limit_bytes=VMEM_LIMIT_BYTES)


def _split3(x):
    p0 = x.astype(jnp.bfloat16).astype(jnp.float32)
    r1 = x - p0
    p1 = r1.astype(jnp.bfloat16).astype(jnp.float32)
    p2 = (r1 - p1).astype(jnp.bfloat16).astype(jnp.float32)
    return p0, p1, p2


def _layer_norm_rows(x, g, b):
    mu = jnp.mean(x, axis=-1, keepdims=True)
    xc = x - mu
    var = jnp.mean(xc * xc, axis=-1, keepdims=True)
    return xc * lax.rsqrt(var + LN_EPS) * g + b


def _rms_norm_rows(x, g):
    ms = jnp.mean(x * x, axis=-1, keepdims=True)
    return x * lax.rsqrt(ms + RMS_EPS) * g


def _mem_kv_kernel(mem_ref, g_ref, b_ref, w_ref, o_ref):
    mem_n = _layer_norm_rows(mem_ref[0], g_ref[...], b_ref[...]).astype(jnp.bfloat16)
    for l in range(w_ref.shape[0]):
        o_ref[l, 0] = jnp.dot(mem_n, w_ref[l], preferred_element_type=jnp.float32).astype(jnp.bfloat16)


def _mem_kv_call(mem, g, b, w_mem_kv_bf16):
    bsz, m, d = mem.shape
    depth, _, c = w_mem_kv_bf16.shape
    return pl.pallas_call(
        _mem_kv_kernel,
        grid=(bsz,),
        in_specs=[pl.BlockSpec((1, m, d), lambda i: (i, 0, 0)),
                  pl.BlockSpec((1, d), lambda i: (0, 0)),
                  pl.BlockSpec((1, d), lambda i: (0, 0)),
                  pl.BlockSpec((depth, d, c), lambda i: (0, 0, 0))],
        out_specs=pl.BlockSpec((depth, 1, m, c), lambda i: (0, i, 0, 0)),
        out_shape=jax.ShapeDtypeStruct((depth, bsz, m, c), jnp.bfloat16),
        compiler_params=_cparams(1),
        name="mem_kv",
    )(mem, g.reshape(1, d), b.reshape(1, d), w_mem_kv_bf16)


def _proj_kernel(h_ref, lng_ref, lnb_ref, w_ref, wt_ref, bfor_ref, gq_ref, wqupt_ref, gkv_ref, wkup_ref,
                 wvupt_ref, cqt_ref, sqt_ref, ck_ref, sk_ref, mkv_ref, *rest, pre_ln):
    hout_ref = rest[0] if pre_ln else None
    (fqt_ref, fk_ref, fvt_ref, sbq_ref, sbk_ref, sbv_ref,
     mqt_ref, mk_ref, mvt_ref, omem_ref, gate_ref, carry_ref) = rest[1:] if pre_ln else rest
    i = pl.program_id(1)
    tm = h_ref.shape[1]
    bf16, f32 = jnp.bfloat16, jnp.float32
    head_c = (HEAD_DIM ** -0.5) * LOG2E

    @pl.when(i == 0)
    def _():
        carry_ref[...] = jnp.zeros_like(carry_ref)

    def sub_block(r0, nr):
        rs = slice(r0, r0 + nr)
        h = h_ref[0, rs]
        if pre_ln:
            h = _layer_norm_rows(h, lng_ref[...], lnb_ref[...])
            hout_ref[0, rs] = h
        hb = h.astype(bf16)
        proj = jnp.dot(hb, w_ref[...], preferred_element_type=f32)
        proj_t = lax.dot_general(wt_ref[...], hb, NT_DIMS, preferred_element_type=f32)

        def cols(c0, n):
            return proj[:, c0:c0 + n]

        def store_values_t(vt, out_ref):
            for hh in range(N_HEADS):
                out_ref[0, hh, :HEAD_DIM, rs] = vt[HEAD_DIM * hh:HEAD_DIM * (hh + 1), :].astype(bf16)
                out_ref[0, hh, HEAD_DIM:, rs] = jnp.ones((VT_ROWS - HEAD_DIM, nr), bf16)

        lane = lax.broadcasted_iota(jnp.int32, (nr, LANES), 1)
        sub = lane % FORGET_STRIDE
        used = (lane < N_HEADS * FORGET_STRIDE) & (sub < FORGET_DUP)
        xf = cols(C_MISC, LANES) + bfor_ref[...]
        log_f = jnp.minimum(xf, 0.0) - jnp.log1p(jnp.exp(-jnp.abs(xf)))
        log_f = jnp.where(used, log_f, 0.0)
        row = lax.broadcasted_iota(jnp.int32, (nr, nr), 0)
        col = lax.broadcasted_iota(jnp.int32, (nr, nr), 1)
        tril = jnp.where(col <= row, 1.0, 0.0).astype(bf16)
        parts = jnp.concatenate([p.astype(bf16) for p in _split3(log_f)], axis=1)
        csum = jnp.dot(tril, parts, preferred_element_type=f32)
        f_cum = (csum[:, :LANES] + csum[:, LANES:2 * LANES]) + csum[:, 2 * LANES:] + carry_ref[...]
        carry_ref[...] = f_cum[nr - 1:nr, :]
        p0, p1, p2 = _split3(f_cum * LOG2E)
        bias_q = jnp.where(sub == 0, p0, jnp.where(sub == 1, p1, jnp.where(sub == 2, p2, 1.0)))
        bias_q_t = jnp.where(used, bias_q, 0.0).T.astype(bf16)
        bias_k = jnp.where(sub == 3, -p0, jnp.where(sub == 4, -p1, jnp.where(sub == 5, -p2, 1.0)))
        bias_k = jnp.where(used, bias_k, 0.0).astype(bf16)

        fk = cols(C_FK, GROUP_W)
        for p in range(2):
            fqt_ref[0, p, :LANES, rs] = (proj_t[R_FQ + LANES * p:R_FQ + LANES * (p + 1), :] * head_c).astype(bf16)
            fqt_ref[0, p, LANES:, rs] = bias_q_t
            fk_ref[0, rs, 2 * LANES * p:2 * LANES * p + LANES] = fk[:, LANES * p:LANES * (p + 1)].astype(bf16)
            fk_ref[0, rs, 2 * LANES * p + LANES:2 * LANES * (p + 1)] = bias_k
        store_values_t(proj_t[R_FV:R_FV + GROUP_W, :], fvt_ref)

        sbq_ref[0, rs] = (cols(C_SQ, GROUP_W) * head_c).astype(bf16)
        sbk_ref[0, rs] = cols(C_SK, GROUP_W).astype(bf16)
        sbv_ref[0, rs] = cols(C_SV, GROUP_W).astype(bf16)

        cqn = _rms_norm_rows(cols(C_CQ, 256), gq_ref[...]).astype(bf16)
        q_up_t = lax.dot_general(wqupt_ref[...], cqn, NT_DIMS, preferred_element_type=f32)
        ckvn = _rms_norm_rows(cols(C_CKV, 128), gkv_ref[...]).astype(bf16)
        k_up = jnp.dot(ckvn, wkup_ref[...], preferred_element_type=f32)
        k_rope = cols(C_KRP, LANES) * ck_ref[rs] + cols(C_KRRP, LANES) * sk_ref[rs]
        nq = N_HEADS * LANES
        for hh in range(N_HEADS):
            sl = slice(LANES * hh, LANES * (hh + 1))
            mqt_ref[0, hh, :, rs] = (q_up_t[sl, :] * cqt_ref[:, rs]
                                     + q_up_t[nq + LANES * hh:nq + LANES * (hh + 1), :] * sqt_ref[:, rs]).astype(bf16)
            mk_ref[0, rs, sl] = (k_up[:, sl] + k_rope).astype(bf16)
        store_values_t(lax.dot_general(wvupt_ref[...], ckvn, NT_DIMS, preferred_element_type=f32), mvt_ref)

        mem_q = cols(C_MQ, GROUP_W) * head_c
        lane_h = lax.broadcasted_iota(jnp.int32, (nr, LANES), 1)
        for p in range(2):
            qp = mem_q[:, LANES * p:LANES * (p + 1)]
            kp = mkv_ref[0, :, LANES * p:LANES * (p + 1)]
            vp = mkv_ref[0, :, GROUP_W + LANES * p:GROUP_W + LANES * (p + 1)]
            outs = []
            for hh in range(2):
                in_head = (lane_h >= HEAD_DIM * hh) & (lane_h < HEAD_DIM * (hh + 1))
                qh = jnp.where(in_head, qp, 0.0).astype(bf16)
                s = lax.dot_general(qh, kp, NT_DIMS, preferred_element_type=f32)
                m = jnp.max(s, axis=-1, keepdims=True)
                e = jnp.exp2(s - m)
                pr = e / jnp.sum(e, axis=-1, keepdims=True)
                outs.append(jnp.dot(pr.astype(bf16), vp, preferred_element_type=f32))
            omem_ref[0, rs, LANES * p:LANES * (p + 1)] = jnp.where(lane_h < HEAD_DIM, outs[0], outs[1]).astype(bf16)

        for c in range(4):
            g = cols(C_GATE + GROUP_W * c, GROUP_W)
            gate_ref[0, rs, GROUP_W * c:GROUP_W * (c + 1)] = (g / (1.0 + jnp.exp(-g))).astype(bf16)

    n_sub = 2 if tm % (2 * LANES) == 0 else 1
    for j in range(n_sub):
        sub_block(j * (tm // n_sub), tm // n_sub)


def _proj_call(h, ln_gb, w_packed, w_t, bfor_row, gq, wqupt, gkv, wkup, wvupt, tabs, mkv_l, tm, pre_ln):
    bsz, s, d = h.shape
    ln_g, ln_b = ln_gb
    m = mkv_l.shape[1]
    cqt, sqt, ck, sk = tabs
    bf16 = jnp.bfloat16
    full2 = lambda shape: pl.BlockSpec(shape, lambda b, i: (0, 0))
    row_blk = lambda c: pl.BlockSpec((1, tm, c), lambda b, i: (b, i, 0))
    tab_blk = pl.BlockSpec((tm, LANES), lambda b, i: (i, 0))
    tab_t_blk = pl.BlockSpec((LANES, tm), lambda b, i: (0, i))
    rows = lambda c, dt: (row_blk(c), jax.ShapeDtypeStruct((bsz, s, c), dt))
    feat = lambda n, r: (pl.BlockSpec((1, n, r, tm), lambda b, i: (b, 0, 0, i)),
                         jax.ShapeDtypeStruct((bsz, n, r, s), bf16))
    outs = [feat(2, 2 * LANES), rows(2 * GROUP_W, bf16), feat(N_HEADS, VT_ROWS),
            rows(GROUP_W, bf16), rows(GROUP_W, bf16), rows(GROUP_W, bf16),
            feat(N_HEADS, LANES), rows(2 * GROUP_W, bf16), feat(N_HEADS, VT_ROWS),
            rows(GROUP_W, bf16), rows(4 * GROUP_W, bf16)]
    if pre_ln:
        outs = [rows(d, jnp.float32)] + outs
    return pl.pallas_call(
        functools.partial(_proj_kernel, pre_ln=pre_ln),
        grid=(bsz, s // tm),
        in_specs=[row_blk(d), full2((1, d)), full2((1, d)),
                  full2(w_packed.shape), full2(w_t.shape),
                  full2((1, LANES)),
                  full2((1, 256)), full2(wqupt.shape),
                  full2((1, 128)), full2(wkup.shape), full2(wvupt.shape),
                  tab_t_blk, tab_t_blk, tab_blk, tab_blk,
                  pl.BlockSpec((1, m, 2 * GROUP_W), lambda b, i: (b, 0, 0))],
        out_specs=[spec for spec, _ in outs],
        out_shape=[shape for _, shape in outs],
        scratch_shapes=[pltpu.VMEM((1, LANES), jnp.float32)],
        compiler_params=_cparams(2),
        name="proj",
    )(h, ln_g.reshape(1, d), ln_b.reshape(1, d), w_packed, w_t, bfor_row, gq, wqupt, gkv, wkup, wvupt,
      cqt, sqt, ck, sk, mkv_l)


def _softmax_attn_kernel(qt_ref, k_ref, vt_ref, o_ref, q_sc, s_sc, m_sc, acc_sc, *, fox, tq, tk):
    pair = pl.program_id(1)
    qi = pl.program_id(2)
    bf16, f32 = jnp.bfloat16, jnp.float32
    kq = q_sc.shape[1]
    for hh in range(2):
        if fox:
            feat = lax.broadcasted_iota(jnp.int32, (kq, tq), 0)
            lo = LANES + FORGET_STRIDE * (2 * pair + hh)
            keep = ((feat >= HEAD_DIM * hh) & (feat < HEAD_DIM * (hh + 1))) | (
                (feat >= lo) & (feat < lo + FORGET_STRIDE))
            q_sc[hh] = jnp.where(keep, qt_ref[0, 0], jnp.zeros((kq, tq), bf16))
        else:
            q_sc[hh] = qt_ref[0, hh]
    m_sc[...] = jnp.full(m_sc.shape, -jnp.inf, f32)
    acc_sc[...] = jnp.zeros(acc_sc.shape, f32)

    def scores(kb, slot, c0=0):
        k0 = pl.multiple_of(kb * tk, tk)
        for hh in range(2):
            k = k_ref[0, pl.ds(k0, tk), :] if fox else k_ref[0, pl.ds(k0, tk), LANES * hh:LANES * (hh + 1)]
            s_sc[slot, hh, :, c0:] = jnp.dot(k, q_sc[hh, :, c0:], preferred_element_type=f32)

    def update(kb, slot, c0=0, nc=tq, triangle=False):
        k0 = pl.multiple_of(kb * tk, tk)
        qs = slice(c0, c0 + nc)
        if triangle:
            causal = (lax.broadcasted_iota(jnp.int32, (tk, nc), 0)
                      <= lax.broadcasted_iota(jnp.int32, (tk, nc), 1))
        for hh in range(2):
            s = s_sc[slot, hh, :, qs]
            if triangle:
                s = jnp.where(causal, s, -jnp.inf)
            m_prev = m_sc[hh, :, qs]
            m_new = jnp.maximum(m_prev, jnp.max(s, axis=0, keepdims=True))
            alpha = jnp.exp2(m_prev - m_new)
            p = jnp.exp2(s - m_new)
            acc_sc[hh, :, qs] = (alpha * acc_sc[hh, :, qs]
                                 + jnp.dot(vt_ref[0, hh, :, pl.ds(k0, tk)], p.astype(bf16),
                                           preferred_element_type=f32))
            m_sc[hh, :, qs] = m_new

    n_diag = tq // tk
    unroll = 4 if n_diag % 4 == 0 else 2
    assert n_diag % unroll == 0
    n_full = qi * n_diag
    scores(0, 0)

    def body(j, c):
        for u in range(unroll):
            kb = unroll * j + u
            scores(kb + 1, (u + 1) % 2)
            update(kb, u % 2)
        return c

    lax.fori_loop(0, n_full // unroll, body, 0)
    for d in range(n_diag):
        if d + 1 < n_diag:
            scores(n_full + d + 1, (d + 1) % 2, c0=(d + 1) * tk)
        update(n_full + d, d % 2, c0=d * tk, nc=tk, triangle=True)
        if d + 1 < n_diag:
            update(n_full + d, d % 2, c0=(d + 1) * tk, nc=tq - (d + 1) * tk)
    out_t = jnp.concatenate([acc_sc[hh, :HEAD_DIM] / acc_sc[hh, HEAD_DIM:HEAD_DIM + 1] for hh in range(2)],
                            axis=0)
    o_ref[0] = out_t.T.astype(o_ref.dtype)


def _softmax_attn_call(qt, k, vt, *, fox, tq, tk, name):
    bsz, nqt, kq, s = qt.shape
    kw = 2 * LANES
    kern = functools.partial(_softmax_attn_kernel, fox=fox, tq=tq, tk=tk)
    return pl.pallas_call(
        kern,
        grid=(bsz, 2, s // tq),
        in_specs=[pl.BlockSpec((1, nqt // 2, kq, tq), lambda b, p, i: (b, p, 0, i)),
                  pl.BlockSpec((1, s, kw), lambda b, p, i: (b, 0, p)),
                  pl.BlockSpec((1, 2, VT_ROWS, s), lambda b, p, i: (b, p, 0, 0))],
        out_specs=pl.BlockSpec((1, tq, LANES), lambda b, p, i: (b, i, p)),
        out_shape=jax.ShapeDtypeStruct((bsz, s, GROUP_W), jnp.bfloat16),
        scratch_shapes=[pltpu.VMEM((2, kq, tq), jnp.bfloat16),
                        pltpu.VMEM((2, 2, tk, tq), jnp.float32),
                        pltpu.VMEM((2, 1, tq), jnp.float32),
                        pltpu.VMEM((2, VT_ROWS, tq), jnp.float32)],
        compiler_params=_cparams(3),
        name=name,
    )(qt, k, vt)


def _sb_attn_kernel(q_ref, k_ref, v_ref, o_ref, q_sc, z_sc, c_sc, acc_sc, *, tq, tk):
    qi = pl.program_id(2)
    bf16, f32 = jnp.bfloat16, jnp.float32
    lane_q = lax.broadcasted_iota(jnp.int32, (tq, LANES), 1)
    rj = lax.broadcasted_iota(jnp.int32, (tk, tk), 0)
    cs = lax.broadcasted_iota(jnp.int32, (tk, tk), 1)
    upper = jnp.where(rj >= cs, 1.0, 0.0).astype(bf16)
    for hh in range(2):
        in_head = (lane_q >= HEAD_DIM * hh) & (lane_q < HEAD_DIM * (hh + 1))
        q_sc[hh] = jnp.where(in_head, q_ref[0], jnp.zeros((tq, LANES), bf16))
    c_sc[...] = jnp.zeros(c_sc.shape, f32)
    acc_sc[...] = jnp.zeros(acc_sc.shape, f32)

    def scores(kb, slot, r0=0):
        k0 = pl.multiple_of(kb * tk, tk)
        k = k_ref[0, pl.ds(k0, tk), :]
        for hh in range(2):
            z_sc[slot, hh, r0:] = lax.dot_general(q_sc[hh, r0:], k, NT_DIMS, preferred_element_type=f32)

    def update(kb, slot, r0=0, nr=tq, triangle=False):
        k0 = pl.multiple_of(kb * tk, tk)
        v = v_ref[0, pl.ds(k0, tk), :]
        rows = slice(r0, r0 + nr)
        if triangle:
            valid = (lax.broadcasted_iota(jnp.int32, (nr, tk), 1)
                     < lax.broadcasted_iota(jnp.int32, (nr, tk), 0))
        for hh in range(2):
            z = z_sc[slot, hh, rows]
            sp = jnp.maximum(z, 0.0) + jnp.log2(1.0 + jnp.exp2(-jnp.abs(z)))
            if triangle:
                sp = jnp.where(valid, sp, 0.0)
            csum = (jnp.dot(sp.astype(bf16), upper, preferred_element_type=f32)
                    + jnp.tile(c_sc[hh, rows], (1, tk // LANES)))
            w = jnp.exp2(z - csum)
            if triangle:
                w = jnp.where(valid, w, 0.0)
            acc_sc[hh, rows] += jnp.dot(w.astype(bf16), v, preferred_element_type=f32)
            c_sc[hh, rows] = jnp.broadcast_to(csum[:, 0:1], (nr, LANES))

    n_diag = tq // tk
    unroll = 4 if n_diag % 4 == 0 else 2
    assert n_diag % unroll == 0
    n_full = qi * n_diag
    scores(n_full + n_diag - 1, 1, r0=(n_diag - 1) * tk)
    for d in reversed(range(n_diag)):
        kb = n_full + d
        scores(jnp.maximum(kb - 1, 0), (d + 1) % 2, r0=max(d - 1, 0) * tk)
        update(kb, d % 2, r0=d * tk, nr=tk, triangle=True)
        if d + 1 < n_diag:
            update(kb, d % 2, r0=(d + 1) * tk, nr=tq - (d + 1) * tk)

    def body(n, c):
        top = n_full - 1 - unroll * n
        for u in range(unroll):
            kb = top - u
            scores(jnp.maximum(kb - 1, 0), u % 2)
            update(kb, (u + 1) % 2)
        return c

    lax.fori_loop(0, n_full // unroll, body, 0)
    o_ref[0] = jnp.where(lane_q < HEAD_DIM, acc_sc[0], acc_sc[1]).astype(o_ref.dtype)


def _sb_attn_call(q, k, v, *, tq, tk):
    bsz, s, _ = q.shape
    kern = functools.partial(_sb_attn_kernel, tq=tq, tk=tk)
    return pl.pallas_call(
        kern,
        grid=(bsz, 2, s // tq),
        in_specs=[pl.BlockSpec((1, tq, LANES), lambda b, p, i: (b, i, p)),
                  pl.BlockSpec((1, s, LANES), lambda b, p, i: (b, 0, p)),
                  pl.BlockSpec((1, s, LANES), lambda b, p, i: (b, 0, p))],
        out_specs=pl.BlockSpec((1, tq, LANES), lambda b, p, i: (b, i, p)),
        out_shape=jax.ShapeDtypeStruct((bsz, s, GROUP_W), jnp.bfloat16),
        scratch_shapes=[pltpu.VMEM((2, tq, LANES), jnp.bfloat16),
                        pltpu.VMEM((2, 2, tq, tk), jnp.float32),
                        pltpu.VMEM((2, tq, LANES), jnp.float32),
                        pltpu.VMEM((2, tq, LANES), jnp.float32)],
        compiler_params=_cparams(3),
        name="sb_attn",
    )(q, k, v)


def _out_kernel(of_ref, os_ref, om_ref, ox_ref, gate_ref, h_ref, w_ref, g_ref, b_ref, o_ref, *, alpha):
    bf16, f32 = jnp.bfloat16, jnp.float32
    mixed = jnp.concatenate([of_ref[...], os_ref[...], om_ref[...], ox_ref[...]], axis=1)
    y = jnp.dot(mixed * gate_ref[...], w_ref[...], preferred_element_type=f32)
    o_ref[...] = _layer_norm_rows(alpha * h_ref[...] + y, g_ref[...], b_ref[...])


def _out_call(o_fox, o_sb, o_mla, o_mem, gate, h2d, w_out_bf16, g, b, alpha, tm):
    n, d = h2d.shape
    gw = o_fox.shape[1]
    kern = functools.partial(_out_kernel, alpha=alpha)
    blk = lambda c: pl.BlockSpec((tm, c), lambda i: (i, 0))
    return pl.pallas_call(
        kern,
        grid=(n // tm,),
        in_specs=[blk(gw), blk(gw), blk(gw), blk(gw), blk(4 * gw), blk(d),
                  pl.BlockSpec(w_out_bf16.shape, lambda i: (0, 0)),
                  pl.BlockSpec((1, d), lambda i: (0, 0)),
                  pl.BlockSpec((1, d), lambda i: (0, 0))],
        out_specs=blk(d),
        out_shape=jax.ShapeDtypeStruct((n, d), jnp.float32),
        compiler_params=_cparams(1),
        name="out_proj",
    )(o_fox, o_sb, o_mla, o_mem, gate, h2d, w_out_bf16, g.reshape(1, d), b.reshape(1, d))


def _pack_w_in(w):
    d = w.shape[0]
    o = 0
    cols = {}
    for name, width in (("fq", 256), ("fk", 256), ("fv", 256), ("fl", 4), ("sq", 256), ("sk", 256),
                        ("sv", 256), ("cq", 256), ("ckv", 128), ("kr", 32), ("mq", 256), ("gate", 1024)):
        cols[name] = w[:, o:o + width]
        o += width
    z = lambda n: jnp.zeros((d, n), w.dtype)
    half = MLA_ROPE // 2
    kr = cols["kr"]
    kr_rot = jnp.concatenate([-kr[:, half:], kr[:, :half]], axis=1)
    krp = jnp.concatenate([z(MLA_NOPE), kr, z(LANES - MLA_NOPE - MLA_ROPE)], axis=1)
    krrp = jnp.concatenate([z(MLA_NOPE), kr_rot, z(LANES - MLA_NOPE - MLA_ROPE)], axis=1)
    misc = []
    for hh in range(N_HEADS):
        misc += [cols["fl"][:, hh:hh + 1]] * FORGET_DUP + [z(FORGET_STRIDE - FORGET_DUP)]
    misc.append(z(LANES - N_HEADS * FORGET_STRIDE))
    packed = jnp.concatenate([cols["fk"], cols["sq"], cols["sk"], cols["sv"], cols["cq"], cols["ckv"],
                              krp, krrp, cols["mq"], cols["gate"]] + misc, axis=1)
    assert packed.shape[1] == W_COLS
    w_t = jnp.concatenate([cols["fq"], cols["fv"]], axis=1).T
    assert w_t.shape[0] == WT_ROWS
    return packed.astype(jnp.bfloat16), w_t.astype(jnp.bfloat16)


def _pack_forget_bias(b_forget_l):
    row = jnp.zeros((LANES,), jnp.float32)
    for hh in range(N_HEADS):
        row = row.at[FORGET_STRIDE * hh:FORGET_STRIDE * hh + FORGET_DUP].set(b_forget_l[hh])
    return row.reshape(1, LANES)


def _pack_mla_q_up_t(w):
    r = w.shape[0]
    per = MLA_NOPE + MLA_ROPE
    half = MLA_ROPE // 2
    z = lambda n: jnp.zeros((r, n), w.dtype)
    plain, rot = [], []
    for hh in range(N_HEADS):
        nope = w[:, per * hh:per * hh + MLA_NOPE]
        rope = w[:, per * hh + MLA_NOPE:per * (hh + 1)]
        rope_rot = jnp.concatenate([-rope[:, half:], rope[:, :half]], axis=1)
        plain += [nope, rope, z(LANES - per)]
        rot += [z(MLA_NOPE), rope_rot, z(LANES - per)]
    return jnp.concatenate(plain + rot, axis=1).T.astype(jnp.bfloat16)


def _pack_mla_kv_up(w):
    r = w.shape[0]
    per = MLA_NOPE + HEAD_DIM
    z = jnp.zeros((r, LANES - MLA_NOPE), w.dtype)
    ks, vs = [], []
    for hh in range(N_HEADS):
        ks += [w[:, per * hh:per * hh + MLA_NOPE], z]
        vs.append(w[:, per * hh + MLA_NOPE:per * (hh + 1)])
    return (jnp.concatenate(ks, axis=1).astype(jnp.bfloat16),
            jnp.concatenate(vs, axis=1).T.astype(jnp.bfloat16))


def _rope_tables(s, q_scale):
    half = MLA_ROPE // 2
    inv_freq = ROPE_THETA ** (-jnp.arange(half, dtype=jnp.float32) / half)
    ang = jnp.arange(s).astype(jnp.float32)[:, None] * inv_freq[None, :]
    cos, sin = jnp.cos(ang), jnp.sin(ang)
    ones = jnp.ones((s, MLA_NOPE), jnp.float32)
    z_nope = jnp.zeros((s, MLA_NOPE), jnp.float32)
    z_pad = jnp.zeros((s, LANES - MLA_NOPE - MLA_ROPE), jnp.float32)
    cos_q = jnp.concatenate([ones, cos, cos, z_pad], axis=1) * q_scale
    sin_q = jnp.concatenate([z_nope, sin, sin, z_pad], axis=1) * q_scale
    cos_k = jnp.concatenate([z_nope, cos, cos, z_pad], axis=1)
    sin_k = jnp.concatenate([z_nope, sin, sin, z_pad], axis=1)
    return cos_q.T, sin_q.T, cos_k, sin_k


def kernel(x, mem, ln_in_g, ln_in_b, mem_ln_g, mem_ln_b, w_in, b_forget, mla_q_norm_g, w_mla_q_up,
           mla_kv_norm_g, w_mla_kv_up, w_mem_kv, w_out, ln_g, ln_b):
    bsz, s, d = x.shape
    depth = w_in.shape[0]
    alpha = (2 * depth) ** 0.25
    tm = min(512, s)
    tq_sm = min(2048, s)
    tq_sb = min(1024, s)
    tk_sm = min(512, tq_sm // 2)
    tk_sb = min(256, tq_sb // 2)
    mla_scale = (MLA_NOPE + MLA_ROPE) ** -0.5
    tabs = _rope_tables(s, mla_scale * LOG2E)

    mkv = _mem_kv_call(mem, mem_ln_g, mem_ln_b, w_mem_kv.astype(jnp.bfloat16))

    h = x
    for l in range(depth):
        w_packed, w_t = _pack_w_in(w_in[l])
        w_k_up, w_v_up_t = _pack_mla_kv_up(w_mla_kv_up[l])
        outs = _proj_call(
            h, (ln_in_g, ln_in_b), w_packed, w_t, _pack_forget_bias(b_forget[l]),
            mla_q_norm_g[l].reshape(1, -1), _pack_mla_q_up_t(w_mla_q_up[l]),
            mla_kv_norm_g[l].reshape(1, -1), w_k_up, w_v_up_t,
            tabs, mkv[l], tm, pre_ln=(l == 0))
        if l == 0:
            h, outs = outs[0], outs[1:]
        (fqt, fk, fvt, sbq, sbk, sbv, mqt, mk, mvt, o_mem, gate) = outs
        o_fox = _softmax_attn_call(fqt, fk, fvt, fox=True, tq=tq_sm, tk=tk_sm, name="fox_attn")
        o_sb = _sb_attn_call(sbq, sbk, sbv, tq=tq_sb, tk=tk_sb)
        o_mla = _softmax_attn_call(mqt, mk, mvt, fox=False, tq=tq_sm, tk=tk_sm, name="mla_attn")
        n = bsz * s
        h = _out_call(o_fox.reshape(n, -1), o_sb.reshape(n, -1), o_mla.reshape(n, -1), o_mem.reshape(n, -1),
                      gate.reshape(n, -1), h.reshape(n, d), w_out[l].astype(jnp.bfloat16),
                      ln_g[l], ln_b[l], alpha, tm).reshape(bsz, s, d)
    return h
```

```python
import functools

import jax
import jax.numpy as jnp
from jax import lax
from jax.experimental import pallas as pl
from jax.experimental.pallas import tpu as pltpu

N_HEADS = 4
HEAD_DIM = 64
GROUP_W = N_HEADS * HEAD_DIM
MLA_NOPE = 64
MLA_ROPE = 32
ROPE_THETA = 10000.0
LN_EPS = 1e-5
RMS_EPS = 1e-6
LOG2E = 1.4426950408889634
EXP2_CLAMP = 126.0

LANES = 128
VMEM_LIMIT_BYTES = 56 * 1024 * 1024

C_FK = 0
C_SQ, C_SK, C_SV = 256, 512, 768
C_CQ = 1024
C_CKV = 1280
C_KRP = 1408
C_KRRP = 1536
C_MQ = 1664
C_GATE = 1920
C_MISC = 2944
W_COLS = 3072
R_FQ, R_FV = 0, 256
WT_ROWS = 512
FORGET_DUP = 6
FORGET_STRIDE = 8
VT_ROWS = HEAD_DIM + 16
NT_DIMS = (((1,), (1,)), ((), ()))


def _cparams(n_grid):
    return pltpu.CompilerParams(dimension_semantics=("arbitrary",) * n_grid,
                                vmem_limit_bytes=VMEM_LIMIT_BYTES)


def _split3(x):
    p0 = x.astype(jnp.bfloat16).astype(jnp.float32)
    r1 = x - p0
    p1 = r1.astype(jnp.bfloat16).astype(jnp.float32)
    p2 = (r1 - p1).astype(jnp.bfloat16).astype(jnp.float32)
    return p0, p1, p2


def _layer_norm_rows(x, g, b):
    mu = jnp.mean(x, axis=-1, keepdims=True)
    xc = x - mu
    var = jnp.mean(xc * xc, axis=-1, keepdims=True)
    return xc * lax.rsqrt(var + LN_EPS) * g + b


def _rms_norm_rows(x, g):
    ms = jnp.mean(x * x, axis=-1, keepdims=True)
    return x * lax.rsqrt(ms + RMS_EPS) * g


def _mem_kv_kernel(mem_ref, g_ref, b_ref, w_ref, o_ref):
    mem_n = _layer_norm_rows(mem_ref[0], g_ref[...], b_ref[...]).astype(jnp.bfloat16)
    for l in range(w_ref.shape[0]):
        o_ref[l, 0] = jnp.dot(mem_n, w_ref[l], preferred_element_type=jnp.float32).astype(jnp.bfloat16)


def _mem_kv_call(mem, g, b, w_mem_kv_bf16):
    bsz, m, d = mem.shape
    depth, _, c = w_mem_kv_bf16.shape
    return pl.pallas_call(
        _mem_kv_kernel,
        grid=(bsz,),
        in_specs=[pl.BlockSpec((1, m, d), lambda i: (i, 0, 0)),
                  pl.BlockSpec((1, d), lambda i: (0, 0)),
                  pl.BlockSpec((1, d), lambda i: (0, 0)),
                  pl.BlockSpec((depth, d, c), lambda i: (0, 0, 0))],
        out_specs=pl.BlockSpec((depth, 1, m, c), lambda i: (0, i, 0, 0)),
        out_shape=jax.ShapeDtypeStruct((depth, bsz, m, c), jnp.bfloat16),
        compiler_params=_cparams(1),
        name="mem_kv",
    )(mem, g.reshape(1, d), b.reshape(1, d), w_mem_kv_bf16)


def _proj_kernel(h_ref, lng_ref, lnb_ref, w_ref, wt_ref, bfor_ref, gq_ref, wqupt_ref, gkv_ref, wkup_ref,
                 wvupt_ref, cqt_ref, sqt_ref, ck_ref, sk_ref, mkv_ref, *rest, pre_ln):
    hout_ref = rest[0] if pre_ln else None
    (fqt_ref, fk_ref, fvt_ref, sbq_ref, sbk_ref, sbv_ref,
     mqt_ref, mk_ref, mvt_ref, omem_ref, gate_ref, carry_ref) = rest[1:] if pre_ln else rest
    i = pl.program_id(1)
    tm = h_ref.shape[1]
    bf16, f32 = jnp.bfloat16, jnp.float32
    head_c = (HEAD_DIM ** -0.5) * LOG2E

    @pl.when(i == 0)
    def _():
        carry_ref[...] = jnp.zeros_like(carry_ref)

    def sub_block(r0, nr):
        rs = slice(r0, r0 + nr)
        h = h_ref[0, rs]
        if pre_ln:
            h = _layer_norm_rows(h, lng_ref[...], lnb_ref[...])
            hout_ref[0, rs] = h
        hb = h.astype(bf16)
        proj = jnp.dot(hb, w_ref[...], preferred_element_type=f32)
        proj_t = lax.dot_general(wt_ref[...], hb, NT_DIMS, preferred_element_type=f32)

        def cols(c0, n):
            return proj[:, c0:c0 + n]

        def store_values_t(vt, out_ref):
            for hh in range(N_HEADS):
                out_ref[0, hh, :HEAD_DIM, rs] = vt[HEAD_DIM * hh:HEAD_DIM * (hh + 1), :].astype(bf16)
                out_ref[0, hh, HEAD_DIM:, rs] = jnp.ones((VT_ROWS - HEAD_DIM, nr), bf16)

        lane = lax.broadcasted_iota(jnp.int32, (nr, LANES), 1)
        sub = lane % FORGET_STRIDE
        used = (lane < N_HEADS * FORGET_STRIDE) & (sub < FORGET_DUP)
        xf = cols(C_MISC, LANES) + bfor_ref[...]
        log_f = jnp.minimum(xf, 0.0) - jnp.log1p(jnp.exp(-jnp.abs(xf)))
        log_f = jnp.where(used, log_f, 0.0)
        row = lax.broadcasted_iota(jnp.int32, (nr, nr), 0)
        col = lax.broadcasted_iota(jnp.int32, (nr, nr), 1)
        tril = jnp.where(col <= row, 1.0, 0.0).astype(bf16)
        parts = jnp.concatenate([p.astype(bf16) for p in _split3(log_f)], axis=1)
        csum = jnp.dot(tril, parts, preferred_element_type=f32)
        f_cum = (csum[:, :LANES] + csum[:, LANES:2 * LANES]) + csum[:, 2 * LANES:] + carry_ref[...]
        carry_ref[...] = f_cum[nr - 1:nr, :]
        p0, p1, p2 = _split3(f_cum * LOG2E)
        bias_q = jnp.where(sub == 0, p0, jnp.where(sub == 1, p1, jnp.where(sub == 2, p2, 1.0)))
        bias_q_t = jnp.where(used, bias_q, 0.0).T.astype(bf16)
        bias_k = jnp.where(sub == 3, -p0, jnp.where(sub == 4, -p1, jnp.where(sub == 5, -p2, 1.0)))
        bias_k = jnp.where(used, bias_k, 0.0).astype(bf16)

        fk = cols(C_FK, GROUP_W)
        for p in range(2):
            fqt_ref[0, p, :LANES, rs] = (proj_t[R_FQ + LANES * p:R_FQ + LANES * (p + 1), :] * head_c).astype(bf16)
            fqt_ref[0, p, LANES:, rs] = bias_q_t
            fk_ref[0, rs, 2 * LANES * p:2 * LANES * p + LANES] = fk[:, LANES * p:LANES * (p + 1)].astype(bf16)
            fk_ref[0, rs, 2 * LANES * p + LANES:2 * LANES * (p + 1)] = bias_k
        store_values_t(proj_t[R_FV:R_FV + GROUP_W, :], fvt_ref)

        sbq_ref[0, rs] = (cols(C_SQ, GROUP_W) * head_c).astype(bf16)
        sbk_ref[0, rs] = cols(C_SK, GROUP_W).astype(bf16)
        sbv_ref[0, rs] = cols(C_SV, GROUP_W).astype(bf16)

        cqn = _rms_norm_rows(cols(C_CQ, 256), gq_ref[...]).astype(bf16)
        q_up_t = lax.dot_general(wqupt_ref[...], cqn, NT_DIMS, preferred_element_type=f32)
        ckvn = _rms_norm_rows(cols(C_CKV, 128), gkv_ref[...]).astype(bf16)
        k_up = jnp.dot(ckvn, wkup_ref[...], preferred_element_type=f32)
        k_rope = cols(C_KRP, LANES) * ck_ref[rs] + cols(C_KRRP, LANES) * sk_ref[rs]
        nq = N_HEADS * LANES
        for hh in range(N_HEADS):
            sl = slice(LANES * hh, LANES * (hh + 1))
            mqt_ref[0, hh, :, rs] = (q_up_t[sl, :] * cqt_ref[:, rs]
                                     + q_up_t[nq + LANES * hh:nq + LANES * (hh + 1), :] * sqt_ref[:, rs]).astype(bf16)
            mk_ref[0, rs, sl] = (k_up[:, sl] + k_rope).astype(bf16)
        store_values_t(lax.dot_general(wvupt_ref[...], ckvn, NT_DIMS, preferred_element_type=f32), mvt_ref)

        mem_q = cols(C_MQ, GROUP_W) * head_c
        lane_h = lax.broadcasted_iota(jnp.int32, (nr, LANES), 1)
        for p in range(2):
            qp = mem_q[:, LANES * p:LANES * (p + 1)]
            kp = mkv_ref[0, :, LANES * p:LANES * (p + 1)]
            vp = mkv_ref[0, :, GROUP_W + LANES * p:GROUP_W + LANES * (p + 1)]
            outs = []
            for hh in range(2):
                in_head = (lane_h >= HEAD_DIM * hh) & (lane_h < HEAD_DIM * (hh + 1))
                qh = jnp.where(in_head, qp, 0.0).astype(bf16)
                s = lax.dot_general(qh, kp, NT_DIMS, preferred_element_type=f32)
                m = jnp.max(s, axis=-1, keepdims=True)
                e = jnp.exp2(s - m)
                pr = e / jnp.sum(e, axis=-1, keepdims=True)
                outs.append(jnp.dot(pr.astype(bf16), vp, preferred_element_type=f32))
            omem_ref[0, rs, LANES * p:LANES * (p + 1)] = jnp.where(lane_h < HEAD_DIM, outs[0], outs[1]).astype(bf16)

        for c in range(4):
            g = cols(C_GATE + GROUP_W * c, GROUP_W)
            gate_ref[0, rs, GROUP_W * c:GROUP_W * (c + 1)] = (g / (1.0 + jnp.exp(-g))).astype(bf16)

    n_sub = 2 if tm % (2 * LANES) == 0 else 1
    for j in range(n_sub):
        sub_block(j * (tm // n_sub), tm // n_sub)


def _proj_call(h, ln_gb, w_packed, w_t, bfor_row, gq, wqupt, gkv, wkup, wvupt, tabs, mkv_l, tm, pre_ln):
    bsz, s, d = h.shape
    ln_g, ln_b = ln_gb
    m = mkv_l.shape[1]
    cqt, sqt, ck, sk = tabs
    bf16 = jnp.bfloat16
    full2 = lambda shape: pl.BlockSpec(shape, lambda b, i: (0, 0))
    row_blk = lambda c: pl.BlockSpec((1, tm, c), lambda b, i: (b, i, 0))
    tab_blk = pl.BlockSpec((tm, LANES), lambda b, i: (i, 0))
    tab_t_blk = pl.BlockSpec((LANES, tm), lambda b, i: (0, i))
    rows = lambda c, dt: (row_blk(c), jax.ShapeDtypeStruct((bsz, s, c), dt))
    feat = lambda n, r: (pl.BlockSpec((1, n, r, tm), lambda b, i: (b, 0, 0, i)),
                         jax.ShapeDtypeStruct((bsz, n, r, s), bf16))
    outs = [feat(2, 2 * LANES), rows(2 * GROUP_W, bf16), feat(N_HEADS, VT_ROWS),
            rows(GROUP_W, bf16), rows(GROUP_W, bf16), rows(GROUP_W, bf16),
            feat(N_HEADS, LANES), rows(2 * GROUP_W, bf16), feat(N_HEADS, VT_ROWS),
            rows(GROUP_W, bf16), rows(4 * GROUP_W, bf16)]
    if pre_ln:
        outs = [rows(d, jnp.float32)] + outs
    return pl.pallas_call(
        functools.partial(_proj_kernel, pre_ln=pre_ln),
        grid=(bsz, s // tm),
        in_specs=[row_blk(d), full2((1, d)), full2((1, d)),
                  full2(w_packed.shape), full2(w_t.shape),
                  full2((1, LANES)),
                  full2((1, 256)), full2(wqupt.shape),
                  full2((1, 128)), full2(wkup.shape), full2(wvupt.shape),
                  tab_t_blk, tab_t_blk, tab_blk, tab_blk,
                  pl.BlockSpec((1, m, 2 * GROUP_W), lambda b, i: (b, 0, 0))],
        out_specs=[spec for spec, _ in outs],
        out_shape=[shape for _, shape in outs],
        scratch_shapes=[pltpu.VMEM((1, LANES), jnp.float32)],
        compiler_params=_cparams(2),
        name="proj",
    )(h, ln_g.reshape(1, d), ln_b.reshape(1, d), w_packed, w_t, bfor_row, gq, wqupt, gkv, wkup, wvupt,
      cqt, sqt, ck, sk, mkv_l)


def _softmax_attn_kernel(qt_ref, k_ref, vt_ref, o_ref, q_sc, s_sc, m_sc, acc_sc, *, fox, tq, tk):
    pair = pl.program_id(1)
    qi = pl.program_id(2)
    bf16, f32 = jnp.bfloat16, jnp.float32
    kq = q_sc.shape[1]
    for hh in range(2):
        if fox:
            feat = lax.broadcasted_iota(jnp.int32, (kq, tq), 0)
            lo = LANES + FORGET_STRIDE * (2 * pair + hh)
            keep = ((feat >= HEAD_DIM * hh) & (feat < HEAD_DIM * (hh + 1))) | (
                (feat >= lo) & (feat < lo + FORGET_STRIDE))
            q_sc[hh] = jnp.where(keep, qt_ref[0, 0], jnp.zeros((kq, tq), bf16))
        else:
            q_sc[hh] = qt_ref[0, hh]
    m_sc[...] = jnp.full(m_sc.shape, -jnp.inf, f32)
    acc_sc[...] = jnp.zeros(acc_sc.shape, f32)

    def scores(kb, slot, c0=0):
        k0 = pl.multiple_of(kb * tk, tk)
        for hh in range(2):
            k = k_ref[0, pl.ds(k0, tk), :] if fox else k_ref[0, pl.ds(k0, tk), LANES * hh:LANES * (hh + 1)]
            s_sc[slot, hh, :, c0:] = jnp.dot(k, q_sc[hh, :, c0:], preferred_element_type=f32)

    def update(kb, slot, c0=0, nc=tq, triangle=False):
        k0 = pl.multiple_of(kb * tk, tk)
        qs = slice(c0, c0 + nc)
        if triangle:
            causal = (lax.broadcasted_iota(jnp.int32, (tk, nc), 0)
                      <= lax.broadcasted_iota(jnp.int32, (tk, nc), 1))
        for hh in range(2):
            s = s_sc[slot, hh, :, qs]
            if triangle:
                s = jnp.where(causal, s, -jnp.inf)
            m_prev = m_sc[hh, :, qs]
            m_new = jnp.maximum(m_prev, jnp.max(s, axis=0, keepdims=True))
            alpha = jnp.exp2(m_prev - m_new)
            p = jnp.exp2(s - m_new)
            acc_sc[hh, :, qs] = (alpha * acc_sc[hh, :, qs]
                                 + jnp.dot(vt_ref[0, hh, :, pl.ds(k0, tk)], p.astype(bf16),
                                           preferred_element_type=f32))
            m_sc[hh, :, qs] = m_new

    n_diag = tq // tk
    unroll = 4 if n_diag % 4 == 0 else 2
    assert n_diag % unroll == 0
    n_full = qi * n_diag
    scores(0, 0)

    def body(j, c):
        for u in range(unroll):
            kb = unroll * j + u
            scores(kb + 1, (u + 1) % 2)
            update(kb, u % 2)
        return c

    lax.fori_loop(0, n_full // unroll, body, 0)
    for d in range(n_diag):
        if d + 1 < n_diag:
            scores(n_full + d + 1, (d + 1) % 2, c0=(d + 1) * tk)
        update(n_full + d, d % 2, c0=d * tk, nc=tk, triangle=True)
        if d + 1 < n_diag:
            update(n_full + d, d % 2, c0=(d + 1) * tk, nc=tq - (d + 1) * tk)
    out_t = jnp.concatenate([acc_sc[hh, :HEAD_DIM] / acc_sc[hh, HEAD_DIM:HEAD_DIM + 1] for hh in range(2)],
                            axis=0)
    o_ref[0] = out_t.T.astype(o_ref.dtype)


def _softmax_attn_call(qt, k, vt, *, fox, tq, tk, name):
    bsz, nqt, kq, s = qt.shape
    kw = 2 * LANES
    kern = functools.partial(_softmax_attn_kernel, fox=fox, tq=tq, tk=tk)
    return pl.pallas_call(
        kern,
        grid=(bsz, 2, s // tq),
        in_specs=[pl.BlockSpec((1, nqt // 2, kq, tq), lambda b, p, i: (b, p, 0, i)),
                  pl.BlockSpec((1, s, kw), lambda b, p, i: (b, 0, p)),
                  pl.BlockSpec((1, 2, VT_ROWS, s), lambda b, p, i: (b, p, 0, 0))],
        out_specs=pl.BlockSpec((1, tq, LANES), lambda b, p, i: (b, i, p)),
        out_shape=jax.ShapeDtypeStruct((bsz, s, GROUP_W), jnp.bfloat16),
        scratch_shapes=[pltpu.VMEM((2, kq, tq), jnp.bfloat16),
                        pltpu.VMEM((2, 2, tk, tq), jnp.float32),
                        pltpu.VMEM((2, 1, tq), jnp.float32),
                        pltpu.VMEM((2, VT_ROWS, tq), jnp.float32)],
        compiler_params=_cparams(3),
        name=name,
    )(qt, k, vt)


def _sb_attn_kernel(q_ref, k_ref, v_ref, o_ref, q_sc, z_sc, c_sc, acc_sc, *, tq, tk):
    qi = pl.program_id(2)
    bf16, f32 = jnp.bfloat16, jnp.float32
    lane_q = lax.broadcasted_iota(jnp.int32, (tq, LANES), 1)
    rj = lax.broadcasted_iota(jnp.int32, (tk, tk), 0)
    cs = lax.broadcasted_iota(jnp.int32, (tk, tk), 1)
    upper = jnp.where(rj >= cs, 1.0, 0.0).astype(bf16)
    for hh in range(2):
        in_head = (lane_q >= HEAD_DIM * hh) & (lane_q < HEAD_DIM * (hh + 1))
        q_sc[hh] = jnp.where(in_head, q_ref[0], jnp.zeros((tq, LANES), bf16))
    c_sc[...] = jnp.zeros(c_sc.shape, f32)
    acc_sc[...] = jnp.zeros(acc_sc.shape, f32)

    def scores(kb, slot, r0=0):
        k0 = pl.multiple_of(kb * tk, tk)
        k = k_ref[0, pl.ds(k0, tk), :]
        for hh in range(2):
            z_sc[slot, hh, r0:] = lax.dot_general(q_sc[hh, r0:], k, NT_DIMS, preferred_element_type=f32)

    def update(kb, slot, r0=0, nr=tq, triangle=False):
        k0 = pl.multiple_of(kb * tk, tk)
        v = v_ref[0, pl.ds(k0, tk), :]
        rows = slice(r0, r0 + nr)
        if triangle:
            valid = (lax.broadcasted_iota(jnp.int32, (nr, tk), 1)
                     < lax.broadcasted_iota(jnp.int32, (nr, tk), 0))
        for hh in range(2):
            z = z_sc[slot, hh, rows]
            sp = jnp.maximum(jnp.log2(1.0 + jnp.exp2(jnp.minimum(z, EXP2_CLAMP))), z)
            if triangle:
                sp = jnp.where(valid, sp, 0.0)
            csum = (jnp.dot(sp.astype(bf16), upper, preferred_element_type=f32)
                    + jnp.tile(c_sc[hh, rows], (1, tk // LANES)))
            w = jnp.exp2(z - csum)
            if triangle:
                w = jnp.where(valid, w, 0.0)
            acc_sc[hh, rows] += jnp.dot(w.astype(bf16), v, preferred_element_type=f32)
            c_sc[hh, rows] = jnp.broadcast_to(csum[:, 0:1], (nr, LANES))

    n_diag = tq // tk
    unroll = 4 if n_diag % 4 == 0 else 2
    assert n_diag % unroll == 0
    n_full = qi * n_diag
    scores(n_full + n_diag - 1, 1, r0=(n_diag - 1) * tk)
    for d in reversed(range(n_diag)):
        kb = n_full + d
        scores(jnp.maximum(kb - 1, 0), (d + 1) % 2, r0=max(d - 1, 0) * tk)
        update(kb, d % 2, r0=d * tk, nr=tk, triangle=True)
        if d + 1 < n_diag:
            update(kb, d % 2, r0=(d + 1) * tk, nr=tq - (d + 1) * tk)

    def body(n, c):
        top = n_full - 1 - unroll * n
        for u in range(unroll):
            kb = top - u
            scores(jnp.maximum(kb - 1, 0), u % 2)
            update(kb, (u + 1) % 2)
        return c

    lax.fori_loop(0, n_full // unroll, body, 0)
    o_ref[0] = jnp.where(lane_q < HEAD_DIM, acc_sc[0], acc_sc[1]).astype(o_ref.dtype)


def _sb_attn_call(q, k, v, *, tq, tk):
    bsz, s, _ = q.shape
    kern = functools.partial(_sb_attn_kernel, tq=tq, tk=tk)
    return pl.pallas_call(
        kern,
        grid=(bsz, 2, s // tq),
        in_specs=[pl.BlockSpec((1, tq, LANES), lambda b, p, i: (b, i, p)),
                  pl.BlockSpec((1, s, LANES), lambda b, p, i: (b, 0, p)),
                  pl.BlockSpec((1, s, LANES), lambda b, p, i: (b, 0, p))],
        out_specs=pl.BlockSpec((1, tq, LANES), lambda b, p, i: (b, i, p)),
        out_shape=jax.ShapeDtypeStruct((bsz, s, GROUP_W), jnp.bfloat16),
        scratch_shapes=[pltpu.VMEM((2, tq, LANES), jnp.bfloat16),
                        pltpu.VMEM((2, 2, tq, tk), jnp.float32),
                        pltpu.VMEM((2, tq, LANES), jnp.float32),
                        pltpu.VMEM((2, tq, LANES), jnp.float32)],
        compiler_params=_cparams(3),
        name="sb_attn",
    )(q, k, v)


def _out_kernel(of_ref, os_ref, om_ref, ox_ref, gate_ref, h_ref, w_ref, g_ref, b_ref, o_ref, *, alpha):
    bf16, f32 = jnp.bfloat16, jnp.float32
    mixed = jnp.concatenate([of_ref[...], os_ref[...], om_ref[...], ox_ref[...]], axis=1)
    y = jnp.dot(mixed * gate_ref[...], w_ref[...], preferred_element_type=f32)
    o_ref[...] = _layer_norm_rows(alpha * h_ref[...] + y, g_ref[...], b_ref[...])


def _out_call(o_fox, o_sb, o_mla, o_mem, gate, h2d, w_out_bf16, g, b, alpha, tm):
    n, d = h2d.shape
    gw = o_fox.shape[1]
    kern = functools.partial(_out_kernel, alpha=alpha)
    blk = lambda c: pl.BlockSpec((tm, c), lambda i: (i, 0))
    return pl.pallas_call(
        kern,
        grid=(n // tm,),
        in_specs=[blk(gw), blk(gw), blk(gw), blk(gw), blk(4 * gw), blk(d),
                  pl.BlockSpec(w_out_bf16.shape, lambda i: (0, 0)),
                  pl.BlockSpec((1, d), lambda i: (0, 0)),
                  pl.BlockSpec((1, d), lambda i: (0, 0))],
        out_specs=blk(d),
        out_shape=jax.ShapeDtypeStruct((n, d), jnp.float32),
        compiler_params=_cparams(1),
        name="out_proj",
    )(o_fox, o_sb, o_mla, o_mem, gate, h2d, w_out_bf16, g.reshape(1, d), b.reshape(1, d))


def _pack_w_in(w):
    d = w.shape[0]
    o = 0
    cols = {}
    for name, width in (("fq", 256), ("fk", 256), ("fv", 256), ("fl", 4), ("sq", 256), ("sk", 256),
                        ("sv", 256), ("cq", 256), ("ckv", 128), ("kr", 32), ("mq", 256), ("gate", 1024)):
        cols[name] = w[:, o:o + width]
        o += width
    z = lambda n: jnp.zeros((d, n), w.dtype)
    half = MLA_ROPE // 2
    kr = cols["kr"]
    kr_rot = jnp.concatenate([-kr[:, half:], kr[:, :half]], axis=1)
    krp = jnp.concatenate([z(MLA_NOPE), kr, z(LANES - MLA_NOPE - MLA_ROPE)], axis=1)
    krrp = jnp.concatenate([z(MLA_NOPE), kr_rot, z(LANES - MLA_NOPE - MLA_ROPE)], axis=1)
    misc = []
    for hh in range(N_HEADS):
        misc += [cols["fl"][:, hh:hh + 1]] * FORGET_DUP + [z(FORGET_STRIDE - FORGET_DUP)]
    misc.append(z(LANES - N_HEADS * FORGET_STRIDE))
    packed = jnp.concatenate([cols["fk"], cols["sq"], cols["sk"], cols["sv"], cols["cq"], cols["ckv"],
                              krp, krrp, cols["mq"], cols["gate"]] + misc, axis=1)
    assert packed.shape[1] == W_COLS
    w_t = jnp.concatenate([cols["fq"], cols["fv"]], axis=1).T
    assert w_t.shape[0] == WT_ROWS
    return packed.astype(jnp.bfloat16), w_t.astype(jnp.bfloat16)


def _pack_forget_bias(b_forget_l):
    row = jnp.zeros((LANES,), jnp.float32)
    for hh in range(N_HEADS):
        row = row.at[FORGET_STRIDE * hh:FORGET_STRIDE * hh + FORGET_DUP].set(b_forget_l[hh])
    return row.reshape(1, LANES)


def _pack_mla_q_up_t(w):
    r = w.shape[0]
    per = MLA_NOPE + MLA_ROPE
    half = MLA_ROPE // 2
    z = lambda n: jnp.zeros((r, n), w.dtype)
    plain, rot = [], []
    for hh in range(N_HEADS):
        nope = w[:, per * hh:per * hh + MLA_NOPE]
        rope = w[:, per * hh + MLA_NOPE:per * (hh + 1)]
        rope_rot = jnp.concatenate([-rope[:, half:], rope[:, :half]], axis=1)
        plain += [nope, rope, z(LANES - per)]
        rot += [z(MLA_NOPE), rope_rot, z(LANES - per)]
    return jnp.concatenate(plain + rot, axis=1).T.astype(jnp.bfloat16)


def _pack_mla_kv_up(w):
    r = w.shape[0]
    per = MLA_NOPE + HEAD_DIM
    z = jnp.zeros((r, LANES - MLA_NOPE), w.dtype)
    ks, vs = [], []
    for hh in range(N_HEADS):
        ks += [w[:, per * hh:per * hh + MLA_NOPE], z]
        vs.append(w[:, per * hh + MLA_NOPE:per * (hh + 1)])
    return (jnp.concatenate(ks, axis=1).astype(jnp.bfloat16),
            jnp.concatenate(vs, axis=1).T.astype(jnp.bfloat16))


def _rope_tables(s, q_scale):
    half = MLA_ROPE // 2
    inv_freq = ROPE_THETA ** (-jnp.arange(half, dtype=jnp.float32) / half)
    ang = jnp.arange(s).astype(jnp.float32)[:, None] * inv_freq[None, :]
    cos, sin = jnp.cos(ang), jnp.sin(ang)
    ones = jnp.ones((s, MLA_NOPE), jnp.float32)
    z_nope = jnp.zeros((s, MLA_NOPE), jnp.float32)
    z_pad = jnp.zeros((s, LANES - MLA_NOPE - MLA_ROPE), jnp.float32)
    cos_q = jnp.concatenate([ones, cos, cos, z_pad], axis=1) * q_scale
    sin_q = jnp.concatenate([z_nope, sin, sin, z_pad], axis=1) * q_scale
    cos_k = jnp.concatenate([z_nope, cos, cos, z_pad], axis=1)
    sin_k = jnp.concatenate([z_nope, sin, sin, z_pad], axis=1)
    return cos_q.T, sin_q.T, cos_k, sin_k


def kernel(x, mem, ln_in_g, ln_in_b, mem_ln_g, mem_ln_b, w_in, b_forget, mla_q_norm_g, w_mla_q_up,
           mla_kv_norm_g, w_mla_kv_up, w_mem_kv, w_out, ln_g, ln_b):
    bsz, s, d = x.shape
    depth = w_in.shape[0]
    alpha = (2 * depth) ** 0.25
    tm = min(512, s)
    tq_sm = min(2048, s)
    tq_sb = min(1024, s)
    tk_sm = min(512, tq_sm // 2)
    tk_sb = min(256, tq_sb // 2)
    mla_scale = (MLA_NOPE + MLA_ROPE) ** -0.5
    tabs = _rope_tables(s, mla_scale * LOG2E)

    mkv = _mem_kv_call(mem, mem_ln_g, mem_ln_b, w_mem_kv.astype(jnp.bfloat16))

    h = x
    for l in range(depth):
        w_packed, w_t = _pack_w_in(w_in[l])
        w_k_up, w_v_up_t = _pack_mla_kv_up(w_mla_kv_up[l])
        outs = _proj_call(
            h, (ln_in_g, ln_in_b), w_packed, w_t, _pack_forget_bias(b_forget[l]),
            mla_q_norm_g[l].reshape(1, -1), _pack_mla_q_up_t(w_mla_q_up[l]),
            mla_kv_norm_g[l].reshape(1, -1), w_k_up, w_v_up_t,
            tabs, mkv[l], tm, pre_ln=(l == 0))
        if l == 0:
            h, outs = outs[0], outs[1:]
        (fqt, fk, fvt, sbq, sbk, sbv, mqt, mk, mvt, o_mem, gate) = outs
        o_fox = _softmax_attn_call(fqt, fk, fvt, fox=True, tq=tq_sm, tk=tk_sm, name="fox_attn")
        o_sb = _sb_attn_call(sbq, sbk, sbv, tq=tq_sb, tk=tk_sb)
        o_mla = _softmax_attn_call(mqt, mk, mvt, fox=False, tq=tq_sm, tk=tk_sm, name="mla_attn")
        n = bsz * s
        h = _out_call(o_fox.reshape(n, -1), o_sb.reshape(n, -1), o_mla.reshape(n, -1), o_mem.reshape(n, -1),
                      gate.reshape(n, -1), h.reshape(n, d), w_out[l].astype(jnp.bfloat16),
                      ln_g[l], ln_b[l], alpha, tm).reshape(bsz, s, d)
    return h
```

```python
import functools

import jax
import jax.numpy as jnp
from jax import lax
from jax.experimental import pallas as pl
from jax.experimental.pallas import tpu as pltpu

N_HEADS = 4
HEAD_DIM = 64
GROUP_W = N_HEADS * HEAD_DIM
MLA_NOPE = 64
MLA_ROPE = 32
MLA_Q_RANK = 256
MLA_KV_RANK = 128
ROPE_THETA = 10000.0
LN_EPS = 1e-5
RMS_EPS = 1e-6
LOG2E = 1.4426950408889634
EXP2_CLAMP = 126.0

LANES = 128
VMEM_LIMIT_BYTES = 56 * 1024 * 1024

C_GATE = 0
C_CQ = 1024
C_CKV = 1280
C_MISC = 1408
C_MQ = 1536
C_KRP = 1792
C_KRRP = 1920
C_FK = 2048
C_SQ, C_SK, C_SV = 2304, 2560, 2816
W_COLS = 3072
R_FQ, R_FV = 0, 256
WT_ROWS = 512
FORGET_DUP = 6
FORGET_STRIDE = 8
VT_ROWS = HEAD_DIM + 16
NT_DIMS = (((1,), (1,)), ((), ()))


def _cparams(n_grid):
    return pltpu.CompilerParams(dimension_semantics=("arbitrary",) * n_grid,
                                vmem_limit_bytes=VMEM_LIMIT_BYTES)


def _split3(x):
    p0 = x.astype(jnp.bfloat16).astype(jnp.float32)
    r1 = x - p0
    p1 = r1.astype(jnp.bfloat16).astype(jnp.float32)
    p2 = (r1 - p1).astype(jnp.bfloat16).astype(jnp.float32)
    return p0, p1, p2


def _layer_norm_rows(x, g, b):
    mu = jnp.mean(x, axis=-1, keepdims=True)
    xc = x - mu
    var = jnp.mean(xc * xc, axis=-1, keepdims=True)
    return xc * lax.rsqrt(var + LN_EPS) * g + b


def _rms_norm_rows(x, g):
    ms = jnp.mean(x * x, axis=-1, keepdims=True)
    return x * lax.rsqrt(ms + RMS_EPS) * g


def _mem_kv_kernel(mem_ref, g_ref, b_ref, w_ref, o_ref):
    mem_n = _layer_norm_rows(mem_ref[0], g_ref[...], b_ref[...]).astype(jnp.bfloat16)
    for l in range(w_ref.shape[0]):
        o_ref[l, 0] = jnp.dot(mem_n, w_ref[l], preferred_element_type=jnp.float32).astype(jnp.bfloat16)


def _mem_kv_call(mem, g, b, w_mem_kv_bf16):
    bsz, m, d = mem.shape
    depth, _, c = w_mem_kv_bf16.shape
    return pl.pallas_call(
        _mem_kv_kernel,
        grid=(bsz,),
        in_specs=[pl.BlockSpec((1, m, d), lambda i: (i, 0, 0)),
                  pl.BlockSpec((1, d), lambda i: (0, 0)),
                  pl.BlockSpec((1, d), lambda i: (0, 0)),
                  pl.BlockSpec((depth, d, c), lambda i: (0, 0, 0))],
        out_specs=pl.BlockSpec((depth, 1, m, c), lambda i: (0, i, 0, 0)),
        out_shape=jax.ShapeDtypeStruct((depth, bsz, m, c), jnp.bfloat16),
        compiler_params=_cparams(1),
        name="mem_kv",
    )(mem, g.reshape(1, d), b.reshape(1, d), w_mem_kv_bf16)


def _proj_kernel(h_ref, lng_ref, lnb_ref, w_ref, wt_ref, bfor_ref, gq_ref, wqupt_ref, gkv_ref, wkup_ref,
                 wvupt_ref, cqt_ref, sqt_ref, ck_ref, sk_ref, mkv_ref, *rest, pre_ln):
    hout_ref = rest[0] if pre_ln else None
    (fqt_ref, fk_ref, fvt_ref, sbq_ref, sbk_ref, sbv_ref,
     mqt_ref, mk_ref, mvt_ref, omem_ref, gate_ref, carry_ref) = rest[1:] if pre_ln else rest
    i = pl.program_id(1)
    tm = h_ref.shape[1]
    bf16, f32 = jnp.bfloat16, jnp.float32
    head_c = (HEAD_DIM ** -0.5) * LOG2E

    @pl.when(i == 0)
    def _():
        carry_ref[...] = jnp.zeros_like(carry_ref)

    def sub_block(r0, nr):
        rs = slice(r0, r0 + nr)
        h = h_ref[0, rs]
        if pre_ln:
            h = _layer_norm_rows(h, lng_ref[...], lnb_ref[...])
            hout_ref[0, rs] = h
        hb = h.astype(bf16)
        proj = jnp.dot(hb, w_ref[...], preferred_element_type=f32)
        proj_t = lax.dot_general(wt_ref[...], hb, NT_DIMS, preferred_element_type=f32)

        def cols(c0, n):
            return proj[:, c0:c0 + n]

        def store_values_t(vt, out_ref):
            for hh in range(N_HEADS):
                out_ref[0, hh, :HEAD_DIM, rs] = vt[HEAD_DIM * hh:HEAD_DIM * (hh + 1), :].astype(bf16)
                out_ref[0, hh, HEAD_DIM:, rs] = jnp.ones((VT_ROWS - HEAD_DIM, nr), bf16)

        lane = lax.broadcasted_iota(jnp.int32, (nr, LANES), 1)
        sub = lane % FORGET_STRIDE
        used = (lane < N_HEADS * FORGET_STRIDE) & (sub < FORGET_DUP)
        xf = cols(C_MISC, LANES) + bfor_ref[...]
        log_f = jnp.minimum(xf, 0.0) - jnp.log1p(jnp.exp(-jnp.abs(xf)))
        log_f = jnp.where(used, log_f, 0.0)
        row = lax.broadcasted_iota(jnp.int32, (nr, nr), 0)
        col = lax.broadcasted_iota(jnp.int32, (nr, nr), 1)
        tril = jnp.where(col <= row, 1.0, 0.0).astype(bf16)
        parts = jnp.concatenate([p.astype(bf16) for p in _split3(log_f)], axis=1)
        csum = jnp.dot(tril, parts, preferred_element_type=f32)
        f_cum = (csum[:, :LANES] + csum[:, LANES:2 * LANES]) + csum[:, 2 * LANES:] + carry_ref[...]
        carry_ref[...] = f_cum[nr - 1:nr, :]
        p0, p1, p2 = _split3(f_cum * LOG2E)
        bias_q = jnp.where(sub == 0, p0, jnp.where(sub == 1, p1, jnp.where(sub == 2, p2, 1.0)))
        bias_q_t = jnp.where(used, bias_q, 0.0).T.astype(bf16)
        bias_k = jnp.where(sub == 3, -p0, jnp.where(sub == 4, -p1, jnp.where(sub == 5, -p2, 1.0)))
        bias_k = jnp.where(used, bias_k, 0.0).astype(bf16)

        fk = cols(C_FK, GROUP_W)
        for p in range(2):
            fqt_ref[0, p, :LANES, rs] = (proj_t[R_FQ + LANES * p:R_FQ + LANES * (p + 1), :] * head_c).astype(bf16)
            fqt_ref[0, p, LANES:, rs] = bias_q_t
            fk_ref[0, rs, 2 * LANES * p:2 * LANES * p + LANES] = fk[:, LANES * p:LANES * (p + 1)].astype(bf16)
            fk_ref[0, rs, 2 * LANES * p + LANES:2 * LANES * (p + 1)] = bias_k
        store_values_t(proj_t[R_FV:R_FV + GROUP_W, :], fvt_ref)

        sbq_ref[0, rs] = (cols(C_SQ, GROUP_W) * head_c).astype(bf16)
        sbk_ref[0, rs] = cols(C_SK, GROUP_W).astype(bf16)
        sbv_ref[0, rs] = cols(C_SV, GROUP_W).astype(bf16)

        cqn = _rms_norm_rows(cols(C_CQ, MLA_Q_RANK), gq_ref[...]).astype(bf16)
        q_up_t = lax.dot_general(wqupt_ref[...], cqn, NT_DIMS, preferred_element_type=f32)
        ckvn = _rms_norm_rows(cols(C_CKV, MLA_KV_RANK), gkv_ref[...]).astype(bf16)
        k_up = jnp.dot(ckvn, wkup_ref[...], preferred_element_type=f32)
        k_rope = cols(C_KRP, LANES) * ck_ref[rs] + cols(C_KRRP, LANES) * sk_ref[rs]
        nq = N_HEADS * LANES
        for hh in range(N_HEADS):
            sl = slice(LANES * hh, LANES * (hh + 1))
            mqt_ref[0, hh, :, rs] = (q_up_t[sl, :] * cqt_ref[:, rs]
                                     + q_up_t[nq + LANES * hh:nq + LANES * (hh + 1), :] * sqt_ref[:, rs]).astype(bf16)
            mk_ref[0, rs, sl] = (k_up[:, sl] + k_rope).astype(bf16)
        store_values_t(lax.dot_general(wvupt_ref[...], ckvn, NT_DIMS, preferred_element_type=f32), mvt_ref)

        mem_q = cols(C_MQ, GROUP_W) * head_c
        lane_h = lax.broadcasted_iota(jnp.int32, (nr, LANES), 1)
        for p in range(2):
            qp = mem_q[:, LANES * p:LANES * (p + 1)]
            kp = mkv_ref[0, :, LANES * p:LANES * (p + 1)]
            vp = mkv_ref[0, :, GROUP_W + LANES * p:GROUP_W + LANES * (p + 1)]
            outs = []
            for hh in range(2):
                in_head = (lane_h >= HEAD_DIM * hh) & (lane_h < HEAD_DIM * (hh + 1))
                qh = jnp.where(in_head, qp, 0.0).astype(bf16)
                s = lax.dot_general(qh, kp, NT_DIMS, preferred_element_type=f32)
                m = jnp.max(s, axis=-1, keepdims=True)
                e = jnp.exp2(s - m)
                pr = e / jnp.sum(e, axis=-1, keepdims=True)
                outs.append(jnp.dot(pr.astype(bf16), vp, preferred_element_type=f32))
            omem_ref[0, rs, LANES * p:LANES * (p + 1)] = jnp.where(lane_h < HEAD_DIM, outs[0], outs[1]).astype(bf16)

        for c in range(4):
            g = cols(C_GATE + GROUP_W * c, GROUP_W)
            gate_ref[0, rs, GROUP_W * c:GROUP_W * (c + 1)] = (g / (1.0 + jnp.exp(-g))).astype(bf16)

    n_sub = 2 if tm % (2 * LANES) == 0 else 1
    for j in range(n_sub):
        sub_block(j * (tm // n_sub), tm // n_sub)


def _proj_call(h, ln_gb, w_packed, w_t, bfor_row, gq, wqupt, gkv, wkup, wvupt, tabs, mkv_l, tm, pre_ln):
    bsz, s, d = h.shape
    ln_g, ln_b = ln_gb
    m = mkv_l.shape[1]
    cqt, sqt, ck, sk = tabs
    bf16 = jnp.bfloat16
    full2 = lambda shape: pl.BlockSpec(shape, lambda b, i: (0, 0))
    row_blk = lambda c: pl.BlockSpec((1, tm, c), lambda b, i: (b, i, 0))
    tab_blk = pl.BlockSpec((tm, LANES), lambda b, i: (i, 0))
    tab_t_blk = pl.BlockSpec((LANES, tm), lambda b, i: (0, i))
    rows = lambda c, dt: (row_blk(c), jax.ShapeDtypeStruct((bsz, s, c), dt))
    feat = lambda n, r: (pl.BlockSpec((1, n, r, tm), lambda b, i: (b, 0, 0, i)),
                         jax.ShapeDtypeStruct((bsz, n, r, s), bf16))
    outs = [feat(2, 2 * LANES), rows(2 * GROUP_W, bf16), feat(N_HEADS, VT_ROWS),
            rows(GROUP_W, bf16), rows(GROUP_W, bf16), rows(GROUP_W, bf16),
            feat(N_HEADS, LANES), rows(2 * GROUP_W, bf16), feat(N_HEADS, VT_ROWS),
            rows(GROUP_W, bf16), rows(4 * GROUP_W, bf16)]
    if pre_ln:
        outs = [rows(d, jnp.float32)] + outs
    return pl.pallas_call(
        functools.partial(_proj_kernel, pre_ln=pre_ln),
        grid=(bsz, s // tm),
        in_specs=[row_blk(d), full2((1, d)), full2((1, d)),
                  full2(w_packed.shape), full2(w_t.shape),
                  full2((1, LANES)),
                  full2((1, MLA_Q_RANK)), full2(wqupt.shape),
                  full2((1, MLA_KV_RANK)), full2(wkup.shape), full2(wvupt.shape),
                  tab_t_blk, tab_t_blk, tab_blk, tab_blk,
                  pl.BlockSpec((1, m, 2 * GROUP_W), lambda b, i: (b, 0, 0))],
        out_specs=[spec for spec, _ in outs],
        out_shape=[shape for _, shape in outs],
        scratch_shapes=[pltpu.VMEM((1, LANES), jnp.float32)],
        compiler_params=_cparams(2),
        name="proj",
    )(h, ln_g.reshape(1, d), ln_b.reshape(1, d), w_packed, w_t, bfor_row, gq, wqupt, gkv, wkup, wvupt,
      cqt, sqt, ck, sk, mkv_l)


def _softmax_attn_kernel(qt_ref, k_ref, vt_ref, o_ref, q_sc, s_sc, m_sc, acc_sc, *, fox, tq, tk):
    pair = pl.program_id(1)
    qi = pl.program_id(2)
    bf16, f32 = jnp.bfloat16, jnp.float32
    kq = q_sc.shape[1]
    for hh in range(2):
        if fox:
            feat = lax.broadcasted_iota(jnp.int32, (kq, tq), 0)
            lo = LANES + FORGET_STRIDE * (2 * pair + hh)
            keep = ((feat >= HEAD_DIM * hh) & (feat < HEAD_DIM * (hh + 1))) | (
                (feat >= lo) & (feat < lo + FORGET_STRIDE))
            q_sc[hh] = jnp.where(keep, qt_ref[0, 0], jnp.zeros((kq, tq), bf16))
        else:
            q_sc[hh] = qt_ref[0, hh]
    m_sc[...] = jnp.full(m_sc.shape, -jnp.inf, f32)
    acc_sc[...] = jnp.zeros(acc_sc.shape, f32)

    def scores(kb, slot, c0=0):
        k0 = pl.multiple_of(kb * tk, tk)
        for hh in range(2):
            k = k_ref[0, pl.ds(k0, tk), :] if fox else k_ref[0, pl.ds(k0, tk), LANES * hh:LANES * (hh + 1)]
            s_sc[slot, hh, :, c0:] = jnp.dot(k, q_sc[hh, :, c0:], preferred_element_type=f32)

    def update(kb, slot, c0=0, nc=tq, triangle=False):
        k0 = pl.multiple_of(kb * tk, tk)
        qs = slice(c0, c0 + nc)
        if triangle:
            causal = (lax.broadcasted_iota(jnp.int32, (tk, nc), 0)
                      <= lax.broadcasted_iota(jnp.int32, (tk, nc), 1))
        for hh in range(2):
            s = s_sc[slot, hh, :, qs]
            if triangle:
                s = jnp.where(causal, s, -jnp.inf)
            m_prev = m_sc[hh, :, qs]
            m_new = jnp.maximum(m_prev, jnp.max(s, axis=0, keepdims=True))
            alpha = jnp.exp2(m_prev - m_new)
            p = jnp.exp2(s - m_new)
            acc_sc[hh, :, qs] = (alpha * acc_sc[hh, :, qs]
                                 + jnp.dot(vt_ref[0, hh, :, pl.ds(k0, tk)], p.astype(bf16),
                                           preferred_element_type=f32))
            m_sc[hh, :, qs] = m_new

    n_diag = tq // tk
    unroll = 4 if n_diag % 4 == 0 else 2
    assert n_diag % unroll == 0
    n_full = qi * n_diag
    scores(0, 0)

    def body(j, c):
        for u in range(unroll):
            kb = unroll * j + u
            scores(kb + 1, (u + 1) % 2)
            update(kb, u % 2)
        return c

    lax.fori_loop(0, n_full // unroll, body, 0)
    for d in range(n_diag):
        if d + 1 < n_diag:
            scores(n_full + d + 1, (d + 1) % 2, c0=(d + 1) * tk)
        update(n_full + d, d % 2, c0=d * tk, nc=tk, triangle=True)
        if d + 1 < n_diag:
            update(n_full + d, d % 2, c0=(d + 1) * tk, nc=tq - (d + 1) * tk)
    out_t = jnp.concatenate([acc_sc[hh, :HEAD_DIM] / acc_sc[hh, HEAD_DIM:HEAD_DIM + 1] for hh in range(2)],
                            axis=0)
    o_ref[0] = out_t.T.astype(o_ref.dtype)


def _softmax_attn_call(qt, k, vt, *, fox, tq, tk, name):
    bsz, nqt, kq, s = qt.shape
    kw = 2 * LANES
    kern = functools.partial(_softmax_attn_kernel, fox=fox, tq=tq, tk=tk)
    return pl.pallas_call(
        kern,
        grid=(bsz, 2, s // tq),
        in_specs=[pl.BlockSpec((1, nqt // 2, kq, tq), lambda b, p, i: (b, p, 0, i)),
                  pl.BlockSpec((1, s, kw), lambda b, p, i: (b, 0, p)),
                  pl.BlockSpec((1, 2, VT_ROWS, s), lambda b, p, i: (b, p, 0, 0))],
        out_specs=pl.BlockSpec((1, tq, LANES), lambda b, p, i: (b, i, p)),
        out_shape=jax.ShapeDtypeStruct((bsz, s, GROUP_W), jnp.bfloat16),
        scratch_shapes=[pltpu.VMEM((2, kq, tq), jnp.bfloat16),
                        pltpu.VMEM((2, 2, tk, tq), jnp.float32),
                        pltpu.VMEM((2, 1, tq), jnp.float32),
                        pltpu.VMEM((2, VT_ROWS, tq), jnp.float32)],
        compiler_params=_cparams(3),
        name=name,
    )(qt, k, vt)


def _sb_attn_kernel(q_ref, k_ref, v_ref, o_ref, q_sc, z_sc, c_sc, acc_sc, *, tq, tk):
    qi = pl.program_id(2)
    bf16, f32 = jnp.bfloat16, jnp.float32
    lane_q = lax.broadcasted_iota(jnp.int32, (tq, LANES), 1)
    rj = lax.broadcasted_iota(jnp.int32, (tk, tk), 0)
    cs = lax.broadcasted_iota(jnp.int32, (tk, tk), 1)
    upper = jnp.where(rj >= cs, 1.0, 0.0).astype(bf16)
    for hh in range(2):
        in_head = (lane_q >= HEAD_DIM * hh) & (lane_q < HEAD_DIM * (hh + 1))
        q_sc[hh] = jnp.where(in_head, q_ref[0], jnp.zeros((tq, LANES), bf16))
    c_sc[...] = jnp.zeros(c_sc.shape, f32)
    acc_sc[...] = jnp.zeros(acc_sc.shape, f32)

    def scores(kb, slot, r0=0):
        k0 = pl.multiple_of(kb * tk, tk)
        k = k_ref[0, pl.ds(k0, tk), :]
        for hh in range(2):
            z_sc[slot, hh, r0:] = lax.dot_general(q_sc[hh, r0:], k, NT_DIMS, preferred_element_type=f32)

    def update(kb, slot, r0=0, nr=tq, triangle=False):
        k0 = pl.multiple_of(kb * tk, tk)
        v = v_ref[0, pl.ds(k0, tk), :]
        rows = slice(r0, r0 + nr)
        if triangle:
            valid = (lax.broadcasted_iota(jnp.int32, (nr, tk), 1)
                     < lax.broadcasted_iota(jnp.int32, (nr, tk), 0))
        for hh in range(2):
            z = z_sc[slot, hh, rows]
            sp = jnp.maximum(jnp.log2(1.0 + jnp.exp2(jnp.minimum(z, EXP2_CLAMP))), z)
            if triangle:
                sp = jnp.where(valid, sp, 0.0)
            csum = (jnp.dot(sp.astype(bf16), upper, preferred_element_type=f32)
                    + jnp.tile(c_sc[hh, rows], (1, tk // LANES)))
            w = jnp.exp2(z - csum)
            if triangle:
                w = jnp.where(valid, w, 0.0)
            acc_sc[hh, rows] += jnp.dot(w.astype(bf16), v, preferred_element_type=f32)
            c_sc[hh, rows] = jnp.broadcast_to(csum[:, 0:1], (nr, LANES))

    n_diag = tq // tk
    unroll = 4 if n_diag % 4 == 0 else 2
    assert n_diag % unroll == 0
    n_full = qi * n_diag
    scores(n_full + n_diag - 1, 1, r0=(n_diag - 1) * tk)
    for d in reversed(range(n_diag)):
        kb = n_full + d
        scores(jnp.maximum(kb - 1, 0), (d + 1) % 2, r0=max(d - 1, 0) * tk)
        update(kb, d % 2, r0=d * tk, nr=tk, triangle=True)
        if d + 1 < n_diag:
            update(kb, d % 2, r0=(d + 1) * tk, nr=tq - (d + 1) * tk)

    def body(n, c):
        top = n_full - 1 - unroll * n
        for u in range(unroll):
            kb = top - u
            scores(jnp.maximum(kb - 1, 0), u % 2)
            update(kb, (u + 1) % 2)
        return c

    lax.fori_loop(0, n_full // unroll, body, 0)
    o_ref[0] = jnp.where(lane_q < HEAD_DIM, acc_sc[0], acc_sc[1]).astype(o_ref.dtype)


def _sb_attn_call(q, k, v, *, tq, tk):
    bsz, s, _ = q.shape
    kern = functools.partial(_sb_attn_kernel, tq=tq, tk=tk)
    return pl.pallas_call(
        kern,
        grid=(bsz, 2, s // tq),
        in_specs=[pl.BlockSpec((1, tq, LANES), lambda b, p, i: (b, i, p)),
                  pl.BlockSpec((1, s, LANES), lambda b, p, i: (b, 0, p)),
                  pl.BlockSpec((1, s, LANES), lambda b, p, i: (b, 0, p))],
        out_specs=pl.BlockSpec((1, tq, LANES), lambda b, p, i: (b, i, p)),
        out_shape=jax.ShapeDtypeStruct((bsz, s, GROUP_W), jnp.bfloat16),
        scratch_shapes=[pltpu.VMEM((2, tq, LANES), jnp.bfloat16),
                        pltpu.VMEM((2, 2, tq, tk), jnp.float32),
                        pltpu.VMEM((2, tq, LANES), jnp.float32),
                        pltpu.VMEM((2, tq, LANES), jnp.float32)],
        compiler_params=_cparams(3),
        name="sb_attn",
    )(q, k, v)


def _out_kernel(of_ref, os_ref, om_ref, ox_ref, gate_ref, h_ref, w_ref, g_ref, b_ref, o_ref, *, alpha):
    f32 = jnp.float32
    mixed = jnp.concatenate([of_ref[...], os_ref[...], om_ref[...], ox_ref[...]], axis=1)
    y = jnp.dot(mixed * gate_ref[...], w_ref[...], preferred_element_type=f32)
    o_ref[...] = _layer_norm_rows(alpha * h_ref[...] + y, g_ref[...], b_ref[...])


def _out_call(o_fox, o_sb, o_mla, o_mem, gate, h2d, w_out_bf16, g, b, alpha, tm):
    n, d = h2d.shape
    gw = o_fox.shape[1]
    kern = functools.partial(_out_kernel, alpha=alpha)
    blk = lambda c: pl.BlockSpec((tm, c), lambda i: (i, 0))
    return pl.pallas_call(
        kern,
        grid=(n // tm,),
        in_specs=[blk(gw), blk(gw), blk(gw), blk(gw), blk(4 * gw), blk(d),
                  pl.BlockSpec(w_out_bf16.shape, lambda i: (0, 0)),
                  pl.BlockSpec((1, d), lambda i: (0, 0)),
                  pl.BlockSpec((1, d), lambda i: (0, 0))],
        out_specs=blk(d),
        out_shape=jax.ShapeDtypeStruct((n, d), jnp.float32),
        compiler_params=_cparams(1),
        name="out_proj",
    )(o_fox, o_sb, o_mla, o_mem, gate, h2d, w_out_bf16, g.reshape(1, d), b.reshape(1, d))


def _pack_w_in(w):
    d = w.shape[0]
    o = 0
    cols = {}
    for name, width in (("fq", 256), ("fk", 256), ("fv", 256), ("fl", 4), ("sq", 256), ("sk", 256),
                        ("sv", 256), ("cq", 256), ("ckv", 128), ("kr", 32), ("mq", 256), ("gate", 1024)):
        cols[name] = w[:, o:o + width]
        o += width
    z = lambda n: jnp.zeros((d, n), w.dtype)
    half = MLA_ROPE // 2
    kr = cols["kr"]
    kr_rot = jnp.concatenate([-kr[:, half:], kr[:, :half]], axis=1)
    krp = jnp.concatenate([z(MLA_NOPE), kr, z(LANES - MLA_NOPE - MLA_ROPE)], axis=1)
    krrp = jnp.concatenate([z(MLA_NOPE), kr_rot, z(LANES - MLA_NOPE - MLA_ROPE)], axis=1)
    misc = []
    for hh in range(N_HEADS):
        misc += [cols["fl"][:, hh:hh + 1]] * FORGET_DUP + [z(FORGET_STRIDE - FORGET_DUP)]
    misc.append(z(LANES - N_HEADS * FORGET_STRIDE))
    packed = jnp.concatenate([cols["gate"], cols["cq"], cols["ckv"]] + misc
                             + [cols["mq"], krp, krrp, cols["fk"], cols["sq"], cols["sk"], cols["sv"]], axis=1)
    assert packed.shape[1] == W_COLS
    w_t = jnp.concatenate([cols["fq"], cols["fv"]], axis=1).T
    assert w_t.shape[0] == WT_ROWS
    return packed.astype(jnp.bfloat16), w_t.astype(jnp.bfloat16)


def _pack_forget_bias(b_forget_l):
    row = jnp.zeros((LANES,), jnp.float32)
    for hh in range(N_HEADS):
        row = row.at[FORGET_STRIDE * hh:FORGET_STRIDE * hh + FORGET_DUP].set(b_forget_l[hh])
    return row.reshape(1, LANES)


def _pack_mla_q_up_t(w):
    r = w.shape[0]
    per = MLA_NOPE + MLA_ROPE
    half = MLA_ROPE // 2
    z = lambda n: jnp.zeros((r, n), w.dtype)
    plain, rot = [], []
    for hh in range(N_HEADS):
        nope = w[:, per * hh:per * hh + MLA_NOPE]
        rope = w[:, per * hh + MLA_NOPE:per * (hh + 1)]
        rope_rot = jnp.concatenate([-rope[:, half:], rope[:, :half]], axis=1)
        plain += [nope, rope, z(LANES - per)]
        rot += [z(MLA_NOPE), rope_rot, z(LANES - per)]
    return jnp.concatenate(plain + rot, axis=1).T.astype(jnp.bfloat16)


def _pack_mla_kv_up(w):
    r = w.shape[0]
    per = MLA_NOPE + HEAD_DIM
    z = jnp.zeros((r, LANES - MLA_NOPE), w.dtype)
    ks, vs = [], []
    for hh in range(N_HEADS):
        ks += [w[:, per * hh:per * hh + MLA_NOPE], z]
        vs.append(w[:, per * hh + MLA_NOPE:per * (hh + 1)])
    return (jnp.concatenate(ks, axis=1).astype(jnp.bfloat16),
            jnp.concatenate(vs, axis=1).T.astype(jnp.bfloat16))


def _rope_tables(s, q_scale):
    half = MLA_ROPE // 2
    inv_freq = ROPE_THETA ** (-jnp.arange(half, dtype=jnp.float32) / half)
    ang = jnp.arange(s).astype(jnp.float32)[:, None] * inv_freq[None, :]
    cos, sin = jnp.cos(ang), jnp.sin(ang)
    ones = jnp.ones((s, MLA_NOPE), jnp.float32)
    z_nope = jnp.zeros((s, MLA_NOPE), jnp.float32)
    z_pad = jnp.zeros((s, LANES - MLA_NOPE - MLA_ROPE), jnp.float32)
    cos_q = jnp.concatenate([ones, cos, cos, z_pad], axis=1) * q_scale
    sin_q = jnp.concatenate([z_nope, sin, sin, z_pad], axis=1) * q_scale
    cos_k = jnp.concatenate([z_nope, cos, cos, z_pad], axis=1)
    sin_k = jnp.concatenate([z_nope, sin, sin, z_pad], axis=1)
    return cos_q.T, sin_q.T, cos_k, sin_k


def kernel(x, mem, ln_in_g, ln_in_b, mem_ln_g, mem_ln_b, w_in, b_forget, mla_q_norm_g, w_mla_q_up,
           mla_kv_norm_g, w_mla_kv_up, w_mem_kv, w_out, ln_g, ln_b):
    bsz, s, d = x.shape
    depth = w_in.shape[0]
    alpha = (2 * depth) ** 0.25
    tm = min(512, s)
    tm_out = min(1024, s)
    tq_sm = min(2048, s)
    tq_sb = min(1024, s)
    tk_sm = min(512, tq_sm // 2)
    tk_sb = min(256, tq_sb // 2)
    mla_scale = (MLA_NOPE + MLA_ROPE) ** -0.5
    tabs = _rope_tables(s, mla_scale * LOG2E)

    mkv = _mem_kv_call(mem, mem_ln_g, mem_ln_b, w_mem_kv.astype(jnp.bfloat16))

    h = x
    for l in range(depth):
        w_packed, w_t = _pack_w_in(w_in[l])
        w_k_up, w_v_up_t = _pack_mla_kv_up(w_mla_kv_up[l])
        outs = _proj_call(
            h, (ln_in_g, ln_in_b), w_packed, w_t, _pack_forget_bias(b_forget[l]),
            mla_q_norm_g[l].reshape(1, -1), _pack_mla_q_up_t(w_mla_q_up[l]),
            mla_kv_norm_g[l].reshape(1, -1), w_k_up, w_v_up_t,
            tabs, mkv[l], tm, pre_ln=(l == 0))
        if l == 0:
            h, outs = outs[0], outs[1:]
        (fqt, fk, fvt, sbq, sbk, sbv, mqt, mk, mvt, o_mem, gate) = outs
        o_fox = _softmax_attn_call(fqt, fk, fvt, fox=True, tq=tq_sm, tk=tk_sm, name="fox_attn")
        o_sb = _sb_attn_call(sbq, sbk, sbv, tq=tq_sb, tk=tk_sb)
        o_mla = _softmax_attn_call(mqt, mk, mvt, fox=False, tq=tq_sm, tk=tk_sm, name="mla_attn")
        n = bsz * s
        h = _out_call(o_fox.reshape(n, -1), o_sb.reshape(n, -1), o_mla.reshape(n, -1), o_mem.reshape(n, -1),
                      gate.reshape(n, -1), h.reshape(n, d), w_out[l].astype(jnp.bfloat16),
                      ln_g[l], ln_b[l], alpha, tm_out).reshape(bsz, s, d)
    return h
```

```python
import functools

import jax
import jax.numpy as jnp
from jax import lax
from jax.experimental import pallas as pl
from jax.experimental.pallas import tpu as pltpu

N_HEADS = 4
HEAD_DIM = 64
GROUP_W = N_HEADS * HEAD_DIM
MLA_NOPE = 64
MLA_ROPE = 32
MLA_Q_RANK = 256
MLA_KV_RANK = 128
ROPE_THETA = 10000.0
LN_EPS = 1e-5
RMS_EPS = 1e-6
LOG2E = 1.4426950408889634
EXP2_CLAMP = 126.0

LANES = 128
VMEM_LIMIT_BYTES = 56 * 1024 * 1024

C_GATE = 0
C_CQ = 1024
C_CKV = 1280
C_MISC = 1408
C_MQ = 1536
C_KRP = 1792
C_KRRP = 1920
C_FK = 2048
C_SQ, C_SK, C_SV = 2304, 2560, 2816
W_COLS = 3072
R_FQ, R_FV = 0, 256
WT_ROWS = 512
FORGET_DUP = 6
FORGET_STRIDE = 8
VT_ROWS = HEAD_DIM + 16
NT_DIMS = (((1,), (1,)), ((), ()))


def _cparams(n_grid):
    return pltpu.CompilerParams(dimension_semantics=("arbitrary",) * n_grid,
                                vmem_limit_bytes=VMEM_LIMIT_BYTES)


def _split3(x):
    p0 = x.astype(jnp.bfloat16).astype(jnp.float32)
    r1 = x - p0
    p1 = r1.astype(jnp.bfloat16).astype(jnp.float32)
    p2 = (r1 - p1).astype(jnp.bfloat16).astype(jnp.float32)
    return p0, p1, p2


def _layer_norm_rows(x, g, b):
    mu = jnp.mean(x, axis=-1, keepdims=True)
    xc = x - mu
    var = jnp.mean(xc * xc, axis=-1, keepdims=True)
    return xc * lax.rsqrt(var + LN_EPS) * g + b


def _rms_norm_rows(x, g):
    ms = jnp.mean(x * x, axis=-1, keepdims=True)
    return x * lax.rsqrt(ms + RMS_EPS) * g


def _mem_kv_kernel(mem_ref, g_ref, b_ref, w_ref, o_ref):
    mem_n = _layer_norm_rows(mem_ref[0], g_ref[...], b_ref[...]).astype(jnp.bfloat16)
    for l in range(w_ref.shape[0]):
        o_ref[l, 0] = jnp.dot(mem_n, w_ref[l], preferred_element_type=jnp.float32).astype(jnp.bfloat16)


def _mem_kv_call(mem, g, b, w_mem_kv_bf16):
    bsz, m, d = mem.shape
    depth, _, c = w_mem_kv_bf16.shape
    return pl.pallas_call(
        _mem_kv_kernel,
        grid=(bsz,),
        in_specs=[pl.BlockSpec((1, m, d), lambda i: (i, 0, 0)),
                  pl.BlockSpec((1, d), lambda i: (0, 0)),
                  pl.BlockSpec((1, d), lambda i: (0, 0)),
                  pl.BlockSpec((depth, d, c), lambda i: (0, 0, 0))],
        out_specs=pl.BlockSpec((depth, 1, m, c), lambda i: (0, i, 0, 0)),
        out_shape=jax.ShapeDtypeStruct((depth, bsz, m, c), jnp.bfloat16),
        compiler_params=_cparams(1),
        name="mem_kv",
    )(mem, g.reshape(1, d), b.reshape(1, d), w_mem_kv_bf16)


def _proj_kernel(h_ref, lng_ref, lnb_ref, w_ref, wt_ref, bfor_ref, gq_ref, wqupt_ref, gkv_ref, wkup_ref,
                 wvupt_ref, cqt_ref, sqt_ref, ck_ref, sk_ref, mkv_ref, *rest, pre_ln):
    hout_ref = rest[0] if pre_ln else None
    (fqt_ref, fk_ref, fvt_ref, sbq_ref, sbk_ref, sbv_ref,
     mqt_ref, mk_ref, mvt_ref, omem_ref, gate_ref, carry_ref) = rest[1:] if pre_ln else rest
    i = pl.program_id(1)
    tm = h_ref.shape[1]
    bf16, f32 = jnp.bfloat16, jnp.float32
    head_c = (HEAD_DIM ** -0.5) * LOG2E

    @pl.when(i == 0)
    def _():
        carry_ref[...] = jnp.zeros_like(carry_ref)

    def sub_block(r0, nr):
        rs = slice(r0, r0 + nr)
        h = h_ref[0, rs]
        if pre_ln:
            h = _layer_norm_rows(h, lng_ref[...], lnb_ref[...])
            hout_ref[0, rs] = h
        hb = h.astype(bf16)
        proj = jnp.dot(hb, w_ref[...], preferred_element_type=f32)
        proj_t = lax.dot_general(wt_ref[...], hb, NT_DIMS, preferred_element_type=f32)

        def cols(c0, n):
            return proj[:, c0:c0 + n]

        def store_values_t(vt, out_ref):
            for hh in range(N_HEADS):
                out_ref[0, hh, :HEAD_DIM, rs] = vt[HEAD_DIM * hh:HEAD_DIM * (hh + 1), :].astype(bf16)
                out_ref[0, hh, HEAD_DIM:, rs] = jnp.ones((VT_ROWS - HEAD_DIM, nr), bf16)

        lane = lax.broadcasted_iota(jnp.int32, (nr, LANES), 1)
        sub = lane % FORGET_STRIDE
        used = (lane < N_HEADS * FORGET_STRIDE) & (sub < FORGET_DUP)
        xf = cols(C_MISC, LANES) + bfor_ref[...]
        log_f = jnp.minimum(xf, 0.0) - jnp.log1p(jnp.exp(-jnp.abs(xf)))
        log_f = jnp.where(used, log_f, 0.0)
        row = lax.broadcasted_iota(jnp.int32, (nr, nr), 0)
        col = lax.broadcasted_iota(jnp.int32, (nr, nr), 1)
        tril = jnp.where(col <= row, 1.0, 0.0).astype(bf16)
        parts = jnp.concatenate([p.astype(bf16) for p in _split3(log_f)], axis=1)
        csum = jnp.dot(tril, parts, preferred_element_type=f32)
        f_cum = (csum[:, :LANES] + csum[:, LANES:2 * LANES]) + csum[:, 2 * LANES:] + carry_ref[...]
        carry_ref[...] = f_cum[nr - 1:nr, :]
        p0, p1, p2 = _split3(f_cum * LOG2E)
        bias_q = jnp.where(sub == 0, p0, jnp.where(sub == 1, p1, jnp.where(sub == 2, p2, 1.0)))
        bias_q_t = jnp.where(used, bias_q, 0.0).T.astype(bf16)
        bias_k = jnp.where(sub == 3, -p0, jnp.where(sub == 4, -p1, jnp.where(sub == 5, -p2, 1.0)))
        bias_k = jnp.where(used, bias_k, 0.0).astype(bf16)

        fk = cols(C_FK, GROUP_W)
        for p in range(2):
            fqt_ref[0, p, :LANES, rs] = (proj_t[R_FQ + LANES * p:R_FQ + LANES * (p + 1), :] * head_c).astype(bf16)
            fqt_ref[0, p, LANES:, rs] = bias_q_t
            fk_ref[0, rs, 2 * LANES * p:2 * LANES * p + LANES] = fk[:, LANES * p:LANES * (p + 1)].astype(bf16)
            fk_ref[0, rs, 2 * LANES * p + LANES:2 * LANES * (p + 1)] = bias_k
        store_values_t(proj_t[R_FV:R_FV + GROUP_W, :], fvt_ref)

        sbq_ref[0, rs] = (cols(C_SQ, GROUP_W) * head_c).astype(bf16)
        sbk_ref[0, rs] = cols(C_SK, GROUP_W).astype(bf16)
        sbv_ref[0, rs] = cols(C_SV, GROUP_W).astype(bf16)

        cqn = _rms_norm_rows(cols(C_CQ, MLA_Q_RANK), gq_ref[...]).astype(bf16)
        q_up_t = lax.dot_general(wqupt_ref[...], cqn, NT_DIMS, preferred_element_type=f32)
        ckvn = _rms_norm_rows(cols(C_CKV, MLA_KV_RANK), gkv_ref[...]).astype(bf16)
        k_up = jnp.dot(ckvn, wkup_ref[...], preferred_element_type=f32)
        k_rope = cols(C_KRP, LANES) * ck_ref[rs] + cols(C_KRRP, LANES) * sk_ref[rs]
        nq = N_HEADS * LANES
        for hh in range(N_HEADS):
            sl = slice(LANES * hh, LANES * (hh + 1))
            mqt_ref[0, hh, :, rs] = (q_up_t[sl, :] * cqt_ref[:, rs]
                                     + q_up_t[nq + LANES * hh:nq + LANES * (hh + 1), :] * sqt_ref[:, rs]).astype(bf16)
            mk_ref[0, rs, sl] = (k_up[:, sl] + k_rope).astype(bf16)
        store_values_t(lax.dot_general(wvupt_ref[...], ckvn, NT_DIMS, preferred_element_type=f32), mvt_ref)

        mem_q = cols(C_MQ, GROUP_W) * head_c
        lane_h = lax.broadcasted_iota(jnp.int32, (nr, LANES), 1)
        for p in range(2):
            qp = mem_q[:, LANES * p:LANES * (p + 1)]
            kp = mkv_ref[0, :, LANES * p:LANES * (p + 1)]
            vp = mkv_ref[0, :, GROUP_W + LANES * p:GROUP_W + LANES * (p + 1)]
            outs = []
            for hh in range(2):
                in_head = (lane_h >= HEAD_DIM * hh) & (lane_h < HEAD_DIM * (hh + 1))
                qh = jnp.where(in_head, qp, 0.0).astype(bf16)
                s = lax.dot_general(qh, kp, NT_DIMS, preferred_element_type=f32)
                m = jnp.max(s, axis=-1, keepdims=True)
                e = jnp.exp2(s - m)
                pr = e / jnp.sum(e, axis=-1, keepdims=True)
                outs.append(jnp.dot(pr.astype(bf16), vp, preferred_element_type=f32))
            omem_ref[0, rs, LANES * p:LANES * (p + 1)] = jnp.where(lane_h < HEAD_DIM, outs[0], outs[1]).astype(bf16)

        for c in range(4):
            g = cols(C_GATE + GROUP_W * c, GROUP_W)
            gate_ref[0, rs, GROUP_W * c:GROUP_W * (c + 1)] = (g / (1.0 + jnp.exp(-g))).astype(bf16)

    n_sub = max(tm // (2 * LANES), 1)
    for j in range(n_sub):
        sub_block(j * (tm // n_sub), tm // n_sub)


def _proj_call(h, ln_gb, w_packed, w_t, bfor_row, gq, wqupt, gkv, wkup, wvupt, tabs, mkv_l, tm, pre_ln):
    bsz, s, d = h.shape
    ln_g, ln_b = ln_gb
    m = mkv_l.shape[1]
    cqt, sqt, ck, sk = tabs
    bf16 = jnp.bfloat16
    full2 = lambda shape: pl.BlockSpec(shape, lambda b, i: (0, 0))
    row_blk = lambda c: pl.BlockSpec((1, tm, c), lambda b, i: (b, i, 0))
    tab_blk = pl.BlockSpec((tm, LANES), lambda b, i: (i, 0))
    tab_t_blk = pl.BlockSpec((LANES, tm), lambda b, i: (0, i))
    rows = lambda c, dt: (row_blk(c), jax.ShapeDtypeStruct((bsz, s, c), dt))
    feat = lambda n, r: (pl.BlockSpec((1, n, r, tm), lambda b, i: (b, 0, 0, i)),
                         jax.ShapeDtypeStruct((bsz, n, r, s), bf16))
    outs = [feat(2, 2 * LANES), rows(2 * GROUP_W, bf16), feat(N_HEADS, VT_ROWS),
            rows(GROUP_W, bf16), rows(GROUP_W, bf16), rows(GROUP_W, bf16),
            feat(N_HEADS, LANES), rows(2 * GROUP_W, bf16), feat(N_HEADS, VT_ROWS),
            rows(GROUP_W, bf16), rows(4 * GROUP_W, bf16)]
    if pre_ln:
        outs = [rows(d, jnp.float32)] + outs
    return pl.pallas_call(
        functools.partial(_proj_kernel, pre_ln=pre_ln),
        grid=(bsz, s // tm),
        in_specs=[row_blk(d), full2((1, d)), full2((1, d)),
                  full2(w_packed.shape), full2(w_t.shape),
                  full2((1, LANES)),
                  full2((1, MLA_Q_RANK)), full2(wqupt.shape),
                  full2((1, MLA_KV_RANK)), full2(wkup.shape), full2(wvupt.shape),
                  tab_t_blk, tab_t_blk, tab_blk, tab_blk,
                  pl.BlockSpec((1, m, 2 * GROUP_W), lambda b, i: (b, 0, 0))],
        out_specs=[spec for spec, _ in outs],
        out_shape=[shape for _, shape in outs],
        scratch_shapes=[pltpu.VMEM((1, LANES), jnp.float32)],
        compiler_params=_cparams(2),
        name="proj",
    )(h, ln_g.reshape(1, d), ln_b.reshape(1, d), w_packed, w_t, bfor_row, gq, wqupt, gkv, wkup, wvupt,
      cqt, sqt, ck, sk, mkv_l)


def _softmax_attn_kernel(qt_ref, k_ref, vt_ref, o_ref, q_sc, s_sc, m_sc, acc_sc, *, fox, tq, tk):
    pair = pl.program_id(1)
    qi = pl.program_id(2)
    bf16, f32 = jnp.bfloat16, jnp.float32
    kq = q_sc.shape[1]
    for hh in range(2):
        if fox:
            feat = lax.broadcasted_iota(jnp.int32, (kq, tq), 0)
            lo = LANES + FORGET_STRIDE * (2 * pair + hh)
            keep = ((feat >= HEAD_DIM * hh) & (feat < HEAD_DIM * (hh + 1))) | (
                (feat >= lo) & (feat < lo + FORGET_STRIDE))
            q_sc[hh] = jnp.where(keep, qt_ref[0, 0], jnp.zeros((kq, tq), bf16))
        else:
            q_sc[hh] = qt_ref[0, hh]
    m_sc[...] = jnp.full(m_sc.shape, -jnp.inf, f32)
    acc_sc[...] = jnp.zeros(acc_sc.shape, f32)

    def scores(kb, slot, c0=0):
        k0 = pl.multiple_of(kb * tk, tk)
        for hh in range(2):
            k = k_ref[0, pl.ds(k0, tk), :] if fox else k_ref[0, pl.ds(k0, tk), LANES * hh:LANES * (hh + 1)]
            s_sc[slot, hh, :, c0:] = jnp.dot(k, q_sc[hh, :, c0:], preferred_element_type=f32)

    def update(kb, slot, c0=0, nc=tq, triangle=False):
        k0 = pl.multiple_of(kb * tk, tk)
        qs = slice(c0, c0 + nc)
        if triangle:
            causal = (lax.broadcasted_iota(jnp.int32, (tk, nc), 0)
                      <= lax.broadcasted_iota(jnp.int32, (tk, nc), 1))
        for hh in range(2):
            s = s_sc[slot, hh, :, qs]
            if triangle:
                s = jnp.where(causal, s, -jnp.inf)
            m_prev = m_sc[hh, :, qs]
            m_new = jnp.maximum(m_prev, jnp.max(s, axis=0, keepdims=True))
            alpha = jnp.exp2(m_prev - m_new)
            p = jnp.exp2(s - m_new)
            acc_sc[hh, :, qs] = (alpha * acc_sc[hh, :, qs]
                                 + jnp.dot(vt_ref[0, hh, :, pl.ds(k0, tk)], p.astype(bf16),
                                           preferred_element_type=f32))
            m_sc[hh, :, qs] = m_new

    n_diag = tq // tk
    unroll = 4 if n_diag % 4 == 0 else 2
    assert n_diag % unroll == 0
    n_full = qi * n_diag
    scores(0, 0)

    def body(j, c):
        for u in range(unroll):
            kb = unroll * j + u
            scores(kb + 1, (u + 1) % 2)
            update(kb, u % 2)
        return c

    lax.fori_loop(0, n_full // unroll, body, 0)
    for d in range(n_diag):
        if d + 1 < n_diag:
            scores(n_full + d + 1, (d + 1) % 2, c0=(d + 1) * tk)
        update(n_full + d, d % 2, c0=d * tk, nc=tk, triangle=True)
        if d + 1 < n_diag:
            update(n_full + d, d % 2, c0=(d + 1) * tk, nc=tq - (d + 1) * tk)
    out_t = jnp.concatenate([acc_sc[hh, :HEAD_DIM] / acc_sc[hh, HEAD_DIM:HEAD_DIM + 1] for hh in range(2)],
                            axis=0)
    o_ref[0] = out_t.T.astype(o_ref.dtype)


def _softmax_attn_call(qt, k, vt, *, fox, tq, tk, name):
    bsz, nqt, kq, s = qt.shape
    kw = 2 * LANES
    kern = functools.partial(_softmax_attn_kernel, fox=fox, tq=tq, tk=tk)
    return pl.pallas_call(
        kern,
        grid=(bsz, 2, s // tq),
        in_specs=[pl.BlockSpec((1, nqt // 2, kq, tq), lambda b, p, i: (b, p, 0, i)),
                  pl.BlockSpec((1, s, kw), lambda b, p, i: (b, 0, p)),
                  pl.BlockSpec((1, 2, VT_ROWS, s), lambda b, p, i: (b, p, 0, 0))],
        out_specs=pl.BlockSpec((1, tq, LANES), lambda b, p, i: (b, i, p)),
        out_shape=jax.ShapeDtypeStruct((bsz, s, GROUP_W), jnp.bfloat16),
        scratch_shapes=[pltpu.VMEM((2, kq, tq), jnp.bfloat16),
                        pltpu.VMEM((2, 2, tk, tq), jnp.float32),
                        pltpu.VMEM((2, 1, tq), jnp.float32),
                        pltpu.VMEM((2, VT_ROWS, tq), jnp.float32)],
        compiler_params=_cparams(3),
        name=name,
    )(qt, k, vt)


def _sb_attn_kernel(q_ref, k_ref, v_ref, o_ref, q_sc, z_sc, c_sc, acc_sc, *, tq, tk):
    qi = pl.program_id(2)
    bf16, f32 = jnp.bfloat16, jnp.float32
    lane_q = lax.broadcasted_iota(jnp.int32, (tq, LANES), 1)
    rj = lax.broadcasted_iota(jnp.int32, (tk, tk), 0)
    cs = lax.broadcasted_iota(jnp.int32, (tk, tk), 1)
    upper = jnp.where(rj >= cs, 1.0, 0.0).astype(bf16)
    for hh in range(2):
        in_head = (lane_q >= HEAD_DIM * hh) & (lane_q < HEAD_DIM * (hh + 1))
        q_sc[hh] = jnp.where(in_head, q_ref[0], jnp.zeros((tq, LANES), bf16))
    c_sc[...] = jnp.zeros(c_sc.shape, f32)
    acc_sc[...] = jnp.zeros(acc_sc.shape, f32)

    def scores(kb, slot, r0=0):
        k0 = pl.multiple_of(kb * tk, tk)
        k = k_ref[0, pl.ds(k0, tk), :]
        for hh in range(2):
            z_sc[slot, hh, r0:] = lax.dot_general(q_sc[hh, r0:], k, NT_DIMS, preferred_element_type=f32)

    def update(kb, slot, r0=0, nr=tq, triangle=False):
        k0 = pl.multiple_of(kb * tk, tk)
        v = v_ref[0, pl.ds(k0, tk), :]
        rows = slice(r0, r0 + nr)
        if triangle:
            valid = (lax.broadcasted_iota(jnp.int32, (nr, tk), 1)
                     < lax.broadcasted_iota(jnp.int32, (nr, tk), 0))
        for hh in range(2):
            z = z_sc[slot, hh, rows]
            sp = jnp.maximum(jnp.log2(1.0 + jnp.exp2(jnp.minimum(z, EXP2_CLAMP))), z)
            if triangle:
                sp = jnp.where(valid, sp, 0.0)
            csum = (jnp.dot(sp.astype(bf16), upper, preferred_element_type=f32)
                    + jnp.tile(c_sc[hh, rows], (1, tk // LANES)))
            w = jnp.exp2(z - csum)
            if triangle:
                w = jnp.where(valid, w, 0.0)
            acc_sc[hh, rows] += jnp.dot(w.astype(bf16), v, preferred_element_type=f32)
            c_sc[hh, rows] = jnp.broadcast_to(csum[:, 0:1], (nr, LANES))

    n_diag = tq // tk
    unroll = 4 if n_diag % 4 == 0 else 2
    assert n_diag % unroll == 0
    n_full = qi * n_diag
    scores(n_full + n_diag - 1, 1, r0=(n_diag - 1) * tk)
    for d in reversed(range(n_diag)):
        kb = n_full + d
        scores(jnp.maximum(kb - 1, 0), (d + 1) % 2, r0=max(d - 1, 0) * tk)
        update(kb, d % 2, r0=d * tk, nr=tk, triangle=True)
        if d + 1 < n_diag:
            update(kb, d % 2, r0=(d + 1) * tk, nr=tq - (d + 1) * tk)

    def body(n, c):
        top = n_full - 1 - unroll * n
        for u in range(unroll):
            kb = top - u
            scores(jnp.maximum(kb - 1, 0), u % 2)
            update(kb, (u + 1) % 2)
        return c

    lax.fori_loop(0, n_full // unroll, body, 0)
    o_ref[0] = jnp.where(lane_q < HEAD_DIM, acc_sc[0], acc_sc[1]).astype(o_ref.dtype)


def _sb_attn_call(q, k, v, *, tq, tk):
    bsz, s, _ = q.shape
    kern = functools.partial(_sb_attn_kernel, tq=tq, tk=tk)
    return pl.pallas_call(
        kern,
        grid=(bsz, 2, s // tq),
        in_specs=[pl.BlockSpec((1, tq, LANES), lambda b, p, i: (b, i, p)),
                  pl.BlockSpec((1, s, LANES), lambda b, p, i: (b, 0, p)),
                  pl.BlockSpec((1, s, LANES), lambda b, p, i: (b, 0, p))],
        out_specs=pl.BlockSpec((1, tq, LANES), lambda b, p, i: (b, i, p)),
        out_shape=jax.ShapeDtypeStruct((bsz, s, GROUP_W), jnp.bfloat16),
        scratch_shapes=[pltpu.VMEM((2, tq, LANES), jnp.bfloat16),
                        pltpu.VMEM((2, 2, tq, tk), jnp.float32),
                        pltpu.VMEM((2, tq, LANES), jnp.float32),
                        pltpu.VMEM((2, tq, LANES), jnp.float32)],
        compiler_params=_cparams(3),
        name="sb_attn",
    )(q, k, v)


def _out_kernel(of_ref, os_ref, om_ref, ox_ref, gate_ref, h_ref, w_ref, g_ref, b_ref, o_ref, *, alpha):
    f32 = jnp.float32
    mixed = jnp.concatenate([of_ref[...], os_ref[...], om_ref[...], ox_ref[...]], axis=1)
    y = jnp.dot(mixed * gate_ref[...], w_ref[...], preferred_element_type=f32)
    o_ref[...] = _layer_norm_rows(alpha * h_ref[...] + y, g_ref[...], b_ref[...])


def _out_call(o_fox, o_sb, o_mla, o_mem, gate, h2d, w_out_bf16, g, b, alpha, tm):
    n, d = h2d.shape
    gw = o_fox.shape[1]
    kern = functools.partial(_out_kernel, alpha=alpha)
    blk = lambda c: pl.BlockSpec((tm, c), lambda i: (i, 0))
    return pl.pallas_call(
        kern,
        grid=(n // tm,),
        in_specs=[blk(gw), blk(gw), blk(gw), blk(gw), blk(4 * gw), blk(d),
                  pl.BlockSpec(w_out_bf16.shape, lambda i: (0, 0)),
                  pl.BlockSpec((1, d), lambda i: (0, 0)),
                  pl.BlockSpec((1, d), lambda i: (0, 0))],
        out_specs=blk(d),
        out_shape=jax.ShapeDtypeStruct((n, d), jnp.float32),
        compiler_params=_cparams(1),
        name="out_proj",
    )(o_fox, o_sb, o_mla, o_mem, gate, h2d, w_out_bf16, g.reshape(1, d), b.reshape(1, d))


def _pack_w_in(w):
    d = w.shape[0]
    o = 0
    cols = {}
    for name, width in (("fq", 256), ("fk", 256), ("fv", 256), ("fl", 4), ("sq", 256), ("sk", 256),
                        ("sv", 256), ("cq", 256), ("ckv", 128), ("kr", 32), ("mq", 256), ("gate", 1024)):
        cols[name] = w[:, o:o + width]
        o += width
    z = lambda n: jnp.zeros((d, n), w.dtype)
    half = MLA_ROPE // 2
    kr = cols["kr"]
    kr_rot = jnp.concatenate([-kr[:, half:], kr[:, :half]], axis=1)
    krp = jnp.concatenate([z(MLA_NOPE), kr, z(LANES - MLA_NOPE - MLA_ROPE)], axis=1)
    krrp = jnp.concatenate([z(MLA_NOPE), kr_rot, z(LANES - MLA_NOPE - MLA_ROPE)], axis=1)
    misc = []
    for hh in range(N_HEADS):
        misc += [cols["fl"][:, hh:hh + 1]] * FORGET_DUP + [z(FORGET_STRIDE - FORGET_DUP)]
    misc.append(z(LANES - N_HEADS * FORGET_STRIDE))
    packed = jnp.concatenate([cols["gate"], cols["cq"], cols["ckv"]] + misc
                             + [cols["mq"], krp, krrp, cols["fk"], cols["sq"], cols["sk"], cols["sv"]], axis=1)
    assert packed.shape[1] == W_COLS
    w_t = jnp.concatenate([cols["fq"], cols["fv"]], axis=1).T
    assert w_t.shape[0] == WT_ROWS
    return packed.astype(jnp.bfloat16), w_t.astype(jnp.bfloat16)


def _pack_forget_bias(b_forget_l):
    row = jnp.zeros((LANES,), jnp.float32)
    for hh in range(N_HEADS):
        row = row.at[FORGET_STRIDE * hh:FORGET_STRIDE * hh + FORGET_DUP].set(b_forget_l[hh])
    return row.reshape(1, LANES)


def _pack_mla_q_up_t(w):
    r = w.shape[0]
    per = MLA_NOPE + MLA_ROPE
    half = MLA_ROPE // 2
    z = lambda n: jnp.zeros((r, n), w.dtype)
    plain, rot = [], []
    for hh in range(N_HEADS):
        nope = w[:, per * hh:per * hh + MLA_NOPE]
        rope = w[:, per * hh + MLA_NOPE:per * (hh + 1)]
        rope_rot = jnp.concatenate([-rope[:, half:], rope[:, :half]], axis=1)
        plain += [nope, rope, z(LANES - per)]
        rot += [z(MLA_NOPE), rope_rot, z(LANES - per)]
    return jnp.concatenate(plain + rot, axis=1).T.astype(jnp.bfloat16)


def _pack_mla_kv_up(w):
    r = w.shape[0]
    per = MLA_NOPE + HEAD_DIM
    z = jnp.zeros((r, LANES - MLA_NOPE), w.dtype)
    ks, vs = [], []
    for hh in range(N_HEADS):
        ks += [w[:, per * hh:per * hh + MLA_NOPE], z]
        vs.append(w[:, per * hh + MLA_NOPE:per * (hh + 1)])
    return (jnp.concatenate(ks, axis=1).astype(jnp.bfloat16),
            jnp.concatenate(vs, axis=1).T.astype(jnp.bfloat16))


def _rope_tables(s, q_scale):
    half = MLA_ROPE // 2
    inv_freq = ROPE_THETA ** (-jnp.arange(half, dtype=jnp.float32) / half)
    ang = jnp.arange(s).astype(jnp.float32)[:, None] * inv_freq[None, :]
    cos, sin = jnp.cos(ang), jnp.sin(ang)
    ones = jnp.ones((s, MLA_NOPE), jnp.float32)
    z_nope = jnp.zeros((s, MLA_NOPE), jnp.float32)
    z_pad = jnp.zeros((s, LANES - MLA_NOPE - MLA_ROPE), jnp.float32)
    cos_q = jnp.concatenate([ones, cos, cos, z_pad], axis=1) * q_scale
    sin_q = jnp.concatenate([z_nope, sin, sin, z_pad], axis=1) * q_scale
    cos_k = jnp.concatenate([z_nope, cos, cos, z_pad], axis=1)
    sin_k = jnp.concatenate([z_nope, sin, sin, z_pad], axis=1)
    return cos_q.T, sin_q.T, cos_k, sin_k


def kernel(x, mem, ln_in_g, ln_in_b, mem_ln_g, mem_ln_b, w_in, b_forget, mla_q_norm_g, w_mla_q_up,
           mla_kv_norm_g, w_mla_kv_up, w_mem_kv, w_out, ln_g, ln_b):
    bsz, s, d = x.shape
    depth = w_in.shape[0]
    alpha = (2 * depth) ** 0.25
    tm = min(1024, s)
    tm_out = min(1024, s)
    tq_sm = min(2048, s)
    tq_sb = min(1024, s)
    tk_sm = min(512, tq_sm // 2)
    tk_sb = min(256, tq_sb // 2)
    mla_scale = (MLA_NOPE + MLA_ROPE) ** -0.5
    tabs = _rope_tables(s, mla_scale * LOG2E)

    mkv = _mem_kv_call(mem, mem_ln_g, mem_ln_b, w_mem_kv.astype(jnp.bfloat16))

    h = x
    for l in range(depth):
        w_packed, w_t = _pack_w_in(w_in[l])
        w_k_up, w_v_up_t = _pack_mla_kv_up(w_mla_kv_up[l])
        outs = _proj_call(
            h, (ln_in_g, ln_in_b), w_packed, w_t, _pack_forget_bias(b_forget[l]),
            mla_q_norm_g[l].reshape(1, -1), _pack_mla_q_up_t(w_mla_q_up[l]),
            mla_kv_norm_g[l].reshape(1, -1), w_k_up, w_v_up_t,
            tabs, mkv[l], tm, pre_ln=(l == 0))
        if l == 0:
            h, outs = outs[0], outs[1:]
        (fqt, fk, fvt, sbq, sbk, sbv, mqt, mk, mvt, o_mem, gate) = outs
        o_fox = _softmax_attn_call(fqt, fk, fvt, fox=True, tq=tq_sm, tk=tk_sm, name="fox_attn")
        o_sb = _sb_attn_call(sbq, sbk, sbv, tq=tq_sb, tk=tk_sb)
        o_mla = _softmax_attn_call(mqt, mk, mvt, fox=False, tq=tq_sm, tk=tk_sm, name="mla_attn")
        n = bsz * s
        h = _out_call(o_fox.reshape(n, -1), o_sb.reshape(n, -1), o_mla.reshape(n, -1), o_mem.reshape(n, -1),
                      gate.reshape(n, -1), h.reshape(n, d), w_out[l].astype(jnp.bfloat16),
                      ln_g[l], ln_b[l], alpha, tm_out).reshape(bsz, s, d)
    return h
```

```python
import functools

import jax
import jax.numpy as jnp
from jax import lax
from jax.experimental import pallas as pl
from jax.experimental.pallas import tpu as pltpu

N_HEADS = 4
HEAD_DIM = 64
GROUP_W = N_HEADS * HEAD_DIM
MLA_NOPE = 64
MLA_ROPE = 32
MLA_Q_RANK = 256
MLA_KV_RANK = 128
ROPE_THETA = 10000.0
LN_EPS = 1e-5
RMS_EPS = 1e-6
LOG2E = 1.4426950408889634
EXP2_CLAMP = 126.0

LANES = 128
VMEM_LIMIT_BYTES = 56 * 1024 * 1024

C_GATE = 0
C_CQ = 1024
C_CKV = 1280
C_MISC = 1408
C_MQ = 1536
C_KRP = 1792
C_KRRP = 1920
C_FK = 2048
C_SQ, C_SK, C_SV = 2304, 2560, 2816
W_COLS = 3072
R_FQ, R_FV = 0, 256
WT_ROWS = 512
FORGET_DUP = 6
FORGET_STRIDE = 8
VT_ROWS = HEAD_DIM + 16
PROJ_SUB_ROWS = 512
NT_DIMS = (((1,), (1,)), ((), ()))


def _cparams(n_grid):
    return pltpu.CompilerParams(dimension_semantics=("arbitrary",) * n_grid,
                                vmem_limit_bytes=VMEM_LIMIT_BYTES)


def _split3(x):
    p0 = x.astype(jnp.bfloat16).astype(jnp.float32)
    r1 = x - p0
    p1 = r1.astype(jnp.bfloat16).astype(jnp.float32)
    p2 = (r1 - p1).astype(jnp.bfloat16).astype(jnp.float32)
    return p0, p1, p2


def _layer_norm_rows(x, g, b):
    mu = jnp.mean(x, axis=-1, keepdims=True)
    xc = x - mu
    var = jnp.mean(xc * xc, axis=-1, keepdims=True)
    return xc * lax.rsqrt(var + LN_EPS) * g + b


def _rms_norm_rows(x, g):
    ms = jnp.mean(x * x, axis=-1, keepdims=True)
    return x * lax.rsqrt(ms + RMS_EPS) * g


def _mem_kv_kernel(mem_ref, g_ref, b_ref, w_ref, o_ref):
    mem_n = _layer_norm_rows(mem_ref[0], g_ref[...], b_ref[...]).astype(jnp.bfloat16)
    for l in range(w_ref.shape[0]):
        o_ref[l, 0] = jnp.dot(mem_n, w_ref[l], preferred_element_type=jnp.float32).astype(jnp.bfloat16)


def _mem_kv_call(mem, g, b, w_mem_kv_bf16):
    bsz, m, d = mem.shape
    depth, _, c = w_mem_kv_bf16.shape
    return pl.pallas_call(
        _mem_kv_kernel,
        grid=(bsz,),
        in_specs=[pl.BlockSpec((1, m, d), lambda i: (i, 0, 0)),
                  pl.BlockSpec((1, d), lambda i: (0, 0)),
                  pl.BlockSpec((1, d), lambda i: (0, 0)),
                  pl.BlockSpec((depth, d, c), lambda i: (0, 0, 0))],
        out_specs=pl.BlockSpec((depth, 1, m, c), lambda i: (0, i, 0, 0)),
        out_shape=jax.ShapeDtypeStruct((depth, bsz, m, c), jnp.bfloat16),
        compiler_params=_cparams(1),
        name="mem_kv",
    )(mem, g.reshape(1, d), b.reshape(1, d), w_mem_kv_bf16)


def _proj_kernel(h_ref, lng_ref, lnb_ref, w_ref, wt_ref, bfor_ref, gq_ref, wqupt_ref, gkv_ref, wkup_ref,
                 wvupt_ref, cqt_ref, sqt_ref, ck_ref, sk_ref, mkv_ref, *rest, pre_ln):
    hout_ref = rest[0] if pre_ln else None
    (fqt_ref, fk_ref, fvt_ref, sbq_ref, sbk_ref, sbv_ref,
     mqt_ref, mk_ref, mvt_ref, omem_ref, gate_ref, carry_ref) = rest[1:] if pre_ln else rest
    i = pl.program_id(1)
    tm = h_ref.shape[1]
    bf16, f32 = jnp.bfloat16, jnp.float32
    head_c = (HEAD_DIM ** -0.5) * LOG2E

    @pl.when(i == 0)
    def _():
        carry_ref[...] = jnp.zeros_like(carry_ref)

    def sub_block(r0, nr):
        rs = slice(r0, r0 + nr)
        h = h_ref[0, rs]
        if pre_ln:
            h = _layer_norm_rows(h, lng_ref[...], lnb_ref[...])
            hout_ref[0, rs] = h
        hb = h.astype(bf16)
        proj = jnp.dot(hb, w_ref[...], preferred_element_type=f32)
        proj_t = lax.dot_general(wt_ref[...], hb, NT_DIMS, preferred_element_type=f32)

        def cols(c0, n):
            return proj[:, c0:c0 + n]

        def store_values_t(vt, out_ref):
            for hh in range(N_HEADS):
                out_ref[0, hh, :HEAD_DIM, rs] = vt[HEAD_DIM * hh:HEAD_DIM * (hh + 1), :].astype(bf16)
                out_ref[0, hh, HEAD_DIM:, rs] = jnp.ones((VT_ROWS - HEAD_DIM, nr), bf16)

        lane = lax.broadcasted_iota(jnp.int32, (nr, LANES), 1)
        sub = lane % FORGET_STRIDE
        used = (lane < N_HEADS * FORGET_STRIDE) & (sub < FORGET_DUP)
        xf = cols(C_MISC, LANES) + bfor_ref[...]
        log_f = jnp.minimum(xf, 0.0) - jnp.log1p(jnp.exp(-jnp.abs(xf)))
        log_f = jnp.where(used, log_f, 0.0)
        row = lax.broadcasted_iota(jnp.int32, (nr, nr), 0)
        col = lax.broadcasted_iota(jnp.int32, (nr, nr), 1)
        tril = jnp.where(col <= row, 1.0, 0.0).astype(bf16)
        parts = jnp.concatenate([p.astype(bf16) for p in _split3(log_f)], axis=1)
        csum = jnp.dot(tril, parts, preferred_element_type=f32)
        f_cum = (csum[:, :LANES] + csum[:, LANES:2 * LANES]) + csum[:, 2 * LANES:] + carry_ref[...]
        carry_ref[...] = f_cum[nr - 1:nr, :]
        p0, p1, p2 = _split3(f_cum * LOG2E)
        bias_q = jnp.where(sub == 0, p0, jnp.where(sub == 1, p1, jnp.where(sub == 2, p2, 1.0)))
        bias_q_t = jnp.where(used, bias_q, 0.0).T.astype(bf16)
        bias_k = jnp.where(sub == 3, -p0, jnp.where(sub == 4, -p1, jnp.where(sub == 5, -p2, 1.0)))
        bias_k = jnp.where(used, bias_k, 0.0).astype(bf16)

        fk = cols(C_FK, GROUP_W)
        for p in range(2):
            fqt_ref[0, p, :LANES, rs] = (proj_t[R_FQ + LANES * p:R_FQ + LANES * (p + 1), :] * head_c).astype(bf16)
            fqt_ref[0, p, LANES:, rs] = bias_q_t
            fk_ref[0, rs, 2 * LANES * p:2 * LANES * p + LANES] = fk[:, LANES * p:LANES * (p + 1)].astype(bf16)
            fk_ref[0, rs, 2 * LANES * p + LANES:2 * LANES * (p + 1)] = bias_k
        store_values_t(proj_t[R_FV:R_FV + GROUP_W, :], fvt_ref)

        sbq_ref[0, rs] = (cols(C_SQ, GROUP_W) * head_c).astype(bf16)
        sbk_ref[0, rs] = cols(C_SK, GROUP_W).astype(bf16)
        sbv_ref[0, rs] = cols(C_SV, GROUP_W).astype(bf16)

        cqn = _rms_norm_rows(cols(C_CQ, MLA_Q_RANK), gq_ref[...]).astype(bf16)
        q_up_t = lax.dot_general(wqupt_ref[...], cqn, NT_DIMS, preferred_element_type=f32)
        ckvn = _rms_norm_rows(cols(C_CKV, MLA_KV_RANK), gkv_ref[...]).astype(bf16)
        k_up = jnp.dot(ckvn, wkup_ref[...], preferred_element_type=f32)
        k_rope = cols(C_KRP, LANES) * ck_ref[rs] + cols(C_KRRP, LANES) * sk_ref[rs]
        nq = N_HEADS * LANES
        for hh in range(N_HEADS):
            sl = slice(LANES * hh, LANES * (hh + 1))
            mqt_ref[0, hh, :, rs] = (q_up_t[sl, :] * cqt_ref[:, rs]
                                     + q_up_t[nq + LANES * hh:nq + LANES * (hh + 1), :] * sqt_ref[:, rs]).astype(bf16)
            mk_ref[0, rs, sl] = (k_up[:, sl] + k_rope).astype(bf16)
        store_values_t(lax.dot_general(wvupt_ref[...], ckvn, NT_DIMS, preferred_element_type=f32), mvt_ref)

        mem_q = cols(C_MQ, GROUP_W) * head_c
        lane_h = lax.broadcasted_iota(jnp.int32, (nr, LANES), 1)
        for p in range(2):
            qp = mem_q[:, LANES * p:LANES * (p + 1)]
            kp = mkv_ref[0, :, LANES * p:LANES * (p + 1)]
            vp = mkv_ref[0, :, GROUP_W + LANES * p:GROUP_W + LANES * (p + 1)]
            outs = []
            for hh in range(2):
                in_head = (lane_h >= HEAD_DIM * hh) & (lane_h < HEAD_DIM * (hh + 1))
                qh = jnp.where(in_head, qp, 0.0).astype(bf16)
                s = lax.dot_general(qh, kp, NT_DIMS, preferred_element_type=f32)
                m = jnp.max(s, axis=-1, keepdims=True)
                e = jnp.exp2(s - m)
                pr = e / jnp.sum(e, axis=-1, keepdims=True)
                outs.append(jnp.dot(pr.astype(bf16), vp, preferred_element_type=f32))
            omem_ref[0, rs, LANES * p:LANES * (p + 1)] = jnp.where(lane_h < HEAD_DIM, outs[0], outs[1]).astype(bf16)

        for c in range(4):
            g = cols(C_GATE + GROUP_W * c, GROUP_W)
            gate_ref[0, rs, GROUP_W * c:GROUP_W * (c + 1)] = (g / (1.0 + jnp.exp(-g))).astype(bf16)

    n_sub = max(tm // PROJ_SUB_ROWS, 1)
    for j in range(n_sub):
        sub_block(j * (tm // n_sub), tm // n_sub)


def _proj_call(h, ln_gb, w_packed, w_t, bfor_row, gq, wqupt, gkv, wkup, wvupt, tabs, mkv_l, tm, pre_ln):
    bsz, s, d = h.shape
    ln_g, ln_b = ln_gb
    m = mkv_l.shape[1]
    cqt, sqt, ck, sk = tabs
    bf16 = jnp.bfloat16
    full2 = lambda shape: pl.BlockSpec(shape, lambda b, i: (0, 0))
    row_blk = lambda c: pl.BlockSpec((1, tm, c), lambda b, i: (b, i, 0))
    tab_blk = pl.BlockSpec((tm, LANES), lambda b, i: (i, 0))
    tab_t_blk = pl.BlockSpec((LANES, tm), lambda b, i: (0, i))
    rows = lambda c, dt: (row_blk(c), jax.ShapeDtypeStruct((bsz, s, c), dt))
    feat = lambda n, r: (pl.BlockSpec((1, n, r, tm), lambda b, i: (b, 0, 0, i)),
                         jax.ShapeDtypeStruct((bsz, n, r, s), bf16))
    outs = [feat(2, 2 * LANES), rows(2 * GROUP_W, bf16), feat(N_HEADS, VT_ROWS),
            rows(GROUP_W, bf16), rows(GROUP_W, bf16), rows(GROUP_W, bf16),
            feat(N_HEADS, LANES), rows(2 * GROUP_W, bf16), feat(N_HEADS, VT_ROWS),
            rows(GROUP_W, bf16), rows(4 * GROUP_W, bf16)]
    if pre_ln:
        outs = [rows(d, jnp.float32)] + outs
    return pl.pallas_call(
        functools.partial(_proj_kernel, pre_ln=pre_ln),
        grid=(bsz, s // tm),
        in_specs=[row_blk(d), full2((1, d)), full2((1, d)),
                  full2(w_packed.shape), full2(w_t.shape),
                  full2((1, LANES)),
                  full2((1, MLA_Q_RANK)), full2(wqupt.shape),
                  full2((1, MLA_KV_RANK)), full2(wkup.shape), full2(wvupt.shape),
                  tab_t_blk, tab_t_blk, tab_blk, tab_blk,
                  pl.BlockSpec((1, m, 2 * GROUP_W), lambda b, i: (b, 0, 0))],
        out_specs=[spec for spec, _ in outs],
        out_shape=[shape for _, shape in outs],
        scratch_shapes=[pltpu.VMEM((1, LANES), jnp.float32)],
        compiler_params=_cparams(2),
        name="proj",
    )(h, ln_g.reshape(1, d), ln_b.reshape(1, d), w_packed, w_t, bfor_row, gq, wqupt, gkv, wkup, wvupt,
      cqt, sqt, ck, sk, mkv_l)


def _softmax_attn_kernel(qt_ref, k_ref, vt_ref, o_ref, q_sc, s_sc, m_sc, acc_sc, *, fox, tq, tk):
    pair = pl.program_id(1)
    qi = pl.program_id(2)
    bf16, f32 = jnp.bfloat16, jnp.float32
    kq = q_sc.shape[1]
    for hh in range(2):
        if fox:
            feat = lax.broadcasted_iota(jnp.int32, (kq, tq), 0)
            lo = LANES + FORGET_STRIDE * (2 * pair + hh)
            keep = ((feat >= HEAD_DIM * hh) & (feat < HEAD_DIM * (hh + 1))) | (
                (feat >= lo) & (feat < lo + FORGET_STRIDE))
            q_sc[hh] = jnp.where(keep, qt_ref[0, 0], jnp.zeros((kq, tq), bf16))
        else:
            q_sc[hh] = qt_ref[0, hh]
    m_sc[...] = jnp.full(m_sc.shape, -jnp.inf, f32)
    acc_sc[...] = jnp.zeros(acc_sc.shape, f32)

    def scores(kb, slot, c0=0):
        k0 = pl.multiple_of(kb * tk, tk)
        for hh in range(2):
            k = k_ref[0, pl.ds(k0, tk), :] if fox else k_ref[0, pl.ds(k0, tk), LANES * hh:LANES * (hh + 1)]
            s_sc[slot, hh, :, c0:] = jnp.dot(k, q_sc[hh, :, c0:], preferred_element_type=f32)

    def update(kb, slot, c0=0, nc=tq, triangle=False):
        k0 = pl.multiple_of(kb * tk, tk)
        qs = slice(c0, c0 + nc)
        if triangle:
            causal = (lax.broadcasted_iota(jnp.int32, (tk, nc), 0)
                      <= lax.broadcasted_iota(jnp.int32, (tk, nc), 1))
        for hh in range(2):
            s = s_sc[slot, hh, :, qs]
            if triangle:
                s = jnp.where(causal, s, -jnp.inf)
            m_prev = m_sc[hh, :, qs]
            m_new = jnp.maximum(m_prev, jnp.max(s, axis=0, keepdims=True))
            alpha = jnp.exp2(m_prev - m_new)
            p = jnp.exp2(s - m_new)
            acc_sc[hh, :, qs] = (alpha * acc_sc[hh, :, qs]
                                 + jnp.dot(vt_ref[0, hh, :, pl.ds(k0, tk)], p.astype(bf16),
                                           preferred_element_type=f32))
            m_sc[hh, :, qs] = m_new

    n_diag = tq // tk
    unroll = 4 if n_diag % 4 == 0 else 2
    assert n_diag % unroll == 0
    n_full = qi * n_diag
    scores(0, 0)

    def body(j, c):
        for u in range(unroll):
            kb = unroll * j + u
            scores(kb + 1, (u + 1) % 2)
            update(kb, u % 2)
        return c

    lax.fori_loop(0, n_full // unroll, body, 0)
    for d in range(n_diag):
        if d + 1 < n_diag:
            scores(n_full + d + 1, (d + 1) % 2, c0=(d + 1) * tk)
        update(n_full + d, d % 2, c0=d * tk, nc=tk, triangle=True)
        if d + 1 < n_diag:
            update(n_full + d, d % 2, c0=(d + 1) * tk, nc=tq - (d + 1) * tk)
    out_t = jnp.concatenate([acc_sc[hh, :HEAD_DIM] / acc_sc[hh, HEAD_DIM:HEAD_DIM + 1] for hh in range(2)],
                            axis=0)
    o_ref[0] = out_t.T.astype(o_ref.dtype)


def _softmax_attn_call(qt, k, vt, *, fox, tq, tk, name):
    bsz, nqt, kq, s = qt.shape
    kw = 2 * LANES
    kern = functools.partial(_softmax_attn_kernel, fox=fox, tq=tq, tk=tk)
    return pl.pallas_call(
        kern,
        grid=(bsz, 2, s // tq),
        in_specs=[pl.BlockSpec((1, nqt // 2, kq, tq), lambda b, p, i: (b, p, 0, i)),
                  pl.BlockSpec((1, s, kw), lambda b, p, i: (b, 0, p)),
                  pl.BlockSpec((1, 2, VT_ROWS, s), lambda b, p, i: (b, p, 0, 0))],
        out_specs=pl.BlockSpec((1, tq, LANES), lambda b, p, i: (b, i, p)),
        out_shape=jax.ShapeDtypeStruct((bsz, s, GROUP_W), jnp.bfloat16),
        scratch_shapes=[pltpu.VMEM((2, kq, tq), jnp.bfloat16),
                        pltpu.VMEM((2, 2, tk, tq), jnp.float32),
                        pltpu.VMEM((2, 1, tq), jnp.float32),
                        pltpu.VMEM((2, VT_ROWS, tq), jnp.float32)],
        compiler_params=_cparams(3),
        name=name,
    )(qt, k, vt)


def _sb_attn_kernel(q_ref, k_ref, v_ref, o_ref, q_sc, z_sc, c_sc, acc_sc, *, tq, tk):
    qi = pl.program_id(2)
    bf16, f32 = jnp.bfloat16, jnp.float32
    lane_q = lax.broadcasted_iota(jnp.int32, (tq, LANES), 1)
    rj = lax.broadcasted_iota(jnp.int32, (tk, tk), 0)
    cs = lax.broadcasted_iota(jnp.int32, (tk, tk), 1)
    upper = jnp.where(rj >= cs, 1.0, 0.0).astype(bf16)
    for hh in range(2):
        in_head = (lane_q >= HEAD_DIM * hh) & (lane_q < HEAD_DIM * (hh + 1))
        q_sc[hh] = jnp.where(in_head, q_ref[0], jnp.zeros((tq, LANES), bf16))
    c_sc[...] = jnp.zeros(c_sc.shape, f32)
    acc_sc[...] = jnp.zeros(acc_sc.shape, f32)

    def scores(kb, slot, r0=0):
        k0 = pl.multiple_of(kb * tk, tk)
        k = k_ref[0, pl.ds(k0, tk), :]
        for hh in range(2):
            z_sc[slot, hh, r0:] = lax.dot_general(q_sc[hh, r0:], k, NT_DIMS, preferred_element_type=f32)

    def update(kb, slot, r0=0, nr=tq, triangle=False):
        k0 = pl.multiple_of(kb * tk, tk)
        v = v_ref[0, pl.ds(k0, tk), :]
        rows = slice(r0, r0 + nr)
        if triangle:
            valid = (lax.broadcasted_iota(jnp.int32, (nr, tk), 1)
                     < lax.broadcasted_iota(jnp.int32, (nr, tk), 0))
        for hh in range(2):
            z = z_sc[slot, hh, rows]
            sp = jnp.maximum(jnp.log2(1.0 + jnp.exp2(jnp.minimum(z, EXP2_CLAMP))), z)
            if triangle:
                sp = jnp.where(valid, sp, 0.0)
            csum = (jnp.dot(sp.astype(bf16), upper, preferred_element_type=f32)
                    + jnp.tile(c_sc[hh, rows], (1, tk // LANES)))
            w = jnp.exp2(z - csum)
            if triangle:
                w = jnp.where(valid, w, 0.0)
            acc_sc[hh, rows] += jnp.dot(w.astype(bf16), v, preferred_element_type=f32)
            c_sc[hh, rows] = jnp.broadcast_to(csum[:, 0:1], (nr, LANES))

    n_diag = tq // tk
    unroll = 4 if n_diag % 4 == 0 else 2
    assert n_diag % unroll == 0
    n_full = qi * n_diag
    scores(n_full + n_diag - 1, 1, r0=(n_diag - 1) * tk)
    for d in reversed(range(n_diag)):
        kb = n_full + d
        scores(jnp.maximum(kb - 1, 0), (d + 1) % 2, r0=max(d - 1, 0) * tk)
        update(kb, d % 2, r0=d * tk, nr=tk, triangle=True)
        if d + 1 < n_diag:
            update(kb, d % 2, r0=(d + 1) * tk, nr=tq - (d + 1) * tk)

    def body(n, c):
        top = n_full - 1 - unroll * n
        for u in range(unroll):
            kb = top - u
            scores(jnp.maximum(kb - 1, 0), u % 2)
            update(kb, (u + 1) % 2)
        return c

    lax.fori_loop(0, n_full // unroll, body, 0)
    o_ref[0] = jnp.where(lane_q < HEAD_DIM, acc_sc[0], acc_sc[1]).astype(o_ref.dtype)


def _sb_attn_call(q, k, v, *, tq, tk):
    bsz, s, _ = q.shape
    kern = functools.partial(_sb_attn_kernel, tq=tq, tk=tk)
    return pl.pallas_call(
        kern,
        grid=(bsz, 2, s // tq),
        in_specs=[pl.BlockSpec((1, tq, LANES), lambda b, p, i: (b, i, p)),
                  pl.BlockSpec((1, s, LANES), lambda b, p, i: (b, 0, p)),
                  pl.BlockSpec((1, s, LANES), lambda b, p, i: (b, 0, p))],
        out_specs=pl.BlockSpec((1, tq, LANES), lambda b, p, i: (b, i, p)),
        out_shape=jax.ShapeDtypeStruct((bsz, s, GROUP_W), jnp.bfloat16),
        scratch_shapes=[pltpu.VMEM((2, tq, LANES), jnp.bfloat16),
                        pltpu.VMEM((2, 2, tq, tk), jnp.float32),
                        pltpu.VMEM((2, tq, LANES), jnp.float32),
                        pltpu.VMEM((2, tq, LANES), jnp.float32)],
        compiler_params=_cparams(3),
        name="sb_attn",
    )(q, k, v)


def _out_kernel(of_ref, os_ref, om_ref, ox_ref, gate_ref, h_ref, w_ref, g_ref, b_ref, o_ref, *, alpha):
    f32 = jnp.float32
    mixed = jnp.concatenate([of_ref[...], os_ref[...], om_ref[...], ox_ref[...]], axis=1)
    y = jnp.dot(mixed * gate_ref[...], w_ref[...], preferred_element_type=f32)
    o_ref[...] = _layer_norm_rows(alpha * h_ref[...] + y, g_ref[...], b_ref[...])


def _out_call(o_fox, o_sb, o_mla, o_mem, gate, h2d, w_out_bf16, g, b, alpha, tm):
    n, d = h2d.shape
    gw = o_fox.shape[1]
    kern = functools.partial(_out_kernel, alpha=alpha)
    blk = lambda c: pl.BlockSpec((tm, c), lambda i: (i, 0))
    return pl.pallas_call(
        kern,
        grid=(n // tm,),
        in_specs=[blk(gw), blk(gw), blk(gw), blk(gw), blk(4 * gw), blk(d),
                  pl.BlockSpec(w_out_bf16.shape, lambda i: (0, 0)),
                  pl.BlockSpec((1, d), lambda i: (0, 0)),
                  pl.BlockSpec((1, d), lambda i: (0, 0))],
        out_specs=blk(d),
        out_shape=jax.ShapeDtypeStruct((n, d), jnp.float32),
        compiler_params=_cparams(1),
        name="out_proj",
    )(o_fox, o_sb, o_mla, o_mem, gate, h2d, w_out_bf16, g.reshape(1, d), b.reshape(1, d))


def _pack_w_in(w):
    d = w.shape[0]
    o = 0
    cols = {}
    for name, width in (("fq", 256), ("fk", 256), ("fv", 256), ("fl", 4), ("sq", 256), ("sk", 256),
                        ("sv", 256), ("cq", 256), ("ckv", 128), ("kr", 32), ("mq", 256), ("gate", 1024)):
        cols[name] = w[:, o:o + width]
        o += width
    z = lambda n: jnp.zeros((d, n), w.dtype)
    half = MLA_ROPE // 2
    kr = cols["kr"]
    kr_rot = jnp.concatenate([-kr[:, half:], kr[:, :half]], axis=1)
    krp = jnp.concatenate([z(MLA_NOPE), kr, z(LANES - MLA_NOPE - MLA_ROPE)], axis=1)
    krrp = jnp.concatenate([z(MLA_NOPE), kr_rot, z(LANES - MLA_NOPE - MLA_ROPE)], axis=1)
    misc = []
    for hh in range(N_HEADS):
        misc += [cols["fl"][:, hh:hh + 1]] * FORGET_DUP + [z(FORGET_STRIDE - FORGET_DUP)]
    misc.append(z(LANES - N_HEADS * FORGET_STRIDE))
    packed = jnp.concatenate([cols["gate"], cols["cq"], cols["ckv"]] + misc
                             + [cols["mq"], krp, krrp, cols["fk"], cols["sq"], cols["sk"], cols["sv"]], axis=1)
    assert packed.shape[1] == W_COLS
    w_t = jnp.concatenate([cols["fq"], cols["fv"]], axis=1).T
    assert w_t.shape[0] == WT_ROWS
    return packed.astype(jnp.bfloat16), w_t.astype(jnp.bfloat16)


def _pack_forget_bias(b_forget_l):
    row = jnp.zeros((LANES,), jnp.float32)
    for hh in range(N_HEADS):
        row = row.at[FORGET_STRIDE * hh:FORGET_STRIDE * hh + FORGET_DUP].set(b_forget_l[hh])
    return row.reshape(1, LANES)


def _pack_mla_q_up_t(w):
    r = w.shape[0]
    per = MLA_NOPE + MLA_ROPE
    half = MLA_ROPE // 2
    z = lambda n: jnp.zeros((r, n), w.dtype)
    plain, rot = [], []
    for hh in range(N_HEADS):
        nope = w[:, per * hh:per * hh + MLA_NOPE]
        rope = w[:, per * hh + MLA_NOPE:per * (hh + 1)]
        rope_rot = jnp.concatenate([-rope[:, half:], rope[:, :half]], axis=1)
        plain += [nope, rope, z(LANES - per)]
        rot += [z(MLA_NOPE), rope_rot, z(LANES - per)]
    return jnp.concatenate(plain + rot, axis=1).T.astype(jnp.bfloat16)


def _pack_mla_kv_up(w):
    r = w.shape[0]
    per = MLA_NOPE + HEAD_DIM
    z = jnp.zeros((r, LANES - MLA_NOPE), w.dtype)
    ks, vs = [], []
    for hh in range(N_HEADS):
        ks += [w[:, per * hh:per * hh + MLA_NOPE], z]
        vs.append(w[:, per * hh + MLA_NOPE:per * (hh + 1)])
    return (jnp.concatenate(ks, axis=1).astype(jnp.bfloat16),
            jnp.concatenate(vs, axis=1).T.astype(jnp.bfloat16))


def _rope_tables(s, q_scale):
    half = MLA_ROPE // 2
    inv_freq = ROPE_THETA ** (-jnp.arange(half, dtype=jnp.float32) / half)
    ang = jnp.arange(s).astype(jnp.float32)[:, None] * inv_freq[None, :]
    cos, sin = jnp.cos(ang), jnp.sin(ang)
    ones = jnp.ones((s, MLA_NOPE), jnp.float32)
    z_nope = jnp.zeros((s, MLA_NOPE), jnp.float32)
    z_pad = jnp.zeros((s, LANES - MLA_NOPE - MLA_ROPE), jnp.float32)
    cos_q = jnp.concatenate([ones, cos, cos, z_pad], axis=1) * q_scale
    sin_q = jnp.concatenate([z_nope, sin, sin, z_pad], axis=1) * q_scale
    cos_k = jnp.concatenate([z_nope, cos, cos, z_pad], axis=1)
    sin_k = jnp.concatenate([z_nope, sin, sin, z_pad], axis=1)
    return cos_q.T, sin_q.T, cos_k, sin_k


def kernel(x, mem, ln_in_g, ln_in_b, mem_ln_g, mem_ln_b, w_in, b_forget, mla_q_norm_g, w_mla_q_up,
           mla_kv_norm_g, w_mla_kv_up, w_mem_kv, w_out, ln_g, ln_b):
    bsz, s, d = x.shape
    depth = w_in.shape[0]
    alpha = (2 * depth) ** 0.25
    tm = min(1024, s)
    tm_out = min(1024, s)
    tq_sm = min(2048, s)
    tq_sb = min(1024, s)
    tk_sm = min(512, tq_sm // 2)
    tk_sb = min(256, tq_sb // 2)
    mla_scale = (MLA_NOPE + MLA_ROPE) ** -0.5
    tabs = _rope_tables(s, mla_scale * LOG2E)

    mkv = _mem_kv_call(mem, mem_ln_g, mem_ln_b, w_mem_kv.astype(jnp.bfloat16))

    h = x
    for l in range(depth):
        w_packed, w_t = _pack_w_in(w_in[l])
        w_k_up, w_v_up_t = _pack_mla_kv_up(w_mla_kv_up[l])
        outs = _proj_call(
            h, (ln_in_g, ln_in_b), w_packed, w_t, _pack_forget_bias(b_forget[l]),
            mla_q_norm_g[l].reshape(1, -1), _pack_mla_q_up_t(w_mla_q_up[l]),
            mla_kv_norm_g[l].reshape(1, -1), w_k_up, w_v_up_t,
            tabs, mkv[l], tm, pre_ln=(l == 0))
        if l == 0:
            h, outs = outs[0], outs[1:]
        (fqt, fk, fvt, sbq, sbk, sbv, mqt, mk, mvt, o_mem, gate) = outs
        o_fox = _softmax_attn_call(fqt, fk, fvt, fox=True, tq=tq_sm, tk=tk_sm, name="fox_attn")
        o_sb = _sb_attn_call(sbq, sbk, sbv, tq=tq_sb, tk=tk_sb)
        o_mla = _softmax_attn_call(mqt, mk, mvt, fox=False, tq=tq_sm, tk=tk_sm, name="mla_attn")
        n = bsz * s
        h = _out_call(o_fox.reshape(n, -1), o_sb.reshape(n, -1), o_mla.reshape(n, -1), o_mem.reshape(n, -1),
                      gate.reshape(n, -1), h.reshape(n, d), w_out[l].astype(jnp.bfloat16),
                      ln_g[l], ln_b[l], alpha, tm_out).reshape(bsz, s, d)
    return h
```

```python
import functools

import jax
import jax.numpy as jnp
from jax import lax
from jax.experimental import pallas as pl
from jax.experimental.pallas import tpu as pltpu

N_HEADS = 4
HEAD_DIM = 64
GROUP_W = N_HEADS * HEAD_DIM
MLA_NOPE = 64
MLA_ROPE = 32
MLA_Q_RANK = 256
MLA_KV_RANK = 128
ROPE_THETA = 10000.0
LN_EPS = 1e-5
RMS_EPS = 1e-6
LOG2E = 1.4426950408889634
EXP2_CLAMP = 126.0

LANES = 128
VMEM_LIMIT_BYTES = 56 * 1024 * 1024

C_GATE = 0
C_CQ = 1024
C_CKV = 1280
C_MISC = 1408
C_MQ = 1536
C_KRP = 1792
C_KRRP = 1920
C_FK = 2048
C_SQ, C_SK, C_SV = 2304, 2560, 2816
W_COLS = 3072
R_FQ, R_FV = 0, 256
WT_ROWS = 512
FORGET_DUP = 6
FORGET_STRIDE = 8
VT_ROWS = HEAD_DIM + 16
PROJ_SUB_ROWS = 512
NT_DIMS = (((1,), (1,)), ((), ()))


def _cparams(n_grid):
    return pltpu.CompilerParams(dimension_semantics=("arbitrary",) * n_grid,
                                vmem_limit_bytes=VMEM_LIMIT_BYTES)


def _split3(x):
    p0 = x.astype(jnp.bfloat16).astype(jnp.float32)
    r1 = x - p0
    p1 = r1.astype(jnp.bfloat16).astype(jnp.float32)
    p2 = (r1 - p1).astype(jnp.bfloat16).astype(jnp.float32)
    return p0, p1, p2


def _layer_norm_rows(x, g, b):
    mu = jnp.mean(x, axis=-1, keepdims=True)
    xc = x - mu
    var = jnp.mean(xc * xc, axis=-1, keepdims=True)
    return xc * lax.rsqrt(var + LN_EPS) * g + b


def _rms_norm_rows(x, g):
    ms = jnp.mean(x * x, axis=-1, keepdims=True)
    return x * lax.rsqrt(ms + RMS_EPS) * g


def _mem_kv_kernel(mem_ref, g_ref, b_ref, w_ref, o_ref):
    mem_n = _layer_norm_rows(mem_ref[0], g_ref[...], b_ref[...]).astype(jnp.bfloat16)
    for l in range(w_ref.shape[0]):
        o_ref[l, 0] = jnp.dot(mem_n, w_ref[l], preferred_element_type=jnp.float32).astype(jnp.bfloat16)


def _mem_kv_call(mem, g, b, w_mem_kv_bf16):
    bsz, m, d = mem.shape
    depth, _, c = w_mem_kv_bf16.shape
    return pl.pallas_call(
        _mem_kv_kernel,
        grid=(bsz,),
        in_specs=[pl.BlockSpec((1, m, d), lambda i: (i, 0, 0)),
                  pl.BlockSpec((1, d), lambda i: (0, 0)),
                  pl.BlockSpec((1, d), lambda i: (0, 0)),
                  pl.BlockSpec((depth, d, c), lambda i: (0, 0, 0))],
        out_specs=pl.BlockSpec((depth, 1, m, c), lambda i: (0, i, 0, 0)),
        out_shape=jax.ShapeDtypeStruct((depth, bsz, m, c), jnp.bfloat16),
        compiler_params=_cparams(1),
        name="mem_kv",
    )(mem, g.reshape(1, d), b.reshape(1, d), w_mem_kv_bf16)


def _proj_kernel(h_ref, lng_ref, lnb_ref, w_ref, wt_ref, bfor_ref, gq_ref, wqupt_ref, gkv_ref, wkup_ref,
                 wvupt_ref, cqt_ref, sqt_ref, ck_ref, sk_ref, mkv_ref, *rest, pre_ln):
    hout_ref = rest[0] if pre_ln else None
    (fqt_ref, fk_ref, fvt_ref, sbq_ref, sbk_ref, sbv_ref,
     mqt_ref, mk_ref, mvt_ref, omem_ref, gate_ref, carry_ref) = rest[1:] if pre_ln else rest
    i = pl.program_id(1)
    tm = h_ref.shape[1]
    bf16, f32 = jnp.bfloat16, jnp.float32
    head_c = (HEAD_DIM ** -0.5) * LOG2E

    @pl.when(i == 0)
    def _():
        carry_ref[...] = jnp.zeros_like(carry_ref)

    def sub_block(r0, nr):
        rs = slice(r0, r0 + nr)
        h = h_ref[0, rs]
        if pre_ln:
            h = _layer_norm_rows(h, lng_ref[...], lnb_ref[...])
            hout_ref[0, rs] = h
        hb = h.astype(bf16)
        proj = jnp.dot(hb, w_ref[...], preferred_element_type=f32)
        proj_t = lax.dot_general(wt_ref[...], hb, NT_DIMS, preferred_element_type=f32)

        def cols(c0, n):
            return proj[:, c0:c0 + n]

        def store_values_t(vt, out_ref):
            for hh in range(N_HEADS):
                out_ref[0, hh, :HEAD_DIM, rs] = vt[HEAD_DIM * hh:HEAD_DIM * (hh + 1), :].astype(bf16)
                out_ref[0, hh, HEAD_DIM:, rs] = jnp.ones((VT_ROWS - HEAD_DIM, nr), bf16)

        lane = lax.broadcasted_iota(jnp.int32, (nr, LANES), 1)
        sub = lane % FORGET_STRIDE
        used = (lane < N_HEADS * FORGET_STRIDE) & (sub < FORGET_DUP)
        xf = cols(C_MISC, LANES) + bfor_ref[...]
        log_f = jnp.minimum(xf, 0.0) - jnp.log1p(jnp.exp(-jnp.abs(xf)))
        log_f = jnp.where(used, log_f, 0.0)
        row = lax.broadcasted_iota(jnp.int32, (nr, nr), 0)
        col = lax.broadcasted_iota(jnp.int32, (nr, nr), 1)
        tril = jnp.where(col <= row, 1.0, 0.0).astype(bf16)
        parts = jnp.concatenate([p.astype(bf16) for p in _split3(log_f)], axis=1)
        csum = jnp.dot(tril, parts, preferred_element_type=f32)
        f_cum = (csum[:, :LANES] + csum[:, LANES:2 * LANES]) + csum[:, 2 * LANES:] + carry_ref[...]
        carry_ref[...] = f_cum[nr - 1:nr, :]
        p0, p1, p2 = _split3(f_cum * LOG2E)
        bias_q = jnp.where(sub == 0, p0, jnp.where(sub == 1, p1, jnp.where(sub == 2, p2, 1.0)))
        bias_q_t = jnp.where(used, bias_q, 0.0).T.astype(bf16)
        bias_k = jnp.where(sub == 3, -p0, jnp.where(sub == 4, -p1, jnp.where(sub == 5, -p2, 1.0)))
        bias_k = jnp.where(used, bias_k, 0.0).astype(bf16)

        fk = cols(C_FK, GROUP_W)
        for p in range(2):
            fqt_ref[0, p, :LANES, rs] = (proj_t[R_FQ + LANES * p:R_FQ + LANES * (p + 1), :] * head_c).astype(bf16)
            fqt_ref[0, p, LANES:, rs] = bias_q_t
            fk_ref[0, rs, 2 * LANES * p:2 * LANES * p + LANES] = fk[:, LANES * p:LANES * (p + 1)].astype(bf16)
            fk_ref[0, rs, 2 * LANES * p + LANES:2 * LANES * (p + 1)] = bias_k
        store_values_t(proj_t[R_FV:R_FV + GROUP_W, :], fvt_ref)

        sbq_ref[0, rs] = (cols(C_SQ, GROUP_W) * head_c).astype(bf16)
        sbk_ref[0, rs] = cols(C_SK, GROUP_W).astype(bf16)
        sbv_ref[0, rs] = cols(C_SV, GROUP_W).astype(bf16)

        cqn = _rms_norm_rows(cols(C_CQ, MLA_Q_RANK), gq_ref[...]).astype(bf16)
        q_up_t = lax.dot_general(wqupt_ref[...], cqn, NT_DIMS, preferred_element_type=f32)
        ckvn = _rms_norm_rows(cols(C_CKV, MLA_KV_RANK), gkv_ref[...]).astype(bf16)
        k_up = jnp.dot(ckvn, wkup_ref[...], preferred_element_type=f32)
        k_rope = cols(C_KRP, LANES) * ck_ref[rs] + cols(C_KRRP, LANES) * sk_ref[rs]
        nq = N_HEADS * LANES
        for hh in range(N_HEADS):
            sl = slice(LANES * hh, LANES * (hh + 1))
            mqt_ref[0, hh, :, rs] = (q_up_t[sl, :] * cqt_ref[:, rs]
                                     + q_up_t[nq + LANES * hh:nq + LANES * (hh + 1), :] * sqt_ref[:, rs]).astype(bf16)
            mk_ref[0, rs, sl] = (k_up[:, sl] + k_rope).astype(bf16)
        store_values_t(lax.dot_general(wvupt_ref[...], ckvn, NT_DIMS, preferred_element_type=f32), mvt_ref)

        mem_q = cols(C_MQ, GROUP_W) * head_c
        lane_h = lax.broadcasted_iota(jnp.int32, (nr, LANES), 1)
        for p in range(2):
            qp = mem_q[:, LANES * p:LANES * (p + 1)]
            kp = mkv_ref[0, :, LANES * p:LANES * (p + 1)]
            vp = mkv_ref[0, :, GROUP_W + LANES * p:GROUP_W + LANES * (p + 1)]
            outs = []
            for hh in range(2):
                in_head = (lane_h >= HEAD_DIM * hh) & (lane_h < HEAD_DIM * (hh + 1))
                qh = jnp.where(in_head, qp, 0.0).astype(bf16)
                s = lax.dot_general(qh, kp, NT_DIMS, preferred_element_type=f32)
                m = jnp.max(s, axis=-1, keepdims=True)
                e = jnp.exp2(s - m)
                pr = e / jnp.sum(e, axis=-1, keepdims=True)
                outs.append(jnp.dot(pr.astype(bf16), vp, preferred_element_type=f32))
            omem_ref[0, rs, LANES * p:LANES * (p + 1)] = jnp.where(lane_h < HEAD_DIM, outs[0], outs[1]).astype(bf16)

        for c in range(4):
            g = cols(C_GATE + GROUP_W * c, GROUP_W)
            gate_ref[0, rs, GROUP_W * c:GROUP_W * (c + 1)] = (g / (1.0 + jnp.exp(-g))).astype(bf16)

    n_sub = max(tm // PROJ_SUB_ROWS, 1)
    for j in range(n_sub):
        sub_block(j * (tm // n_sub), tm // n_sub)


def _proj_call(h, ln_gb, w_packed, w_t, bfor_row, gq, wqupt, gkv, wkup, wvupt, tabs, mkv_l, tm, pre_ln):
    bsz, s, d = h.shape
    ln_g, ln_b = ln_gb
    m = mkv_l.shape[1]
    cqt, sqt, ck, sk = tabs
    bf16 = jnp.bfloat16
    full2 = lambda shape: pl.BlockSpec(shape, lambda b, i: (0, 0))
    row_blk = lambda c: pl.BlockSpec((1, tm, c), lambda b, i: (b, i, 0))
    tab_blk = pl.BlockSpec((tm, LANES), lambda b, i: (i, 0))
    tab_t_blk = pl.BlockSpec((LANES, tm), lambda b, i: (0, i))
    rows = lambda c, dt: (row_blk(c), jax.ShapeDtypeStruct((bsz, s, c), dt))
    feat = lambda n, r: (pl.BlockSpec((1, n, r, tm), lambda b, i: (b, 0, 0, i)),
                         jax.ShapeDtypeStruct((bsz, n, r, s), bf16))
    outs = [feat(2, 2 * LANES), rows(2 * GROUP_W, bf16), feat(N_HEADS, VT_ROWS),
            rows(GROUP_W, bf16), rows(GROUP_W, bf16), rows(GROUP_W, bf16),
            feat(N_HEADS, LANES), rows(2 * GROUP_W, bf16), feat(N_HEADS, VT_ROWS),
            rows(GROUP_W, bf16), rows(4 * GROUP_W, bf16)]
    if pre_ln:
        outs = [rows(d, jnp.float32)] + outs
    return pl.pallas_call(
        functools.partial(_proj_kernel, pre_ln=pre_ln),
        grid=(bsz, s // tm),
        in_specs=[row_blk(d), full2((1, d)), full2((1, d)),
                  full2(w_packed.shape), full2(w_t.shape),
                  full2((1, LANES)),
                  full2((1, MLA_Q_RANK)), full2(wqupt.shape),
                  full2((1, MLA_KV_RANK)), full2(wkup.shape), full2(wvupt.shape),
                  tab_t_blk, tab_t_blk, tab_blk, tab_blk,
                  pl.BlockSpec((1, m, 2 * GROUP_W), lambda b, i: (b, 0, 0))],
        out_specs=[spec for spec, _ in outs],
        out_shape=[shape for _, shape in outs],
        scratch_shapes=[pltpu.VMEM((1, LANES), jnp.float32)],
        compiler_params=_cparams(2),
        name="proj",
    )(h, ln_g.reshape(1, d), ln_b.reshape(1, d), w_packed, w_t, bfor_row, gq, wqupt, gkv, wkup, wvupt,
      cqt, sqt, ck, sk, mkv_l)


def _softmax_attn_kernel(qt_ref, k_ref, vt_ref, o_ref, q_sc, s_sc, m_sc, acc_sc, *, fox, tq, tk):
    pair = pl.program_id(1)
    qi = pl.program_id(2)
    bf16, f32 = jnp.bfloat16, jnp.float32
    kq = q_sc.shape[1]
    for hh in range(2):
        if fox:
            feat = lax.broadcasted_iota(jnp.int32, (kq, tq), 0)
            lo = LANES + FORGET_STRIDE * (2 * pair + hh)
            keep = ((feat >= HEAD_DIM * hh) & (feat < HEAD_DIM * (hh + 1))) | (
                (feat >= lo) & (feat < lo + FORGET_STRIDE))
            q_sc[hh] = jnp.where(keep, qt_ref[0, 0], jnp.zeros((kq, tq), bf16))
        else:
            q_sc[hh] = qt_ref[0, hh]
    m_sc[...] = jnp.full(m_sc.shape, -jnp.inf, f32)
    acc_sc[...] = jnp.zeros(acc_sc.shape, f32)

    def scores(kb, slot, c0=0):
        k0 = pl.multiple_of(kb * tk, tk)
        for hh in range(2):
            k = k_ref[0, pl.ds(k0, tk), :] if fox else k_ref[0, pl.ds(k0, tk), LANES * hh:LANES * (hh + 1)]
            s_sc[slot, hh, :, c0:] = jnp.dot(k, q_sc[hh, :, c0:], preferred_element_type=f32)

    def update(kb, slot, c0=0, nc=tq, triangle=False):
        k0 = pl.multiple_of(kb * tk, tk)
        qs = slice(c0, c0 + nc)
        if triangle:
            causal = (lax.broadcasted_iota(jnp.int32, (tk, nc), 0)
                      <= lax.broadcasted_iota(jnp.int32, (tk, nc), 1))
        for hh in range(2):
            s = s_sc[slot, hh, :, qs]
            if triangle:
                s = jnp.where(causal, s, -jnp.inf)
            m_prev = m_sc[hh, :, qs]
            m_new = jnp.maximum(m_prev, jnp.max(s, axis=0, keepdims=True))
            alpha = jnp.exp2(m_prev - m_new)
            p = jnp.exp2(s - m_new)
            acc_sc[hh, :, qs] = (alpha * acc_sc[hh, :, qs]
                                 + jnp.dot(vt_ref[0, hh, :, pl.ds(k0, tk)], p.astype(bf16),
                                           preferred_element_type=f32))
            m_sc[hh, :, qs] = m_new

    n_diag = tq // tk
    unroll = 4 if n_diag % 4 == 0 else 2
    assert n_diag % unroll == 0
    n_full = qi * n_diag
    scores(0, 0)

    def body(j, c):
        for u in range(unroll):
            kb = unroll * j + u
            scores(kb + 1, (u + 1) % 2)
            update(kb, u % 2)
        return c

    lax.fori_loop(0, n_full // unroll, body, 0)
    for d in range(n_diag):
        if d + 1 < n_diag:
            scores(n_full + d + 1, (d + 1) % 2, c0=(d + 1) * tk)
        update(n_full + d, d % 2, c0=d * tk, nc=tk, triangle=True)
        if d + 1 < n_diag:
            update(n_full + d, d % 2, c0=(d + 1) * tk, nc=tq - (d + 1) * tk)
    out_t = jnp.concatenate([acc_sc[hh, :HEAD_DIM] / acc_sc[hh, HEAD_DIM:HEAD_DIM + 1] for hh in range(2)],
                            axis=0)
    o_ref[0] = out_t.T.astype(o_ref.dtype)


def _softmax_attn_call(qt, k, vt, *, fox, tq, tk, name):
    bsz, nqt, kq, s = qt.shape
    kw = 2 * LANES
    kern = functools.partial(_softmax_attn_kernel, fox=fox, tq=tq, tk=tk)
    return pl.pallas_call(
        kern,
        grid=(bsz, 2, s // tq),
        in_specs=[pl.BlockSpec((1, nqt // 2, kq, tq), lambda b, p, i: (b, p, 0, i)),
                  pl.BlockSpec((1, s, kw), lambda b, p, i: (b, 0, p)),
                  pl.BlockSpec((1, 2, VT_ROWS, s), lambda b, p, i: (b, p, 0, 0))],
        out_specs=pl.BlockSpec((1, tq, LANES), lambda b, p, i: (b, i, p)),
        out_shape=jax.ShapeDtypeStruct((bsz, s, GROUP_W), jnp.bfloat16),
        scratch_shapes=[pltpu.VMEM((2, kq, tq), jnp.bfloat16),
                        pltpu.VMEM((2, 2, tk, tq), jnp.float32),
                        pltpu.VMEM((2, 1, tq), jnp.float32),
                        pltpu.VMEM((2, VT_ROWS, tq), jnp.float32)],
        compiler_params=_cparams(3),
        name=name,
    )(qt, k, vt)


def _sb_attn_kernel(q_ref, k_ref, v_ref, o_ref, q_sc, z_sc, c_sc, acc_sc, *, tq, tk):
    qi = pl.program_id(2)
    bf16, f32 = jnp.bfloat16, jnp.float32
    lane_q = lax.broadcasted_iota(jnp.int32, (tq, LANES), 1)
    rj = lax.broadcasted_iota(jnp.int32, (tk, tk), 0)
    cs = lax.broadcasted_iota(jnp.int32, (tk, tk), 1)
    upper = jnp.where(rj >= cs, 1.0, 0.0).astype(bf16)
    for hh in range(2):
        in_head = (lane_q >= HEAD_DIM * hh) & (lane_q < HEAD_DIM * (hh + 1))
        q_sc[hh] = jnp.where(in_head, q_ref[0], jnp.zeros((tq, LANES), bf16))
    c_sc[...] = jnp.zeros(c_sc.shape, f32)
    acc_sc[...] = jnp.zeros(acc_sc.shape, f32)

    def scores(kb, slot, r0=0):
        k0 = pl.multiple_of(kb * tk, tk)
        k = k_ref[0, pl.ds(k0, tk), :]
        for hh in range(2):
            z_sc[slot, hh, r0:] = lax.dot_general(q_sc[hh, r0:], k, NT_DIMS, preferred_element_type=f32)

    def update(kb, slot, r0=0, nr=tq, triangle=False):
        k0 = pl.multiple_of(kb * tk, tk)
        v = v_ref[0, pl.ds(k0, tk), :]
        rows = slice(r0, r0 + nr)
        if triangle:
            valid = (lax.broadcasted_iota(jnp.int32, (nr, tk), 1)
                     < lax.broadcasted_iota(jnp.int32, (nr, tk), 0))
        for hh in range(2):
            z = z_sc[slot, hh, rows]
            sp = jnp.maximum(jnp.log2(1.0 + jnp.exp2(jnp.minimum(z, EXP2_CLAMP))), z)
            if triangle:
                sp = jnp.where(valid, sp, 0.0)
            csum = (jnp.dot(sp.astype(bf16), upper, preferred_element_type=f32)
                    + jnp.tile(c_sc[hh, rows], (1, tk // LANES)))
            w = jnp.exp2(z - csum)
            if triangle:
                w = jnp.where(valid, w, 0.0)
            acc_sc[hh, rows] += jnp.dot(w.astype(bf16), v, preferred_element_type=f32)
            c_sc[hh, rows] = jnp.broadcast_to(csum[:, 0:1], (nr, LANES))

    n_diag = tq // tk
    unroll = 4 if n_diag % 4 == 0 else 2
    assert n_diag % unroll == 0
    n_full = qi * n_diag
    scores(n_full + n_diag - 1, 1, r0=(n_diag - 1) * tk)
    for d in reversed(range(n_diag)):
        kb = n_full + d
        scores(jnp.maximum(kb - 1, 0), (d + 1) % 2, r0=max(d - 1, 0) * tk)
        update(kb, d % 2, r0=d * tk, nr=tk, triangle=True)
        if d + 1 < n_diag:
            update(kb, d % 2, r0=(d + 1) * tk, nr=tq - (d + 1) * tk)

    def body(n, c):
        top = n_full - 1 - unroll * n
        for u in range(unroll):
            kb = top - u
            scores(jnp.maximum(kb - 1, 0), u % 2)
            update(kb, (u + 1) % 2)
        return c

    lax.fori_loop(0, n_full // unroll, body, 0)
    o_ref[0] = jnp.where(lane_q < HEAD_DIM, acc_sc[0], acc_sc[1]).astype(o_ref.dtype)


def _sb_attn_call(q, k, v, *, tq, tk):
    bsz, s, _ = q.shape
    kern = functools.partial(_sb_attn_kernel, tq=tq, tk=tk)
    return pl.pallas_call(
        kern,
        grid=(bsz, 2, s // tq),
        in_specs=[pl.BlockSpec((1, tq, LANES), lambda b, p, i: (b, i, p)),
                  pl.BlockSpec((1, s, LANES), lambda b, p, i: (b, 0, p)),
                  pl.BlockSpec((1, s, LANES), lambda b, p, i: (b, 0, p))],
        out_specs=pl.BlockSpec((1, tq, LANES), lambda b, p, i: (b, i, p)),
        out_shape=jax.ShapeDtypeStruct((bsz, s, GROUP_W), jnp.bfloat16),
        scratch_shapes=[pltpu.VMEM((2, tq, LANES), jnp.bfloat16),
                        pltpu.VMEM((2, 2, tq, tk), jnp.float32),
                        pltpu.VMEM((2, tq, LANES), jnp.float32),
                        pltpu.VMEM((2, tq, LANES), jnp.float32)],
        compiler_params=_cparams(3),
        name="sb_attn",
    )(q, k, v)


def _out_kernel(of_ref, os_ref, om_ref, ox_ref, gate_ref, h_ref, w_ref, g_ref, b_ref, o_ref, *, alpha):
    f32 = jnp.float32
    mixed = jnp.concatenate([of_ref[...], os_ref[...], om_ref[...], ox_ref[...]], axis=1)
    y = jnp.dot(mixed * gate_ref[...], w_ref[...], preferred_element_type=f32)
    o_ref[...] = _layer_norm_rows(alpha * h_ref[...] + y, g_ref[...], b_ref[...])


def _out_call(o_fox, o_sb, o_mla, o_mem, gate, h2d, w_out_bf16, g, b, alpha, tm):
    n, d = h2d.shape
    gw = o_fox.shape[1]
    kern = functools.partial(_out_kernel, alpha=alpha)
    blk = lambda c: pl.BlockSpec((tm, c), lambda i: (i, 0))
    return pl.pallas_call(
        kern,
        grid=(n // tm,),
        in_specs=[blk(gw), blk(gw), blk(gw), blk(gw), blk(4 * gw), blk(d),
                  pl.BlockSpec(w_out_bf16.shape, lambda i: (0, 0)),
                  pl.BlockSpec((1, d), lambda i: (0, 0)),
                  pl.BlockSpec((1, d), lambda i: (0, 0))],
        out_specs=blk(d),
        out_shape=jax.ShapeDtypeStruct((n, d), jnp.float32),
        compiler_params=_cparams(1),
        name="out_proj",
    )(o_fox, o_sb, o_mla, o_mem, gate, h2d, w_out_bf16, g.reshape(1, d), b.reshape(1, d))


def _pack_w_in(w):
    d = w.shape[0]
    w = w.astype(jnp.bfloat16)
    o = 0
    cols = {}
    for name, width in (("fq", 256), ("fk", 256), ("fv", 256), ("fl", 4), ("sq", 256), ("sk", 256),
                        ("sv", 256), ("cq", 256), ("ckv", 128), ("kr", 32), ("mq", 256), ("gate", 1024)):
        cols[name] = w[:, o:o + width]
        o += width
    z = lambda n: jnp.zeros((d, n), w.dtype)
    half = MLA_ROPE // 2
    kr = cols["kr"]
    kr_rot = jnp.concatenate([-kr[:, half:], kr[:, :half]], axis=1)
    krp = jnp.concatenate([z(MLA_NOPE), kr, z(LANES - MLA_NOPE - MLA_ROPE)], axis=1)
    krrp = jnp.concatenate([z(MLA_NOPE), kr_rot, z(LANES - MLA_NOPE - MLA_ROPE)], axis=1)
    rep = jnp.repeat(cols["fl"], FORGET_STRIDE, axis=1)
    keep = (jnp.arange(N_HEADS * FORGET_STRIDE) % FORGET_STRIDE) < FORGET_DUP
    misc = [jnp.where(keep[None, :], rep, jnp.zeros_like(rep)), z(LANES - N_HEADS * FORGET_STRIDE)]
    packed = jnp.concatenate([cols["gate"], cols["cq"], cols["ckv"]] + misc
                             + [cols["mq"], krp, krrp, cols["fk"], cols["sq"], cols["sk"], cols["sv"]], axis=1)
    assert packed.shape[1] == W_COLS
    w_t = jnp.concatenate([cols["fq"], cols["fv"]], axis=1).T
    assert w_t.shape[0] == WT_ROWS
    return packed.astype(jnp.bfloat16), w_t.astype(jnp.bfloat16)


def _pack_forget_bias(b_forget_l):
    row = jnp.zeros((LANES,), jnp.float32)
    for hh in range(N_HEADS):
        row = row.at[FORGET_STRIDE * hh:FORGET_STRIDE * hh + FORGET_DUP].set(b_forget_l[hh])
    return row.reshape(1, LANES)


def _pack_mla_q_up_t(w):
    r = w.shape[0]
    per = MLA_NOPE + MLA_ROPE
    half = MLA_ROPE // 2
    z = lambda n: jnp.zeros((r, n), w.dtype)
    plain, rot = [], []
    for hh in range(N_HEADS):
        nope = w[:, per * hh:per * hh + MLA_NOPE]
        rope = w[:, per * hh + MLA_NOPE:per * (hh + 1)]
        rope_rot = jnp.concatenate([-rope[:, half:], rope[:, :half]], axis=1)
        plain += [nope, rope, z(LANES - per)]
        rot += [z(MLA_NOPE), rope_rot, z(LANES - per)]
    return jnp.concatenate(plain + rot, axis=1).T.astype(jnp.bfloat16)


def _pack_mla_kv_up(w):
    r = w.shape[0]
    per = MLA_NOPE + HEAD_DIM
    z = jnp.zeros((r, LANES - MLA_NOPE), w.dtype)
    ks, vs = [], []
    for hh in range(N_HEADS):
        ks += [w[:, per * hh:per * hh + MLA_NOPE], z]
        vs.append(w[:, per * hh + MLA_NOPE:per * (hh + 1)])
    return (jnp.concatenate(ks, axis=1).astype(jnp.bfloat16),
            jnp.concatenate(vs, axis=1).T.astype(jnp.bfloat16))


def _rope_tables(s, q_scale):
    half = MLA_ROPE // 2
    inv_freq = ROPE_THETA ** (-jnp.arange(half, dtype=jnp.float32) / half)
    ang = jnp.arange(s).astype(jnp.float32)[:, None] * inv_freq[None, :]
    cos, sin = jnp.cos(ang), jnp.sin(ang)
    ones = jnp.ones((s, MLA_NOPE), jnp.float32)
    z_nope = jnp.zeros((s, MLA_NOPE), jnp.float32)
    z_pad = jnp.zeros((s, LANES - MLA_NOPE - MLA_ROPE), jnp.float32)
    cos_q = jnp.concatenate([ones, cos, cos, z_pad], axis=1) * q_scale
    sin_q = jnp.concatenate([z_nope, sin, sin, z_pad], axis=1) * q_scale
    cos_k = jnp.concatenate([z_nope, cos, cos, z_pad], axis=1)
    sin_k = jnp.concatenate([z_nope, sin, sin, z_pad], axis=1)
    return cos_q.T, sin_q.T, cos_k, sin_k


def kernel(x, mem, ln_in_g, ln_in_b, mem_ln_g, mem_ln_b, w_in, b_forget, mla_q_norm_g, w_mla_q_up,
           mla_kv_norm_g, w_mla_kv_up, w_mem_kv, w_out, ln_g, ln_b):
    bsz, s, d = x.shape
    depth = w_in.shape[0]
    alpha = (2 * depth) ** 0.25
    tm = min(1024, s)
    tm_out = min(1024, s)
    tq_sm = min(2048, s)
    tq_sb = min(1024, s)
    tk_sm = min(512, tq_sm // 2)
    tk_sb = min(256, tq_sb // 2)
    mla_scale = (MLA_NOPE + MLA_ROPE) ** -0.5
    tabs = _rope_tables(s, mla_scale * LOG2E)

    mkv = _mem_kv_call(mem, mem_ln_g, mem_ln_b, w_mem_kv.astype(jnp.bfloat16))

    h = x
    for l in range(depth):
        w_packed, w_t = _pack_w_in(w_in[l])
        w_k_up, w_v_up_t = _pack_mla_kv_up(w_mla_kv_up[l])
        outs = _proj_call(
            h, (ln_in_g, ln_in_b), w_packed, w_t, _pack_forget_bias(b_forget[l]),
            mla_q_norm_g[l].reshape(1, -1), _pack_mla_q_up_t(w_mla_q_up[l]),
            mla_kv_norm_g[l].reshape(1, -1), w_k_up, w_v_up_t,
            tabs, mkv[l], tm, pre_ln=(l == 0))
        if l == 0:
            h, outs = outs[0], outs[1:]
        (fqt, fk, fvt, sbq, sbk, sbv, mqt, mk, mvt, o_mem, gate) = outs
        o_fox = _softmax_attn_call(fqt, fk, fvt, fox=True, tq=tq_sm, tk=tk_sm, name="fox_attn")
        o_sb = _sb_attn_call(sbq, sbk, sbv, tq=tq_sb, tk=tk_sb)
        o_mla = _softmax_attn_call(mqt, mk, mvt, fox=False, tq=tq_sm, tk=tk_sm, name="mla_attn")
        n = bsz * s
        h = _out_call(o_fox.reshape(n, -1), o_sb.reshape(n, -1), o_mla.reshape(n, -1), o_mem.reshape(n, -1),
                      gate.reshape(n, -1), h.reshape(n, d), w_out[l].astype(jnp.bfloat16),
                      ln_g[l], ln_b[l], alpha, tm_out).reshape(bsz, s, d)
    return h
```

```python
import functools

import jax
import jax.numpy as jnp
from jax import lax
from jax.experimental import pallas as pl
from jax.experimental.pallas import tpu as pltpu

N_HEADS = 4
HEAD_DIM = 64
GROUP_W = N_HEADS * HEAD_DIM
MLA_NOPE = 64
MLA_ROPE = 32
MLA_Q_RANK = 256
MLA_KV_RANK = 128
ROPE_THETA = 10000.0
LN_EPS = 1e-5
RMS_EPS = 1e-6
LOG2E = 1.4426950408889634
EXP2_CLAMP = 126.0

LANES = 128
VMEM_LIMIT_BYTES = 56 * 1024 * 1024

C_GATE = 0
C_CQ = 1024
C_CKV = 1280
C_MISC = 1408
C_MQ = 1536
C_KRP = 1792
C_KRRP = 1920
C_FK = 2048
C_SQ, C_SK, C_SV = 2304, 2560, 2816
W_COLS = 3072
R_FQ, R_FV = 0, 256
WT_ROWS = 512
FORGET_DUP = 6
FORGET_STRIDE = 8
VT_ROWS = HEAD_DIM + 16
PROJ_SUB_ROWS = 512
NT_DIMS = (((1,), (1,)), ((), ()))


def _cparams(n_grid):
    return pltpu.CompilerParams(dimension_semantics=("arbitrary",) * n_grid,
                                vmem_limit_bytes=VMEM_LIMIT_BYTES)


def _split3(x):
    p0 = x.astype(jnp.bfloat16).astype(jnp.float32)
    r1 = x - p0
    p1 = r1.astype(jnp.bfloat16).astype(jnp.float32)
    p2 = (r1 - p1).astype(jnp.bfloat16).astype(jnp.float32)
    return p0, p1, p2


def _layer_norm_rows(x, g, b):
    mu = jnp.mean(x, axis=-1, keepdims=True)
    xc = x - mu
    var = jnp.mean(xc * xc, axis=-1, keepdims=True)
    return xc * lax.rsqrt(var + LN_EPS) * g + b


def _rms_norm_rows(x, g):
    ms = jnp.mean(x * x, axis=-1, keepdims=True)
    return x * lax.rsqrt(ms + RMS_EPS) * g


def _mem_kv_kernel(mem_ref, g_ref, b_ref, w_ref, o_ref):
    mem_n = _layer_norm_rows(mem_ref[0], g_ref[...], b_ref[...]).astype(jnp.bfloat16)
    for l in range(w_ref.shape[0]):
        o_ref[l, 0] = jnp.dot(mem_n, w_ref[l], preferred_element_type=jnp.float32).astype(jnp.bfloat16)


def _mem_kv_call(mem, g, b, w_mem_kv_bf16):
    bsz, m, d = mem.shape
    depth, _, c = w_mem_kv_bf16.shape
    return pl.pallas_call(
        _mem_kv_kernel,
        grid=(bsz,),
        in_specs=[pl.BlockSpec((1, m, d), lambda i: (i, 0, 0)),
                  pl.BlockSpec((1, d), lambda i: (0, 0)),
                  pl.BlockSpec((1, d), lambda i: (0, 0)),
                  pl.BlockSpec((depth, d, c), lambda i: (0, 0, 0))],
        out_specs=pl.BlockSpec((depth, 1, m, c), lambda i: (0, i, 0, 0)),
        out_shape=jax.ShapeDtypeStruct((depth, bsz, m, c), jnp.bfloat16),
        compiler_params=_cparams(1),
        name="mem_kv",
    )(mem, g.reshape(1, d), b.reshape(1, d), w_mem_kv_bf16)


def _proj_kernel(h_ref, lng_ref, lnb_ref, w_ref, wt_ref, bfor_ref, gq_ref, wqupt_ref, gkv_ref, wkup_ref,
                 wvupt_ref, cqt_ref, sqt_ref, ck_ref, sk_ref, mkv_ref, *rest, pre_ln):
    hout_ref = rest[0] if pre_ln else None
    (fqt_ref, fk_ref, fvt_ref, sbq_ref, sbk_ref, sbv_ref,
     mqt_ref, mk_ref, mvt_ref, omem_ref, gate_ref, carry_ref) = rest[1:] if pre_ln else rest
    i = pl.program_id(1)
    tm = h_ref.shape[1]
    bf16, f32 = jnp.bfloat16, jnp.float32
    head_c = (HEAD_DIM ** -0.5) * LOG2E

    @pl.when(i == 0)
    def _():
        carry_ref[...] = jnp.zeros_like(carry_ref)

    def sub_block(r0, nr):
        rs = slice(r0, r0 + nr)
        h = h_ref[0, rs]
        if pre_ln:
            h = _layer_norm_rows(h, lng_ref[...], lnb_ref[...])
            hout_ref[0, rs] = h
        hb = h.astype(bf16)
        proj = jnp.dot(hb, w_ref[...], preferred_element_type=f32)
        proj_t = lax.dot_general(wt_ref[...], hb, NT_DIMS, preferred_element_type=f32)

        def cols(c0, n):
            return proj[:, c0:c0 + n]

        def store_values_t(vt, out_ref):
            for hh in range(N_HEADS):
                out_ref[0, hh, :HEAD_DIM, rs] = vt[HEAD_DIM * hh:HEAD_DIM * (hh + 1), :].astype(bf16)
                out_ref[0, hh, HEAD_DIM:, rs] = jnp.ones((VT_ROWS - HEAD_DIM, nr), bf16)

        lane = lax.broadcasted_iota(jnp.int32, (nr, LANES), 1)
        sub = lane % FORGET_STRIDE
        used = (lane < N_HEADS * FORGET_STRIDE) & (sub < FORGET_DUP)
        xf = cols(C_MISC, LANES) + bfor_ref[...]
        log_f = jnp.minimum(xf, 0.0) - jnp.log1p(jnp.exp(-jnp.abs(xf)))
        log_f = jnp.where(used, log_f, 0.0)
        row = lax.broadcasted_iota(jnp.int32, (nr, nr), 0)
        col = lax.broadcasted_iota(jnp.int32, (nr, nr), 1)
        tril = jnp.where(col <= row, 1.0, 0.0).astype(bf16)
        parts = jnp.concatenate([p.astype(bf16) for p in _split3(log_f)], axis=1)
        csum = jnp.dot(tril, parts, preferred_element_type=f32)
        f_cum = (csum[:, :LANES] + csum[:, LANES:2 * LANES]) + csum[:, 2 * LANES:] + carry_ref[...]
        carry_ref[...] = f_cum[nr - 1:nr, :]
        p0, p1, p2 = _split3(f_cum * LOG2E)
        bias_q = jnp.where(sub == 0, p0, jnp.where(sub == 1, p1, jnp.where(sub == 2, p2, 1.0)))
        bias_q_t = jnp.where(used, bias_q, 0.0).T.astype(bf16)
        bias_k = jnp.where(sub == 3, -p0, jnp.where(sub == 4, -p1, jnp.where(sub == 5, -p2, 1.0)))
        bias_k = jnp.where(used, bias_k, 0.0).astype(bf16)

        fk = cols(C_FK, GROUP_W)
        for p in range(2):
            fqt_ref[0, p, :LANES, rs] = (proj_t[R_FQ + LANES * p:R_FQ + LANES * (p + 1), :] * head_c).astype(bf16)
            fqt_ref[0, p, LANES:, rs] = bias_q_t
            fk_ref[0, rs, 2 * LANES * p:2 * LANES * p + LANES] = fk[:, LANES * p:LANES * (p + 1)].astype(bf16)
            fk_ref[0, rs, 2 * LANES * p + LANES:2 * LANES * (p + 1)] = bias_k
        store_values_t(proj_t[R_FV:R_FV + GROUP_W, :], fvt_ref)

        sbq_ref[0, rs] = (cols(C_SQ, GROUP_W) * head_c).astype(bf16)
        sbk_ref[0, rs] = cols(C_SK, GROUP_W).astype(bf16)
        sbv_ref[0, rs] = cols(C_SV, GROUP_W).astype(bf16)

        cqn = _rms_norm_rows(cols(C_CQ, MLA_Q_RANK), gq_ref[...]).astype(bf16)
        q_up_t = lax.dot_general(wqupt_ref[...], cqn, NT_DIMS, preferred_element_type=f32)
        ckvn = _rms_norm_rows(cols(C_CKV, MLA_KV_RANK), gkv_ref[...]).astype(bf16)
        k_up = jnp.dot(ckvn, wkup_ref[...], preferred_element_type=f32)
        k_rope = cols(C_KRP, LANES) * ck_ref[rs] + cols(C_KRRP, LANES) * sk_ref[rs]
        nq = N_HEADS * LANES
        for hh in range(N_HEADS):
            sl = slice(LANES * hh, LANES * (hh + 1))
            mqt_ref[0, hh, :, rs] = (q_up_t[sl, :] * cqt_ref[:, rs]
                                     + q_up_t[nq + LANES * hh:nq + LANES * (hh + 1), :] * sqt_ref[:, rs]).astype(bf16)
            mk_ref[0, rs, sl] = (k_up[:, sl] + k_rope).astype(bf16)
        store_values_t(lax.dot_general(wvupt_ref[...], ckvn, NT_DIMS, preferred_element_type=f32), mvt_ref)

        mem_q = cols(C_MQ, GROUP_W) * head_c
        lane_h = lax.broadcasted_iota(jnp.int32, (nr, LANES), 1)
        for p in range(2):
            qp = mem_q[:, LANES * p:LANES * (p + 1)]
            kp = mkv_ref[0, :, LANES * p:LANES * (p + 1)]
            vp = mkv_ref[0, :, GROUP_W + LANES * p:GROUP_W + LANES * (p + 1)]
            outs = []
            for hh in range(2):
                in_head = (lane_h >= HEAD_DIM * hh) & (lane_h < HEAD_DIM * (hh + 1))
                qh = jnp.where(in_head, qp, 0.0).astype(bf16)
                s = lax.dot_general(qh, kp, NT_DIMS, preferred_element_type=f32)
                m = jnp.max(s, axis=-1, keepdims=True)
                e = jnp.exp2(s - m)
                pr = e / jnp.sum(e, axis=-1, keepdims=True)
                outs.append(jnp.dot(pr.astype(bf16), vp, preferred_element_type=f32))
            omem_ref[0, rs, LANES * p:LANES * (p + 1)] = jnp.where(lane_h < HEAD_DIM, outs[0], outs[1]).astype(bf16)

        for c in range(4):
            g = cols(C_GATE + GROUP_W * c, GROUP_W)
            gate_ref[0, rs, GROUP_W * c:GROUP_W * (c + 1)] = (g / (1.0 + jnp.exp(-g))).astype(bf16)

    n_sub = max(tm // PROJ_SUB_ROWS, 1)
    for j in range(n_sub):
        sub_block(j * (tm // n_sub), tm // n_sub)


def _proj_call(h, ln_gb, w_packed, w_t, bfor_row, gq, wqupt, gkv, wkup, wvupt, tabs, mkv_l, tm, pre_ln):
    bsz, s, d = h.shape
    ln_g, ln_b = ln_gb
    m = mkv_l.shape[1]
    cqt, sqt, ck, sk = tabs
    bf16 = jnp.bfloat16
    full2 = lambda shape: pl.BlockSpec(shape, lambda b, i: (0, 0))
    row_blk = lambda c: pl.BlockSpec((1, tm, c), lambda b, i: (b, i, 0))
    tab_blk = pl.BlockSpec((tm, LANES), lambda b, i: (i, 0))
    tab_t_blk = pl.BlockSpec((LANES, tm), lambda b, i: (0, i))
    rows = lambda c, dt: (row_blk(c), jax.ShapeDtypeStruct((bsz, s, c), dt))
    feat = lambda n, r: (pl.BlockSpec((1, n, r, tm), lambda b, i: (b, 0, 0, i)),
                         jax.ShapeDtypeStruct((bsz, n, r, s), bf16))
    outs = [feat(2, 2 * LANES), rows(2 * GROUP_W, bf16), feat(N_HEADS, VT_ROWS),
            rows(GROUP_W, bf16), rows(GROUP_W, bf16), rows(GROUP_W, bf16),
            feat(N_HEADS, LANES), rows(2 * GROUP_W, bf16), feat(N_HEADS, VT_ROWS),
            rows(GROUP_W, bf16), rows(4 * GROUP_W, bf16)]
    if pre_ln:
        outs = [rows(d, jnp.float32)] + outs
    return pl.pallas_call(
        functools.partial(_proj_kernel, pre_ln=pre_ln),
        grid=(bsz, s // tm),
        in_specs=[row_blk(d), full2((1, d)), full2((1, d)),
                  full2(w_packed.shape), full2(w_t.shape),
                  full2((1, LANES)),
                  full2((1, MLA_Q_RANK)), full2(wqupt.shape),
                  full2((1, MLA_KV_RANK)), full2(wkup.shape), full2(wvupt.shape),
                  tab_t_blk, tab_t_blk, tab_blk, tab_blk,
                  pl.BlockSpec((1, m, 2 * GROUP_W), lambda b, i: (b, 0, 0))],
        out_specs=[spec for spec, _ in outs],
        out_shape=[shape for _, shape in outs],
        scratch_shapes=[pltpu.VMEM((1, LANES), jnp.float32)],
        compiler_params=_cparams(2),
        name="proj",
    )(h, ln_g.reshape(1, d), ln_b.reshape(1, d), w_packed, w_t, bfor_row, gq, wqupt, gkv, wkup, wvupt,
      cqt, sqt, ck, sk, mkv_l)


def _softmax_attn_kernel(qt_ref, k_ref, vt_ref, o_ref, q_sc, s_sc, m_sc, acc_sc, *, fox, tq, tk):
    pair = pl.program_id(1)
    qi = pl.program_id(2)
    bf16, f32 = jnp.bfloat16, jnp.float32
    kq = q_sc.shape[1]
    for hh in range(2):
        if fox:
            feat = lax.broadcasted_iota(jnp.int32, (kq, tq), 0)
            lo = LANES + FORGET_STRIDE * (2 * pair + hh)
            keep = ((feat >= HEAD_DIM * hh) & (feat < HEAD_DIM * (hh + 1))) | (
                (feat >= lo) & (feat < lo + FORGET_STRIDE))
            q_sc[hh] = jnp.where(keep, qt_ref[0, 0], jnp.zeros((kq, tq), bf16))
        else:
            q_sc[hh] = qt_ref[0, hh]
    m_sc[...] = jnp.full(m_sc.shape, -jnp.inf, f32)
    acc_sc[...] = jnp.zeros(acc_sc.shape, f32)

    def scores(kb, slot, c0=0):
        k0 = pl.multiple_of(kb * tk, tk)
        for hh in range(2):
            k = k_ref[0, pl.ds(k0, tk), :] if fox else k_ref[0, pl.ds(k0, tk), LANES * hh:LANES * (hh + 1)]
            s_sc[slot, hh, :, c0:] = jnp.dot(k, q_sc[hh, :, c0:], preferred_element_type=f32)

    def update(kb, slot, c0=0, nc=tq, triangle=False):
        k0 = pl.multiple_of(kb * tk, tk)
        qs = slice(c0, c0 + nc)
        if triangle:
            causal = (lax.broadcasted_iota(jnp.int32, (tk, nc), 0)
                      <= lax.broadcasted_iota(jnp.int32, (tk, nc), 1))
        for hh in range(2):
            s = s_sc[slot, hh, :, qs]
            if triangle:
                s = jnp.where(causal, s, -jnp.inf)
            m_prev = m_sc[hh, :, qs]
            m_new = jnp.maximum(m_prev, jnp.max(s, axis=0, keepdims=True))
            alpha = jnp.exp2(m_prev - m_new)
            p = jnp.exp2(s - m_new)
            acc_sc[hh, :, qs] = (alpha * acc_sc[hh, :, qs]
                                 + jnp.dot(vt_ref[0, hh, :, pl.ds(k0, tk)], p.astype(bf16),
                                           preferred_element_type=f32))
            m_sc[hh, :, qs] = m_new

    n_diag = tq // tk
    unroll = 4 if n_diag % 4 == 0 else 2
    assert n_diag % unroll == 0
    n_full = qi * n_diag
    scores(0, 0)

    def body(j, c):
        for u in range(unroll):
            kb = unroll * j + u
            scores(kb + 1, (u + 1) % 2)
            update(kb, u % 2)
        return c

    lax.fori_loop(0, n_full // unroll, body, 0)
    for d in range(n_diag):
        if d + 1 < n_diag:
            scores(n_full + d + 1, (d + 1) % 2, c0=(d + 1) * tk)
        update(n_full + d, d % 2, c0=d * tk, nc=tk, triangle=True)
        if d + 1 < n_diag:
            update(n_full + d, d % 2, c0=(d + 1) * tk, nc=tq - (d + 1) * tk)
    out_t = jnp.concatenate([acc_sc[hh, :HEAD_DIM] / acc_sc[hh, HEAD_DIM:HEAD_DIM + 1] for hh in range(2)],
                            axis=0)
    o_ref[0] = out_t.T.astype(o_ref.dtype)


def _softmax_attn_call(qt, k, vt, *, fox, tq, tk, name):
    bsz, nqt, kq, s = qt.shape
    kw = 2 * LANES
    kern = functools.partial(_softmax_attn_kernel, fox=fox, tq=tq, tk=tk)
    return pl.pallas_call(
        kern,
        grid=(bsz, 2, s // tq),
        in_specs=[pl.BlockSpec((1, nqt // 2, kq, tq), lambda b, p, i: (b, p, 0, i)),
                  pl.BlockSpec((1, s, kw), lambda b, p, i: (b, 0, p)),
                  pl.BlockSpec((1, 2, VT_ROWS, s), lambda b, p, i: (b, p, 0, 0))],
        out_specs=pl.BlockSpec((1, tq, LANES), lambda b, p, i: (b, i, p)),
        out_shape=jax.ShapeDtypeStruct((bsz, s, GROUP_W), jnp.bfloat16),
        scratch_shapes=[pltpu.VMEM((2, kq, tq), jnp.bfloat16),
                        pltpu.VMEM((2, 2, tk, tq), jnp.float32),
                        pltpu.VMEM((2, 1, tq), jnp.float32),
                        pltpu.VMEM((2, VT_ROWS, tq), jnp.float32)],
        compiler_params=_cparams(3),
        name=name,
    )(qt, k, vt)


def _sb_attn_kernel(q_ref, k_ref, v_ref, o_ref, q_sc, z_sc, c_sc, acc_sc, *, tq, tk):
    qi = pl.program_id(2)
    bf16, f32 = jnp.bfloat16, jnp.float32
    lane_q = lax.broadcasted_iota(jnp.int32, (tq, LANES), 1)
    rj = lax.broadcasted_iota(jnp.int32, (tk, tk), 0)
    cs = lax.broadcasted_iota(jnp.int32, (tk, tk), 1)
    upper = jnp.where(rj >= cs, 1.0, 0.0).astype(bf16)
    for hh in range(2):
        in_head = (lane_q >= HEAD_DIM * hh) & (lane_q < HEAD_DIM * (hh + 1))
        q_sc[hh] = jnp.where(in_head, q_ref[0], jnp.zeros((tq, LANES), bf16))
    c_sc[...] = jnp.zeros(c_sc.shape, f32)
    acc_sc[...] = jnp.zeros(acc_sc.shape, f32)

    def scores(kb, slot, r0=0):
        k0 = pl.multiple_of(kb * tk, tk)
        k = k_ref[0, pl.ds(k0, tk), :]
        for hh in range(2):
            z_sc[slot, hh, r0:] = lax.dot_general(q_sc[hh, r0:], k, NT_DIMS, preferred_element_type=f32)

    def update(kb, slot, r0=0, nr=tq, triangle=False):
        k0 = pl.multiple_of(kb * tk, tk)
        v = v_ref[0, pl.ds(k0, tk), :]
        rows = slice(r0, r0 + nr)
        if triangle:
            valid = (lax.broadcasted_iota(jnp.int32, (nr, tk), 1)
                     < lax.broadcasted_iota(jnp.int32, (nr, tk), 0))
        for hh in range(2):
            z = z_sc[slot, hh, rows]
            sp = jnp.maximum(jnp.log2(1.0 + jnp.exp2(jnp.minimum(z, EXP2_CLAMP))), z)
            if triangle:
                sp = jnp.where(valid, sp, 0.0)
            csum = (jnp.dot(sp.astype(bf16), upper, preferred_element_type=f32)
                    + jnp.tile(c_sc[hh, rows], (1, tk // LANES)))
            w = jnp.exp2(z - csum)
            if triangle:
                w = jnp.where(valid, w, 0.0)
            acc_sc[hh, rows] += jnp.dot(w.astype(bf16), v, preferred_element_type=f32)
            c_sc[hh, rows] = jnp.broadcast_to(csum[:, 0:1], (nr, LANES))

    n_diag = tq // tk
    unroll = 4 if n_diag % 4 == 0 else 2
    assert n_diag % unroll == 0
    n_full = qi * n_diag
    scores(n_full + n_diag - 1, 1, r0=(n_diag - 1) * tk)
    for d in reversed(range(n_diag)):
        kb = n_full + d
        scores(jnp.maximum(kb - 1, 0), (d + 1) % 2, r0=max(d - 1, 0) * tk)
        update(kb, d % 2, r0=d * tk, nr=tk, triangle=True)
        if d + 1 < n_diag:
            update(kb, d % 2, r0=(d + 1) * tk, nr=tq - (d + 1) * tk)

    def body(n, c):
        top = n_full - 1 - unroll * n
        for u in range(unroll):
            kb = top - u
            scores(jnp.maximum(kb - 1, 0), u % 2)
            update(kb, (u + 1) % 2)
        return c

    lax.fori_loop(0, n_full // unroll, body, 0)
    o_ref[0] = jnp.where(lane_q < HEAD_DIM, acc_sc[0], acc_sc[1]).astype(o_ref.dtype)


def _sb_attn_call(q, k, v, *, tq, tk):
    bsz, s, _ = q.shape
    kern = functools.partial(_sb_attn_kernel, tq=tq, tk=tk)
    return pl.pallas_call(
        kern,
        grid=(bsz, 2, s // tq),
        in_specs=[pl.BlockSpec((1, tq, LANES), lambda b, p, i: (b, i, p)),
                  pl.BlockSpec((1, s, LANES), lambda b, p, i: (b, 0, p)),
                  pl.BlockSpec((1, s, LANES), lambda b, p, i: (b, 0, p))],
        out_specs=pl.BlockSpec((1, tq, LANES), lambda b, p, i: (b, i, p)),
        out_shape=jax.ShapeDtypeStruct((bsz, s, GROUP_W), jnp.bfloat16),
        scratch_shapes=[pltpu.VMEM((2, tq, LANES), jnp.bfloat16),
                        pltpu.VMEM((2, 2, tq, tk), jnp.float32),
                        pltpu.VMEM((2, tq, LANES), jnp.float32),
                        pltpu.VMEM((2, tq, LANES), jnp.float32)],
        compiler_params=_cparams(3),
        name="sb_attn",
    )(q, k, v)


def _out_kernel(of_ref, os_ref, om_ref, ox_ref, gate_ref, h_ref, w_ref, g_ref, b_ref, o_ref, *, alpha):
    f32 = jnp.float32
    mixed = jnp.concatenate([of_ref[...], os_ref[...], om_ref[...], ox_ref[...]], axis=1)
    y = jnp.dot(mixed * gate_ref[...], w_ref[...], preferred_element_type=f32)
    o_ref[...] = _layer_norm_rows(alpha * h_ref[...] + y, g_ref[...], b_ref[...])


def _out_call(o_fox, o_sb, o_mla, o_mem, gate, h2d, w_out_bf16, g, b, alpha, tm):
    n, d = h2d.shape
    gw = o_fox.shape[1]
    kern = functools.partial(_out_kernel, alpha=alpha)
    blk = lambda c: pl.BlockSpec((tm, c), lambda i: (i, 0))
    return pl.pallas_call(
        kern,
        grid=(n // tm,),
        in_specs=[blk(gw), blk(gw), blk(gw), blk(gw), blk(4 * gw), blk(d),
                  pl.BlockSpec(w_out_bf16.shape, lambda i: (0, 0)),
                  pl.BlockSpec((1, d), lambda i: (0, 0)),
                  pl.BlockSpec((1, d), lambda i: (0, 0))],
        out_specs=blk(d),
        out_shape=jax.ShapeDtypeStruct((n, d), jnp.float32),
        compiler_params=_cparams(1),
        name="out_proj",
    )(o_fox, o_sb, o_mla, o_mem, gate, h2d, w_out_bf16, g.reshape(1, d), b.reshape(1, d))


def _pack_w_in(w):
    d = w.shape[0]
    w = w.astype(jnp.bfloat16)
    o = 0
    cols = {}
    for name, width in (("fq", 256), ("fk", 256), ("fv", 256), ("fl", 4), ("sq", 256), ("sk", 256),
                        ("sv", 256), ("cq", 256), ("ckv", 128), ("kr", 32), ("mq", 256), ("gate", 1024)):
        cols[name] = w[:, o:o + width]
        o += width
    z = lambda n: jnp.zeros((d, n), w.dtype)
    half = MLA_ROPE // 2
    kr = cols["kr"]
    kr_rot = jnp.concatenate([-kr[:, half:], kr[:, :half]], axis=1)
    krp = jnp.concatenate([z(MLA_NOPE), kr, z(LANES - MLA_NOPE - MLA_ROPE)], axis=1)
    krrp = jnp.concatenate([z(MLA_NOPE), kr_rot, z(LANES - MLA_NOPE - MLA_ROPE)], axis=1)
    rep = jnp.repeat(cols["fl"], FORGET_STRIDE, axis=1)
    keep = (jnp.arange(N_HEADS * FORGET_STRIDE) % FORGET_STRIDE) < FORGET_DUP
    misc = [jnp.where(keep[None, :], rep, jnp.zeros_like(rep)), z(LANES - N_HEADS * FORGET_STRIDE)]
    packed = jnp.concatenate([cols["gate"], cols["cq"], cols["ckv"]] + misc
                             + [cols["mq"], krp, krrp, cols["fk"], cols["sq"], cols["sk"], cols["sv"]], axis=1)
    assert packed.shape[1] == W_COLS
    w_t = jnp.concatenate([cols["fq"], cols["fv"]], axis=1).T
    assert w_t.shape[0] == WT_ROWS
    return packed.astype(jnp.bfloat16), w_t.astype(jnp.bfloat16)


def _pack_forget_bias(b_forget_l):
    row = jnp.zeros((LANES,), jnp.float32)
    for hh in range(N_HEADS):
        row = row.at[FORGET_STRIDE * hh:FORGET_STRIDE * hh + FORGET_DUP].set(b_forget_l[hh])
    return row.reshape(1, LANES)


def _pack_mla_q_up_t(w):
    r = w.shape[0]
    per = MLA_NOPE + MLA_ROPE
    half = MLA_ROPE // 2
    z = lambda n: jnp.zeros((r, n), w.dtype)
    plain, rot = [], []
    for hh in range(N_HEADS):
        nope = w[:, per * hh:per * hh + MLA_NOPE]
        rope = w[:, per * hh + MLA_NOPE:per * (hh + 1)]
        rope_rot = jnp.concatenate([-rope[:, half:], rope[:, :half]], axis=1)
        plain += [nope, rope, z(LANES - per)]
        rot += [z(MLA_NOPE), rope_rot, z(LANES - per)]
    return jnp.concatenate(plain + rot, axis=1).T.astype(jnp.bfloat16)


def _pack_mla_kv_up(w):
    r = w.shape[0]
    per = MLA_NOPE + HEAD_DIM
    z = jnp.zeros((r, LANES - MLA_NOPE), w.dtype)
    ks, vs = [], []
    for hh in range(N_HEADS):
        ks += [w[:, per * hh:per * hh + MLA_NOPE], z]
        vs.append(w[:, per * hh + MLA_NOPE:per * (hh + 1)])
    return (jnp.concatenate(ks, axis=1).astype(jnp.bfloat16),
            jnp.concatenate(vs, axis=1).T.astype(jnp.bfloat16))


def _rope_tables(s, q_scale):
    half = MLA_ROPE // 2
    inv_freq = ROPE_THETA ** (-jnp.arange(half, dtype=jnp.float32) / half)
    ang = jnp.arange(s).astype(jnp.float32)[:, None] * inv_freq[None, :]
    cos, sin = jnp.cos(ang), jnp.sin(ang)
    z_nope = jnp.zeros((s, MLA_NOPE), jnp.float32)
    z_pad = jnp.zeros((s, LANES - MLA_NOPE - MLA_ROPE), jnp.float32)
    cos_k = jnp.concatenate([z_nope, cos, cos, z_pad], axis=1)
    sin_k = jnp.concatenate([z_nope, sin, sin, z_pad], axis=1)
    cos_t, sin_t = cos.T * q_scale, sin.T * q_scale
    cos_q_t = jnp.concatenate([jnp.full((MLA_NOPE, s), q_scale, jnp.float32), cos_t, cos_t, z_pad.T], axis=0)
    sin_q_t = jnp.concatenate([z_nope.T, sin_t, sin_t, z_pad.T], axis=0)
    return cos_q_t, sin_q_t, cos_k, sin_k


def kernel(x, mem, ln_in_g, ln_in_b, mem_ln_g, mem_ln_b, w_in, b_forget, mla_q_norm_g, w_mla_q_up,
           mla_kv_norm_g, w_mla_kv_up, w_mem_kv, w_out, ln_g, ln_b):
    bsz, s, d = x.shape
    depth = w_in.shape[0]
    alpha = (2 * depth) ** 0.25
    tm = min(1024, s)
    tm_out = min(1024, s)
    tq_sm = min(2048, s)
    tq_sb = min(1024, s)
    tk_sm = min(512, tq_sm // 2)
    tk_sb = min(256, tq_sb // 2)
    mla_scale = (MLA_NOPE + MLA_ROPE) ** -0.5
    tabs = _rope_tables(s, mla_scale * LOG2E)

    mkv = _mem_kv_call(mem, mem_ln_g, mem_ln_b, w_mem_kv.astype(jnp.bfloat16))

    h = x
    for l in range(depth):
        w_packed, w_t = _pack_w_in(w_in[l])
        w_k_up, w_v_up_t = _pack_mla_kv_up(w_mla_kv_up[l])
        outs = _proj_call(
            h, (ln_in_g, ln_in_b), w_packed, w_t, _pack_forget_bias(b_forget[l]),
            mla_q_norm_g[l].reshape(1, -1), _pack_mla_q_up_t(w_mla_q_up[l]),
            mla_kv_norm_g[l].reshape(1, -1), w_k_up, w_v_up_t,
            tabs, mkv[l], tm, pre_ln=(l == 0))
        if l == 0:
            h, outs = outs[0], outs[1:]
        (fqt, fk, fvt, sbq, sbk, sbv, mqt, mk, mvt, o_mem, gate) = outs
        o_fox = _softmax_attn_call(fqt, fk, fvt, fox=True, tq=tq_sm, tk=tk_sm, name="fox_attn")
        o_sb = _sb_attn_call(sbq, sbk, sbv, tq=tq_sb, tk=tk_sb)
        o_mla = _softmax_attn_call(mqt, mk, mvt, fox=False, tq=tq_sm, tk=tk_sm, name="mla_attn")
        n = bsz * s
        h = _out_call(o_fox.reshape(n, -1), o_sb.reshape(n, -1), o_mla.reshape(n, -1), o_mem.reshape(n, -1),
                      gate.reshape(n, -1), h.reshape(n, d), w_out[l].astype(jnp.bfloat16),
                      ln_g[l], ln_b[l], alpha, tm_out).reshape(bsz, s, d)
    return h
```

```python
import functools

import jax
import jax.numpy as jnp
from jax import lax
from jax.experimental import pallas as pl
from jax.experimental.pallas import tpu as pltpu

N_HEADS = 4
HEAD_DIM = 64
GROUP_W = N_HEADS * HEAD_DIM
MLA_NOPE = 64
MLA_ROPE = 32
MLA_Q_RANK = 256
MLA_KV_RANK = 128
ROPE_THETA = 10000.0
LN_EPS = 1e-5
RMS_EPS = 1e-6
LOG2E = 1.4426950408889634
EXP2_CLAMP = 126.0

LANES = 128
VMEM_LIMIT_BYTES = 56 * 1024 * 1024

C_GATE = 0
C_CQ = 1024
C_CKV = 1280
C_MISC = 1408
C_MQ = 1536
C_KRP = 1792
C_KRRP = 1920
C_FK = 2048
C_SQ, C_SK, C_SV = 2304, 2560, 2816
W_COLS = 3072
R_FQ, R_FV = 0, 256
WT_ROWS = 512
FORGET_DUP = 6
FORGET_STRIDE = 8
VT_ROWS = HEAD_DIM + 16
PROJ_SUB_ROWS = 512
NT_DIMS = (((1,), (1,)), ((), ()))


def _cparams(n_grid):
    return pltpu.CompilerParams(dimension_semantics=("arbitrary",) * n_grid,
                                vmem_limit_bytes=VMEM_LIMIT_BYTES)


def _split3(x):
    p0 = x.astype(jnp.bfloat16).astype(jnp.float32)
    r1 = x - p0
    p1 = r1.astype(jnp.bfloat16).astype(jnp.float32)
    p2 = (r1 - p1).astype(jnp.bfloat16).astype(jnp.float32)
    return p0, p1, p2


def _layer_norm_rows(x, g, b):
    mu = jnp.mean(x, axis=-1, keepdims=True)
    xc = x - mu
    var = jnp.mean(xc * xc, axis=-1, keepdims=True)
    return xc * lax.rsqrt(var + LN_EPS) * g + b


def _rms_norm_rows(x, g):
    ms = jnp.mean(x * x, axis=-1, keepdims=True)
    return x * lax.rsqrt(ms + RMS_EPS) * g


def _mem_kv_kernel(mem_ref, g_ref, b_ref, w_ref, o_ref):
    mem_n = _layer_norm_rows(mem_ref[0], g_ref[...], b_ref[...]).astype(jnp.bfloat16)
    for l in range(w_ref.shape[0]):
        o_ref[l, 0] = jnp.dot(mem_n, w_ref[l], preferred_element_type=jnp.float32).astype(jnp.bfloat16)


def _mem_kv_call(mem, g, b, w_mem_kv_bf16):
    bsz, m, d = mem.shape
    depth, _, c = w_mem_kv_bf16.shape
    return pl.pallas_call(
        _mem_kv_kernel,
        grid=(bsz,),
        in_specs=[pl.BlockSpec((1, m, d), lambda i: (i, 0, 0)),
                  pl.BlockSpec((1, d), lambda i: (0, 0)),
                  pl.BlockSpec((1, d), lambda i: (0, 0)),
                  pl.BlockSpec((depth, d, c), lambda i: (0, 0, 0))],
        out_specs=pl.BlockSpec((depth, 1, m, c), lambda i: (0, i, 0, 0)),
        out_shape=jax.ShapeDtypeStruct((depth, bsz, m, c), jnp.bfloat16),
        compiler_params=_cparams(1),
        name="mem_kv",
    )(mem, g.reshape(1, d), b.reshape(1, d), w_mem_kv_bf16)


def _proj_kernel(h_ref, lng_ref, lnb_ref, w_ref, wt_ref, bfor_ref, gq_ref, wqupt_ref, gkv_ref, wkup_ref,
                 wvupt_ref, cqt_ref, sqt_ref, ck_ref, sk_ref, *rest, pre_ln):
    hout_ref = rest[0] if pre_ln else None
    (fqt_ref, fk_ref, fvt_ref, sbq_ref, sbk_ref, sbv_ref,
     mqt_ref, mk_ref, mvt_ref, omem_ref, gate_ref, carry_ref) = rest[1:] if pre_ln else rest
    i = pl.program_id(1)
    tm = h_ref.shape[1]
    bf16, f32 = jnp.bfloat16, jnp.float32
    head_c = (HEAD_DIM ** -0.5) * LOG2E

    @pl.when(i == 0)
    def _():
        carry_ref[...] = jnp.zeros_like(carry_ref)

    def sub_block(r0, nr):
        rs = slice(r0, r0 + nr)
        h = h_ref[0, rs]
        if pre_ln:
            h = _layer_norm_rows(h, lng_ref[...], lnb_ref[...])
            hout_ref[0, rs] = h
        hb = h.astype(bf16)
        proj = jnp.dot(hb, w_ref[...], preferred_element_type=f32)
        proj_t = lax.dot_general(wt_ref[...], hb, NT_DIMS, preferred_element_type=f32)

        def cols(c0, n):
            return proj[:, c0:c0 + n]

        def store_values_t(vt, out_ref):
            for hh in range(N_HEADS):
                out_ref[0, hh, :HEAD_DIM, rs] = vt[HEAD_DIM * hh:HEAD_DIM * (hh + 1), :].astype(bf16)
                out_ref[0, hh, HEAD_DIM:, rs] = jnp.ones((VT_ROWS - HEAD_DIM, nr), bf16)

        lane = lax.broadcasted_iota(jnp.int32, (nr, LANES), 1)
        sub = lane % FORGET_STRIDE
        used = (lane < N_HEADS * FORGET_STRIDE) & (sub < FORGET_DUP)
        xf = cols(C_MISC, LANES) + bfor_ref[...]
        log_f = jnp.minimum(xf, 0.0) - jnp.log1p(jnp.exp(-jnp.abs(xf)))
        log_f = jnp.where(used, log_f, 0.0)
        row = lax.broadcasted_iota(jnp.int32, (nr, nr), 0)
        col = lax.broadcasted_iota(jnp.int32, (nr, nr), 1)
        tril = jnp.where(col <= row, 1.0, 0.0).astype(bf16)
        parts = jnp.concatenate([p.astype(bf16) for p in _split3(log_f)], axis=1)
        csum = jnp.dot(tril, parts, preferred_element_type=f32)
        f_cum = (csum[:, :LANES] + csum[:, LANES:2 * LANES]) + csum[:, 2 * LANES:] + carry_ref[...]
        carry_ref[...] = f_cum[nr - 1:nr, :]
        p0, p1, p2 = _split3(f_cum * LOG2E)
        bias_q = jnp.where(sub == 0, p0, jnp.where(sub == 1, p1, jnp.where(sub == 2, p2, 1.0)))
        bias_q_t = jnp.where(used, bias_q, 0.0).T.astype(bf16)
        bias_k = jnp.where(sub == 3, -p0, jnp.where(sub == 4, -p1, jnp.where(sub == 5, -p2, 1.0)))
        bias_k = jnp.where(used, bias_k, 0.0).astype(bf16)

        fk = cols(C_FK, GROUP_W)
        for p in range(2):
            fqt_ref[0, p, :LANES, rs] = (proj_t[R_FQ + LANES * p:R_FQ + LANES * (p + 1), :] * head_c).astype(bf16)
            fqt_ref[0, p, LANES:, rs] = bias_q_t
            fk_ref[0, rs, 2 * LANES * p:2 * LANES * p + LANES] = fk[:, LANES * p:LANES * (p + 1)].astype(bf16)
            fk_ref[0, rs, 2 * LANES * p + LANES:2 * LANES * (p + 1)] = bias_k
        store_values_t(proj_t[R_FV:R_FV + GROUP_W, :], fvt_ref)

        sbq_ref[0, rs] = (cols(C_SQ, GROUP_W) * head_c).astype(bf16)
        sbk_ref[0, rs] = cols(C_SK, GROUP_W).astype(bf16)
        sbv_ref[0, rs] = cols(C_SV, GROUP_W).astype(bf16)

        cqn = _rms_norm_rows(cols(C_CQ, MLA_Q_RANK), gq_ref[...]).astype(bf16)
        q_up_t = lax.dot_general(wqupt_ref[...], cqn, NT_DIMS, preferred_element_type=f32)
        ckvn = _rms_norm_rows(cols(C_CKV, MLA_KV_RANK), gkv_ref[...]).astype(bf16)
        k_up = jnp.dot(ckvn, wkup_ref[...], preferred_element_type=f32)
        k_rope = cols(C_KRP, LANES) * ck_ref[rs] + cols(C_KRRP, LANES) * sk_ref[rs]
        nq = N_HEADS * LANES
        for hh in range(N_HEADS):
            sl = slice(LANES * hh, LANES * (hh + 1))
            mqt_ref[0, hh, :, rs] = (q_up_t[sl, :] * cqt_ref[:, rs]
                                     + q_up_t[nq + LANES * hh:nq + LANES * (hh + 1), :] * sqt_ref[:, rs]).astype(bf16)
            mk_ref[0, rs, sl] = (k_up[:, sl] + k_rope).astype(bf16)
        store_values_t(lax.dot_general(wvupt_ref[...], ckvn, NT_DIMS, preferred_element_type=f32), mvt_ref)

        omem_ref[0, rs] = (cols(C_MQ, GROUP_W) * head_c).astype(bf16)

        for c in range(4):
            g = cols(C_GATE + GROUP_W * c, GROUP_W)
            gate_ref[0, rs, GROUP_W * c:GROUP_W * (c + 1)] = (g / (1.0 + jnp.exp(-g))).astype(bf16)

    n_sub = max(tm // PROJ_SUB_ROWS, 1)
    for j in range(n_sub):
        sub_block(j * (tm // n_sub), tm // n_sub)


def _proj_call(h, ln_gb, w_packed, w_t, bfor_row, gq, wqupt, gkv, wkup, wvupt, tabs, tm, pre_ln):
    bsz, s, d = h.shape
    ln_g, ln_b = ln_gb
    cqt, sqt, ck, sk = tabs
    bf16 = jnp.bfloat16
    full2 = lambda shape: pl.BlockSpec(shape, lambda b, i: (0, 0))
    row_blk = lambda c: pl.BlockSpec((1, tm, c), lambda b, i: (b, i, 0))
    tab_blk = pl.BlockSpec((tm, LANES), lambda b, i: (i, 0))
    tab_t_blk = pl.BlockSpec((LANES, tm), lambda b, i: (0, i))
    rows = lambda c, dt: (row_blk(c), jax.ShapeDtypeStruct((bsz, s, c), dt))
    feat = lambda n, r: (pl.BlockSpec((1, n, r, tm), lambda b, i: (b, 0, 0, i)),
                         jax.ShapeDtypeStruct((bsz, n, r, s), bf16))
    outs = [feat(2, 2 * LANES), rows(2 * GROUP_W, bf16), feat(N_HEADS, VT_ROWS),
            rows(GROUP_W, bf16), rows(GROUP_W, bf16), rows(GROUP_W, bf16),
            feat(N_HEADS, LANES), rows(2 * GROUP_W, bf16), feat(N_HEADS, VT_ROWS),
            rows(GROUP_W, bf16), rows(4 * GROUP_W, bf16)]
    if pre_ln:
        outs = [rows(d, jnp.float32)] + outs
    return pl.pallas_call(
        functools.partial(_proj_kernel, pre_ln=pre_ln),
        grid=(bsz, s // tm),
        in_specs=[row_blk(d), full2((1, d)), full2((1, d)),
                  full2(w_packed.shape), full2(w_t.shape),
                  full2((1, LANES)),
                  full2((1, MLA_Q_RANK)), full2(wqupt.shape),
                  full2((1, MLA_KV_RANK)), full2(wkup.shape), full2(wvupt.shape),
                  tab_t_blk, tab_t_blk, tab_blk, tab_blk],
        out_specs=[spec for spec, _ in outs],
        out_shape=[shape for _, shape in outs],
        scratch_shapes=[pltpu.VMEM((1, LANES), jnp.float32)],
        compiler_params=_cparams(2),
        name="proj",
    )(h, ln_g.reshape(1, d), ln_b.reshape(1, d), w_packed, w_t, bfor_row, gq, wqupt, gkv, wkup, wvupt,
      cqt, sqt, ck, sk)


def _softmax_attn_kernel(qt_ref, k_ref, vt_ref, o_ref, q_sc, s_sc, m_sc, acc_sc, *, fox, tq, tk):
    pair = pl.program_id(1)
    qi = pl.program_id(2)
    bf16, f32 = jnp.bfloat16, jnp.float32
    kq = q_sc.shape[1]
    for hh in range(2):
        if fox:
            feat = lax.broadcasted_iota(jnp.int32, (kq, tq), 0)
            lo = LANES + FORGET_STRIDE * (2 * pair + hh)
            keep = ((feat >= HEAD_DIM * hh) & (feat < HEAD_DIM * (hh + 1))) | (
                (feat >= lo) & (feat < lo + FORGET_STRIDE))
            q_sc[hh] = jnp.where(keep, qt_ref[0, 0], jnp.zeros((kq, tq), bf16))
        else:
            q_sc[hh] = qt_ref[0, hh]
    m_sc[...] = jnp.full(m_sc.shape, -jnp.inf, f32)
    acc_sc[...] = jnp.zeros(acc_sc.shape, f32)

    def scores(kb, slot, c0=0):
        k0 = pl.multiple_of(kb * tk, tk)
        for hh in range(2):
            k = k_ref[0, pl.ds(k0, tk), :] if fox else k_ref[0, pl.ds(k0, tk), LANES * hh:LANES * (hh + 1)]
            s_sc[slot, hh, :, c0:] = jnp.dot(k, q_sc[hh, :, c0:], preferred_element_type=f32)

    def update(kb, slot, c0=0, nc=tq, triangle=False):
        k0 = pl.multiple_of(kb * tk, tk)
        qs = slice(c0, c0 + nc)
        if triangle:
            causal = (lax.broadcasted_iota(jnp.int32, (tk, nc), 0)
                      <= lax.broadcasted_iota(jnp.int32, (tk, nc), 1))
        for hh in range(2):
            s = s_sc[slot, hh, :, qs]
            if triangle:
                s = jnp.where(causal, s, -jnp.inf)
            m_prev = m_sc[hh, :, qs]
            m_new = jnp.maximum(m_prev, jnp.max(s, axis=0, keepdims=True))
            alpha = jnp.exp2(m_prev - m_new)
            p = jnp.exp2(s - m_new)
            acc_sc[hh, :, qs] = (alpha * acc_sc[hh, :, qs]
                                 + jnp.dot(vt_ref[0, hh, :, pl.ds(k0, tk)], p.astype(bf16),
                                           preferred_element_type=f32))
            m_sc[hh, :, qs] = m_new

    n_diag = tq // tk
    unroll = 4 if n_diag % 4 == 0 else 2
    assert n_diag % unroll == 0
    n_full = qi * n_diag
    scores(0, 0)

    def body(j, c):
        for u in range(unroll):
            kb = unroll * j + u
            scores(kb + 1, (u + 1) % 2)
            update(kb, u % 2)
        return c

    lax.fori_loop(0, n_full // unroll, body, 0)
    for d in range(n_diag):
        if d + 1 < n_diag:
            scores(n_full + d + 1, (d + 1) % 2, c0=(d + 1) * tk)
        update(n_full + d, d % 2, c0=d * tk, nc=tk, triangle=True)
        if d + 1 < n_diag:
            update(n_full + d, d % 2, c0=(d + 1) * tk, nc=tq - (d + 1) * tk)
    out_t = jnp.concatenate([acc_sc[hh, :HEAD_DIM] / acc_sc[hh, HEAD_DIM:HEAD_DIM + 1] for hh in range(2)],
                            axis=0)
    o_ref[0] = out_t.T.astype(o_ref.dtype)


def _softmax_attn_call(qt, k, vt, *, fox, tq, tk, name):
    bsz, nqt, kq, s = qt.shape
    kw = 2 * LANES
    kern = functools.partial(_softmax_attn_kernel, fox=fox, tq=tq, tk=tk)
    return pl.pallas_call(
        kern,
        grid=(bsz, 2, s // tq),
        in_specs=[pl.BlockSpec((1, nqt // 2, kq, tq), lambda b, p, i: (b, p, 0, i)),
                  pl.BlockSpec((1, s, kw), lambda b, p, i: (b, 0, p)),
                  pl.BlockSpec((1, 2, VT_ROWS, s), lambda b, p, i: (b, p, 0, 0))],
        out_specs=pl.BlockSpec((1, tq, LANES), lambda b, p, i: (b, i, p)),
        out_shape=jax.ShapeDtypeStruct((bsz, s, GROUP_W), jnp.bfloat16),
        scratch_shapes=[pltpu.VMEM((2, kq, tq), jnp.bfloat16),
                        pltpu.VMEM((2, 2, tk, tq), jnp.float32),
                        pltpu.VMEM((2, 1, tq), jnp.float32),
                        pltpu.VMEM((2, VT_ROWS, tq), jnp.float32)],
        compiler_params=_cparams(3),
        name=name,
    )(qt, k, vt)


def _sb_attn_kernel(q_ref, k_ref, v_ref, o_ref, q_sc, z_sc, c_sc, acc_sc, *, tq, tk):
    qi = pl.program_id(2)
    bf16, f32 = jnp.bfloat16, jnp.float32
    lane_q = lax.broadcasted_iota(jnp.int32, (tq, LANES), 1)
    rj = lax.broadcasted_iota(jnp.int32, (tk, tk), 0)
    cs = lax.broadcasted_iota(jnp.int32, (tk, tk), 1)
    upper = jnp.where(rj >= cs, 1.0, 0.0).astype(bf16)
    for hh in range(2):
        in_head = (lane_q >= HEAD_DIM * hh) & (lane_q < HEAD_DIM * (hh + 1))
        q_sc[hh] = jnp.where(in_head, q_ref[0], jnp.zeros((tq, LANES), bf16))
    c_sc[...] = jnp.zeros(c_sc.shape, f32)
    acc_sc[...] = jnp.zeros(acc_sc.shape, f32)

    def scores(kb, slot, r0=0):
        k0 = pl.multiple_of(kb * tk, tk)
        k = k_ref[0, pl.ds(k0, tk), :]
        for hh in range(2):
            z_sc[slot, hh, r0:] = lax.dot_general(q_sc[hh, r0:], k, NT_DIMS, preferred_element_type=f32)

    def update(kb, slot, r0=0, nr=tq, triangle=False):
        k0 = pl.multiple_of(kb * tk, tk)
        v = v_ref[0, pl.ds(k0, tk), :]
        rows = slice(r0, r0 + nr)
        if triangle:
            valid = (lax.broadcasted_iota(jnp.int32, (nr, tk), 1)
                     < lax.broadcasted_iota(jnp.int32, (nr, tk), 0))
        for hh in range(2):
            z = z_sc[slot, hh, rows]
            sp = jnp.maximum(jnp.log2(1.0 + jnp.exp2(jnp.minimum(z, EXP2_CLAMP))), z)
            if triangle:
                sp = jnp.where(valid, sp, 0.0)
            csum = (jnp.dot(sp.astype(bf16), upper, preferred_element_type=f32)
                    + jnp.tile(c_sc[hh, rows], (1, tk // LANES)))
            w = jnp.exp2(z - csum)
            if triangle:
                w = jnp.where(valid, w, 0.0)
            acc_sc[hh, rows] += jnp.dot(w.astype(bf16), v, preferred_element_type=f32)
            c_sc[hh, rows] = jnp.broadcast_to(csum[:, 0:1], (nr, LANES))

    n_diag = tq // tk
    unroll = 4 if n_diag % 4 == 0 else 2
    assert n_diag % unroll == 0
    n_full = qi * n_diag
    scores(n_full + n_diag - 1, 1, r0=(n_diag - 1) * tk)
    for d in reversed(range(n_diag)):
        kb = n_full + d
        scores(jnp.maximum(kb - 1, 0), (d + 1) % 2, r0=max(d - 1, 0) * tk)
        update(kb, d % 2, r0=d * tk, nr=tk, triangle=True)
        if d + 1 < n_diag:
            update(kb, d % 2, r0=(d + 1) * tk, nr=tq - (d + 1) * tk)

    def body(n, c):
        top = n_full - 1 - unroll * n
        for u in range(unroll):
            kb = top - u
            scores(jnp.maximum(kb - 1, 0), u % 2)
            update(kb, (u + 1) % 2)
        return c

    lax.fori_loop(0, n_full // unroll, body, 0)
    o_ref[0] = jnp.where(lane_q < HEAD_DIM, acc_sc[0], acc_sc[1]).astype(o_ref.dtype)


def _sb_attn_call(q, k, v, *, tq, tk):
    bsz, s, _ = q.shape
    kern = functools.partial(_sb_attn_kernel, tq=tq, tk=tk)
    return pl.pallas_call(
        kern,
        grid=(bsz, 2, s // tq),
        in_specs=[pl.BlockSpec((1, tq, LANES), lambda b, p, i: (b, i, p)),
                  pl.BlockSpec((1, s, LANES), lambda b, p, i: (b, 0, p)),
                  pl.BlockSpec((1, s, LANES), lambda b, p, i: (b, 0, p))],
        out_specs=pl.BlockSpec((1, tq, LANES), lambda b, p, i: (b, i, p)),
        out_shape=jax.ShapeDtypeStruct((bsz, s, GROUP_W), jnp.bfloat16),
        scratch_shapes=[pltpu.VMEM((2, tq, LANES), jnp.bfloat16),
                        pltpu.VMEM((2, 2, tq, tk), jnp.float32),
                        pltpu.VMEM((2, tq, LANES), jnp.float32),
                        pltpu.VMEM((2, tq, LANES), jnp.float32)],
        compiler_params=_cparams(3),
        name="sb_attn",
    )(q, k, v)


def _out_kernel(of_ref, os_ref, om_ref, mq_ref, mkv_ref, gate_ref, h_ref, w_ref, g_ref, b_ref, o_ref, *, alpha):
    bf16, f32 = jnp.bfloat16, jnp.float32
    tm = mq_ref.shape[0]
    lane_h = lax.broadcasted_iota(jnp.int32, (tm, LANES), 1)
    o_mem = []
    for p in range(2):
        qp = mq_ref[:, LANES * p:LANES * (p + 1)]
        kp = mkv_ref[0, :, LANES * p:LANES * (p + 1)]
        vp = mkv_ref[0, :, GROUP_W + LANES * p:GROUP_W + LANES * (p + 1)]
        outs = []
        for hh in range(2):
            in_head = (lane_h >= HEAD_DIM * hh) & (lane_h < HEAD_DIM * (hh + 1))
            qh = jnp.where(in_head, qp, jnp.zeros_like(qp))
            s = lax.dot_general(qh, kp, NT_DIMS, preferred_element_type=f32)
            m = jnp.max(s, axis=-1, keepdims=True)
            e = jnp.exp2(s - m)
            pr = e / jnp.sum(e, axis=-1, keepdims=True)
            outs.append(jnp.dot(pr.astype(bf16), vp, preferred_element_type=f32))
        o_mem.append(jnp.where(lane_h < HEAD_DIM, outs[0], outs[1]).astype(bf16))
    mixed = jnp.concatenate([of_ref[...], os_ref[...], om_ref[...]] + o_mem, axis=1)
    y = jnp.dot(mixed * gate_ref[...], w_ref[...], preferred_element_type=f32)
    o_ref[...] = _layer_norm_rows(alpha * h_ref[...] + y, g_ref[...], b_ref[...])


def _out_call(o_fox, o_sb, o_mla, mem_q, mkv_l, gate, h2d, w_out_bf16, g, b, alpha, tm):
    n, d = h2d.shape
    gw = o_fox.shape[1]
    bsz, m, _ = mkv_l.shape
    steps_per_batch = n // bsz // tm
    kern = functools.partial(_out_kernel, alpha=alpha)
    blk = lambda c: pl.BlockSpec((tm, c), lambda i: (i, 0))
    return pl.pallas_call(
        kern,
        grid=(n // tm,),
        in_specs=[blk(gw), blk(gw), blk(gw), blk(gw),
                  pl.BlockSpec((1, m, 2 * gw), lambda i: (i // steps_per_batch, 0, 0)),
                  blk(4 * gw), blk(d),
                  pl.BlockSpec(w_out_bf16.shape, lambda i: (0, 0)),
                  pl.BlockSpec((1, d), lambda i: (0, 0)),
                  pl.BlockSpec((1, d), lambda i: (0, 0))],
        out_specs=blk(d),
        out_shape=jax.ShapeDtypeStruct((n, d), jnp.float32),
        compiler_params=_cparams(1),
        name="out_proj",
    )(o_fox, o_sb, o_mla, mem_q, mkv_l, gate, h2d, w_out_bf16, g.reshape(1, d), b.reshape(1, d))


def _pack_w_in(w):
    d = w.shape[0]
    w = w.astype(jnp.bfloat16)
    o = 0
    cols = {}
    for name, width in (("fq", 256), ("fk", 256), ("fv", 256), ("fl", 4), ("sq", 256), ("sk", 256),
                        ("sv", 256), ("cq", 256), ("ckv", 128), ("kr", 32), ("mq", 256), ("gate", 1024)):
        cols[name] = w[:, o:o + width]
        o += width
    z = lambda n: jnp.zeros((d, n), w.dtype)
    half = MLA_ROPE // 2
    kr = cols["kr"]
    kr_rot = jnp.concatenate([-kr[:, half:], kr[:, :half]], axis=1)
    krp = jnp.concatenate([z(MLA_NOPE), kr, z(LANES - MLA_NOPE - MLA_ROPE)], axis=1)
    krrp = jnp.concatenate([z(MLA_NOPE), kr_rot, z(LANES - MLA_NOPE - MLA_ROPE)], axis=1)
    rep = jnp.repeat(cols["fl"], FORGET_STRIDE, axis=1)
    keep = (jnp.arange(N_HEADS * FORGET_STRIDE) % FORGET_STRIDE) < FORGET_DUP
    misc = [jnp.where(keep[None, :], rep, jnp.zeros_like(rep)), z(LANES - N_HEADS * FORGET_STRIDE)]
    packed = jnp.concatenate([cols["gate"], cols["cq"], cols["ckv"]] + misc
                             + [cols["mq"], krp, krrp, cols["fk"], cols["sq"], cols["sk"], cols["sv"]], axis=1)
    assert packed.shape[1] == W_COLS
    w_t = jnp.concatenate([cols["fq"], cols["fv"]], axis=1).T
    assert w_t.shape[0] == WT_ROWS
    return packed.astype(jnp.bfloat16), w_t.astype(jnp.bfloat16)


def _pack_forget_bias(b_forget_l):
    row = jnp.zeros((LANES,), jnp.float32)
    for hh in range(N_HEADS):
        row = row.at[FORGET_STRIDE * hh:FORGET_STRIDE * hh + FORGET_DUP].set(b_forget_l[hh])
    return row.reshape(1, LANES)


def _pack_mla_q_up_t(w):
    r = w.shape[0]
    per = MLA_NOPE + MLA_ROPE
    half = MLA_ROPE // 2
    z = lambda n: jnp.zeros((r, n), w.dtype)
    plain, rot = [], []
    for hh in range(N_HEADS):
        nope = w[:, per * hh:per * hh + MLA_NOPE]
        rope = w[:, per * hh + MLA_NOPE:per * (hh + 1)]
        rope_rot = jnp.concatenate([-rope[:, half:], rope[:, :half]], axis=1)
        plain += [nope, rope, z(LANES - per)]
        rot += [z(MLA_NOPE), rope_rot, z(LANES - per)]
    return jnp.concatenate(plain + rot, axis=1).T.astype(jnp.bfloat16)


def _pack_mla_kv_up(w):
    r = w.shape[0]
    per = MLA_NOPE + HEAD_DIM
    z = jnp.zeros((r, LANES - MLA_NOPE), w.dtype)
    ks, vs = [], []
    for hh in range(N_HEADS):
        ks += [w[:, per * hh:per * hh + MLA_NOPE], z]
        vs.append(w[:, per * hh + MLA_NOPE:per * (hh + 1)])
    return (jnp.concatenate(ks, axis=1).astype(jnp.bfloat16),
            jnp.concatenate(vs, axis=1).T.astype(jnp.bfloat16))


def _rope_tables(s, q_scale):
    half = MLA_ROPE // 2
    inv_freq = ROPE_THETA ** (-jnp.arange(half, dtype=jnp.float32) / half)
    ang = jnp.arange(s).astype(jnp.float32)[:, None] * inv_freq[None, :]
    cos, sin = jnp.cos(ang), jnp.sin(ang)
    z_nope = jnp.zeros((s, MLA_NOPE), jnp.float32)
    z_pad = jnp.zeros((s, LANES - MLA_NOPE - MLA_ROPE), jnp.float32)
    cos_k = jnp.concatenate([z_nope, cos, cos, z_pad], axis=1)
    sin_k = jnp.concatenate([z_nope, sin, sin, z_pad], axis=1)
    cos_t, sin_t = cos.T * q_scale, sin.T * q_scale
    cos_q_t = jnp.concatenate([jnp.full((MLA_NOPE, s), q_scale, jnp.float32), cos_t, cos_t, z_pad.T], axis=0)
    sin_q_t = jnp.concatenate([z_nope.T, sin_t, sin_t, z_pad.T], axis=0)
    return cos_q_t, sin_q_t, cos_k, sin_k


def kernel(x, mem, ln_in_g, ln_in_b, mem_ln_g, mem_ln_b, w_in, b_forget, mla_q_norm_g, w_mla_q_up,
           mla_kv_norm_g, w_mla_kv_up, w_mem_kv, w_out, ln_g, ln_b):
    bsz, s, d = x.shape
    depth = w_in.shape[0]
    alpha = (2 * depth) ** 0.25
    tm = min(1024, s)
    tm_out = min(1024, s)
    tq_sm = min(2048, s)
    tq_sb = min(1024, s)
    tk_sm = min(512, tq_sm // 2)
    tk_sb = min(256, tq_sb // 2)
    mla_scale = (MLA_NOPE + MLA_ROPE) ** -0.5
    tabs = _rope_tables(s, mla_scale * LOG2E)

    mkv = _mem_kv_call(mem, mem_ln_g, mem_ln_b, w_mem_kv.astype(jnp.bfloat16))

    h = x
    for l in range(depth):
        w_packed, w_t = _pack_w_in(w_in[l])
        w_k_up, w_v_up_t = _pack_mla_kv_up(w_mla_kv_up[l])
        outs = _proj_call(
            h, (ln_in_g, ln_in_b), w_packed, w_t, _pack_forget_bias(b_forget[l]),
            mla_q_norm_g[l].reshape(1, -1), _pack_mla_q_up_t(w_mla_q_up[l]),
            mla_kv_norm_g[l].reshape(1, -1), w_k_up, w_v_up_t,
            tabs, tm, pre_ln=(l == 0))
        if l == 0:
            h, outs = outs[0], outs[1:]
        (fqt, fk, fvt, sbq, sbk, sbv, mqt, mk, mvt, mem_q, gate) = outs
        o_fox = _softmax_attn_call(fqt, fk, fvt, fox=True, tq=tq_sm, tk=tk_sm, name="fox_attn")
        o_sb = _sb_attn_call(sbq, sbk, sbv, tq=tq_sb, tk=tk_sb)
        o_mla = _softmax_attn_call(mqt, mk, mvt, fox=False, tq=tq_sm, tk=tk_sm, name="mla_attn")
        n = bsz * s
        h = _out_call(o_fox.reshape(n, -1), o_sb.reshape(n, -1), o_mla.reshape(n, -1), mem_q.reshape(n, -1),
                      mkv[l], gate.reshape(n, -1), h.reshape(n, d), w_out[l].astype(jnp.bfloat16),
                      ln_g[l], ln_b[l], alpha, tm_out).reshape(bsz, s, d)
    return h
```
